```python
import math
import jax
import jax.numpy as jnp
from jax import lax
import numpy as np

D_MODEL = 2048
BATCH = 4
SEQ = 2048
DEPTH = 2
DEC_BATCH = 128
DEC_SEQ = 4
PAST_LEN = 8192
PAGE_SIZE = 128

N_A_LAYERS = DEPTH // 2
N_B_LAYERS = DEPTH - N_A_LAYERS
D_RNN = 2688
N_RNN_BLOCKS = 16
RNN_BLOCK = D_RNN // N_RNN_BLOCKS
CONV_WIDTH = 4
LRU_C = 8.0
N_HEADS = 16
Q_LORA_RANK = 512
KV_LORA_RANK = 512
QK_NOPE_DIM = 128
QK_ROPE_DIM = 64
V_HEAD_DIM = 128
ROPE_THETA = 10000.0
ATTN_SCALE = 1.0 / math.sqrt(QK_NOPE_DIM + QK_ROPE_DIM)
Q_BLOCK = 128
N_GROUPS = 8
EXPERTS_PER_GROUP = 8
N_EXPERTS = N_GROUPS * EXPERTS_PER_GROUP
TOP_K = 2
D_EXPERT = 512
MOE_BLOCK = 128
MOD_SCALE = 0.3
EPS = 1e-6

kernel_name = "yoco_rglru_mla_hier_moe_step"


def rms_norm(x, gain):
    xf = x.astype(jnp.float32)
    y = xf * lax.rsqrt(jnp.mean(xf * xf, axis=-1, keepdims=True) + EPS)
    return (y * gain.astype(jnp.float32)).astype(x.dtype)


def modulate(xn, shift, scale):
    return xn * (1.0 + scale[:, None, :]) + shift[:, None, :]


def apply_rope(x, pos):
    half = QK_ROPE_DIM // 2
    inv_freq = ROPE_THETA ** (-jnp.arange(half, dtype=jnp.float32) / half)
    ang = pos.astype(jnp.float32)[:, None] * inv_freq
    ang = ang.reshape((ang.shape[0],) + (1,) * (x.ndim - 3) + (half,))
    cos, sin = jnp.cos(ang), jnp.sin(ang)
    x1 = x[..., :half].astype(jnp.float32)
    x2 = x[..., half:].astype(jnp.float32)
    return jnp.concatenate([x1 * cos - x2 * sin, x1 * sin + x2 * cos], axis=-1).astype(x.dtype)


def causal_conv(u, buf, w, b):
    t = u.shape[1]
    up = jnp.concatenate([buf.astype(u.dtype), u], axis=1)
    y = b
    for k in range(CONV_WIDTH):
        y = y + w[k] * up[:, k:k + t]
    return y, up[:, t:]


def rg_lru(u, h0, w_gates, b_gates, lam):
    bsz, t, _ = u.shape
    ub = u.reshape(bsz, t, N_RNN_BLOCKS, RNN_BLOCK)
    g = (jnp.einsum("btnk,nkj->btnj", ub, w_gates) + b_gates).astype(jnp.float32)
    r = jax.nn.sigmoid(g[..., :RNN_BLOCK]).reshape(bsz, t, D_RNN)
    i = jax.nn.sigmoid(g[..., RNN_BLOCK:]).reshape(bsz, t, D_RNN)
    log_a = -LRU_C * r * jax.nn.softplus(-lam.astype(jnp.float32))
    a = jnp.exp(log_a)
    x_in = jnp.sqrt(-jnp.expm1(2.0 * log_a)) * (i * u.astype(jnp.float32))

    def step(h, inp):
        a_t, x_t = inp
        h = a_t * h + x_t
        return h, h

    h_last, hs = lax.scan(step, h0.astype(jnp.float32),
                          (jnp.swapaxes(a, 0, 1), jnp.swapaxes(x_in, 0, 1)))
    return jnp.swapaxes(hs, 0, 1).astype(u.dtype), h_last.astype(u.dtype)


def recurrent_mixer(xn, conv_buf, h0, w_in, conv_w, conv_b, w_gates, b_gates, lam, w_out):
    proj = xn @ w_in
    gate, u = proj[..., :D_RNN], proj[..., D_RNN:]
    u, conv_new = causal_conv(u, conv_buf, conv_w, conv_b)
    h, h_new = rg_lru(u, h0, w_gates, b_gates, lam)
    return (h * jax.nn.gelu(gate)) @ w_out, conv_new, h_new


def shared_latent_kv(h, c, pos, kv_mod_w, kv_mod_b, kv_norm, kv_w_dkv, kv_latent_norm):
    shift, scale = jnp.split(c @ kv_mod_w + kv_mod_b, 2, axis=-1)
    hn = modulate(rms_norm(h, kv_norm), shift, scale)
    kv = hn @ kv_w_dkv
    ckv = rms_norm(kv[..., :KV_LORA_RANK], kv_latent_norm)
    kpe = apply_rope(kv[..., KV_LORA_RANK:], pos)
    return ckv, kpe


def mla_queries(xn, pos, w_dq, q_norm, w_uq):
    bsz, t, _ = xn.shape
    q = (rms_norm(xn @ w_dq, q_norm) @ w_uq).reshape(bsz, t, N_HEADS, QK_NOPE_DIM + QK_ROPE_DIM)
    return q[..., :QK_NOPE_DIM], apply_rope(q[..., QK_NOPE_DIM:], pos)


def mla_attend_prompt(q_nope, q_pe, ckv, kpe, w_uk, w_uv):
    bsz, s, _, _ = q_nope.shape
    k_nope = jnp.einsum("bsc,chd->bshd", ckv, w_uk)
    v = jnp.einsum("bsc,chd->bshd", ckv, w_uv)
    nb = s // Q_BLOCK
    qn_blocks = jnp.swapaxes(q_nope.reshape(bsz, nb, Q_BLOCK, N_HEADS, QK_NOPE_DIM), 0, 1)
    qp_blocks = jnp.swapaxes(q_pe.reshape(bsz, nb, Q_BLOCK, N_HEADS, QK_ROPE_DIM), 0, 1)
    key_pos = jnp.arange(s)

    def one_block(args):
        qn, qp, blk = args
        sc = (jnp.einsum("bqhd,bshd->bhqs", qn, k_nope)
              + jnp.einsum("bqhr,bsr->bhqs", qp, kpe)).astype(jnp.float32) * ATTN_SCALE
        q_pos = blk * Q_BLOCK + jnp.arange(Q_BLOCK)
        sc = jnp.where(key_pos[None, :] <= q_pos[:, None], sc, -jnp.inf)
        p = jax.nn.softmax(sc, axis=-1).astype(v.dtype)
        return jnp.einsum("bhqs,bshd->bqhd", p, v)

    o = lax.map(one_block, (qn_blocks, qp_blocks, jnp.arange(nb)))
    return jnp.swapaxes(o, 0, 1).reshape(bsz, s, N_HEADS * V_HEAD_DIM)


def mla_attend_sample(q_nope, q_pe, ckv_new, kpe_new, cache_ckv, cache_kpe, page_table, w_uk, w_uv):
    bsz, t, _, _ = q_nope.shape
    q_lat = jnp.einsum("bthd,chd->bhtc", q_nope, w_uk)
    qp = jnp.transpose(q_pe, (0, 2, 1, 3))

    def scores(ckv, kpe):
        return (jnp.einsum("bhtc,bpc->bhtp", q_lat, ckv)
                + jnp.einsum("bhtr,bpr->bhtp", qp, kpe)).astype(jnp.float32) * ATTN_SCALE

    def update(carry, sc, ckv):
        m, l, acc = carry
        m_new = jnp.maximum(m, jnp.max(sc, axis=-1))
        corr = jnp.exp(m - m_new)
        p = jnp.exp(sc - m_new[..., None])
        l = l * corr + jnp.sum(p, axis=-1)
        acc = acc * corr[..., None] + jnp.einsum("bhtp,bpc->bhtc", p, ckv.astype(jnp.float32))
        return (m_new, l, acc)

    def page_step(carry, phys):
        ckv, kpe = cache_ckv[phys], cache_kpe[phys]
        return update(carry, scores(ckv, kpe), ckv), None

    init = (jnp.full((bsz, N_HEADS, t), -jnp.inf, jnp.float32),
            jnp.zeros((bsz, N_HEADS, t), jnp.float32),
            jnp.zeros((bsz, N_HEADS, t, KV_LORA_RANK), jnp.float32))
    carry, _ = lax.scan(page_step, init, page_table.T)
    causal = jnp.tril(jnp.ones((t, t), dtype=bool))
    sc_new = jnp.where(causal, scores(ckv_new, kpe_new), -jnp.inf)
    _, l, acc = update(carry, sc_new, ckv_new)
    out_lat = (acc / l[..., None]).astype(q_nope.dtype)
    o = jnp.einsum("bhtc,chd->bthd", out_lat, w_uv)
    return o.reshape(bsz, t, N_HEADS * V_HEAD_DIM)


def expert_dispatch(xt, expert_ids, weights, w13, w2):
    n, d = xt.shape
    na = n * TOP_K
    flat_e = expert_ids.reshape(na)
    flat_tok = jnp.repeat(jnp.arange(n, dtype=jnp.int32), TOP_K)
    flat_w = weights.reshape(na)
    order = jnp.argsort(flat_e)
    se, stok, sw = flat_e[order], flat_tok[order], flat_w[order]
    counts = jnp.bincount(flat_e, length=N_EXPERTS)
    padded = (counts + MOE_BLOCK - 1) // MOE_BLOCK * MOE_BLOCK
    pad_end = jnp.cumsum(padded)
    pad_start = pad_end - padded
    start = jnp.cumsum(counts) - counts
    dest = pad_start[se] + (jnp.arange(na, dtype=jnp.int32) - start[se])
    n_blocks = (na + MOE_BLOCK - 1) // MOE_BLOCK + N_EXPERTS
    slot_tok = jnp.zeros((n_blocks * MOE_BLOCK,), jnp.int32).at[dest].set(stok)
    block_start = jnp.arange(n_blocks, dtype=jnp.int32) * MOE_BLOCK
    block_expert = jnp.minimum(jnp.searchsorted(pad_end, block_start, side="right"), N_EXPERTS - 1)
    xb = xt[slot_tok].reshape(n_blocks, MOE_BLOCK, d)

    def run_block(args):
        xblk, e = args
        gu = xblk @ w13[e]
        return (jax.nn.silu(gu[:, :D_EXPERT]) * gu[:, D_EXPERT:]) @ w2[e]

    yb = lax.map(run_block, (xb, block_expert)).reshape(n_blocks * MOE_BLOCK, d)
    return jax.ops.segment_sum(yb[dest] * sw[:, None].astype(yb.dtype), stok, num_segments=n)


def hier_moe(x, w_group, b_group, w_expert, b_expert, w13, w2):
    bsz, t, d = x.shape
    xt = x.reshape(bsz * t, d)
    n = xt.shape[0]
    g_prob = jax.nn.softmax((xt @ w_group + b_group).astype(jnp.float32), axis=-1)
    g_w, g_idx = lax.top_k(g_prob, 1)
    g_idx = g_idx[:, 0]
    e_logits = (xt @ w_expert + b_expert).astype(jnp.float32).reshape(n, N_GROUPS, EXPERTS_PER_GROUP)
    e_logits = e_logits[jnp.arange(n), g_idx]
    top_l, top_i = lax.top_k(e_logits, TOP_K)
    weights = jax.nn.softmax(top_l, axis=-1) * g_w
    expert_ids = g_idx[:, None] * EXPERTS_PER_GROUP + top_i
    return expert_dispatch(xt, expert_ids, weights, w13, w2).reshape(bsz, t, d)


def setup_inputs(seed: int = 0) -> dict:
    key = jax.random.key(seed)
    ks = iter(jax.random.split(key, 48))
    f32 = jnp.float32

    def nrm(shape, scale=1.0):
        return jax.random.normal(next(ks), shape, f32) * scale

    def gain(shape):
        return 1.0 + 0.01 * jax.random.normal(next(ks), shape, f32)

    n_pages = PAST_LEN // PAGE_SIZE
    n_used = DEC_BATCH * n_pages
    n_pool = n_used + max(n_used // 4, 1)
    page_table = jax.random.permutation(next(ks), n_pool)[:n_used].reshape(DEC_BATCH, n_pages).astype(jnp.int32)
    a0 = jax.random.uniform(next(ks), (N_A_LAYERS, D_RNN), f32, 0.9, 0.999)
    s0 = a0 ** (1.0 / LRU_C)
    rg_lambda = jnp.log(s0) - jnp.log1p(-s0)
    d = D_MODEL
    return {
        "x_prompt": nrm((BATCH, SEQ, d)),
        "x_sample": nrm((DEC_BATCH, DEC_SEQ, d)),
        "cache_ckv": nrm((n_pool, PAGE_SIZE, KV_LORA_RANK)),
        "cache_kpe": nrm((n_pool, PAGE_SIZE, QK_ROPE_DIM)),
        "state_conv": nrm((N_A_LAYERS, DEC_BATCH, CONV_WIDTH - 1, D_RNN)),
        "state_rglru": nrm((N_A_LAYERS, DEC_BATCH, D_RNN), 0.5),
        "page_table": page_table,
        "c_prompt": nrm((BATCH, d)),
        "c_sample": nrm((DEC_BATCH, d)),
        "mod_w": nrm((DEPTH, d, 6 * d), MOD_SCALE * d ** -0.5),
        "mod_b": nrm((DEPTH, 6 * d), 0.01),
        "mix_norm": gain((DEPTH, d)),
        "ffn_norm": gain((DEPTH, d)),
        "rg_w_in": nrm((N_A_LAYERS, d, 2 * D_RNN), d ** -0.5),
        "rg_conv_w": nrm((N_A_LAYERS, CONV_WIDTH, D_RNN), CONV_WIDTH ** -0.5),
        "rg_conv_b": nrm((N_A_LAYERS, D_RNN), 0.01),
        "rg_w_gates": nrm((N_A_LAYERS, N_RNN_BLOCKS, RNN_BLOCK, 2 * RNN_BLOCK), RNN_BLOCK ** -0.5),
        "rg_b_gates": nrm((N_A_LAYERS, N_RNN_BLOCKS, 2 * RNN_BLOCK), 0.01),
        "rg_lambda": rg_lambda,
        "rg_w_out": nrm((N_A_LAYERS, D_RNN, d), D_RNN ** -0.5),
        "kv_mod_w": nrm((d, 2 * d), MOD_SCALE * d ** -0.5),
        "kv_mod_b": nrm((2 * d,), 0.01),
        "kv_norm": gain((d,)),
        "kv_w_dkv": nrm((d, KV_LORA_RANK + QK_ROPE_DIM), d ** -0.5),
        "kv_latent_norm": gain((KV_LORA_RANK,)),
        "kv_w_uk": nrm((KV_LORA_RANK, N_HEADS, QK_NOPE_DIM), KV_LORA_RANK ** -0.5),
        "kv_w_uv": nrm((KV_LORA_RANK, N_HEADS, V_HEAD_DIM), KV_LORA_RANK ** -0.5),
        "mla_w_dq": nrm((N_B_LAYERS, d, Q_LORA_RANK), d ** -0.5),
        "mla_q_norm": gain((N_B_LAYERS, Q_LORA_RANK)),
        "mla_w_uq": nrm((N_B_LAYERS, Q_LORA_RANK, N_HEADS * (QK_NOPE_DIM + QK_ROPE_DIM)), Q_LORA_RANK ** -0.5),
        "mla_w_o": nrm((N_B_LAYERS, N_HEADS * V_HEAD_DIM, d), (N_HEADS * V_HEAD_DIM) ** -0.5),
        "moe_w_group": nrm((DEPTH, d, N_GROUPS), d ** -0.5),
        "moe_b_group": nrm((DEPTH, N_GROUPS), 0.01),
        "moe_w_expert": nrm((DEPTH, d, N_EXPERTS), d ** -0.5),
        "moe_b_expert": nrm((DEPTH, N_EXPERTS), 0.01),
        "moe_w13": nrm((DEPTH, N_EXPERTS, d, 2 * D_EXPERT), d ** -0.5),
        "moe_w2": nrm((DEPTH, N_EXPERTS, D_EXPERT, d), D_EXPERT ** -0.5),
        "final_norm": gain((d,)),
    }


def reference(x_prompt, x_sample, cache_ckv, cache_kpe, state_conv, state_rglru, page_table,
              c_prompt, c_sample, mod_w, mod_b, mix_norm, ffn_norm,
              rg_w_in, rg_conv_w, rg_conv_b, rg_w_gates, rg_b_gates, rg_lambda, rg_w_out,
              kv_mod_w, kv_mod_b, kv_norm, kv_w_dkv, kv_latent_norm, kv_w_uk, kv_w_uv,
              mla_w_dq, mla_q_norm, mla_w_uq, mla_w_o,
              moe_w_group, moe_b_group, moe_w_expert, moe_b_expert, moe_w13, moe_w2,
              final_norm):

    def trunk(x, c, pos, conv_init, h_init, attend):
        conv_out, h_out = [], []
        ckv, kpe = None, None
        for layer in range(DEPTH):
            sh_m, sc_m, g_m, sh_f, sc_f, g_f = jnp.split(c @ mod_w[layer] + mod_b[layer], 6, axis=-1)
            xn = modulate(rms_norm(x, mix_norm[layer]), sh_m, sc_m)
            if layer < N_A_LAYERS:
                a = layer
                y, conv_new, h_new = recurrent_mixer(
                    xn, conv_init[a], h_init[a], rg_w_in[a], rg_conv_w[a], rg_conv_b[a],
                    rg_w_gates[a], rg_b_gates[a], rg_lambda[a], rg_w_out[a])
                conv_out.append(conv_new)
                h_out.append(h_new)
            else:
                if layer == N_A_LAYERS:
                    ckv, kpe = shared_latent_kv(x, c, pos, kv_mod_w, kv_mod_b, kv_norm,
                                                kv_w_dkv, kv_latent_norm)
                b = layer - N_A_LAYERS
                q_nope, q_pe = mla_queries(xn, pos, mla_w_dq[b], mla_q_norm[b], mla_w_uq[b])
                y = attend(q_nope, q_pe, ckv, kpe) @ mla_w_o[b]
            x = x + g_m[:, None, :] * y
            xn = modulate(rms_norm(x, ffn_norm[layer]), sh_f, sc_f)
            x = x + g_f[:, None, :] * hier_moe(xn, moe_w_group[layer], moe_b_group[layer],
                                               moe_w_expert[layer], moe_b_expert[layer],
                                               moe_w13[layer], moe_w2[layer])
        return rms_norm(x, final_norm), jnp.stack(conv_out), jnp.stack(h_out), ckv, kpe

    bp = x_prompt.shape[0]
    conv0 = jnp.zeros((N_A_LAYERS, bp, CONV_WIDTH - 1, D_RNN), x_prompt.dtype)
    h0 = jnp.zeros((N_A_LAYERS, bp, D_RNN), x_prompt.dtype)
    pos_p = jnp.arange(x_prompt.shape[1], dtype=jnp.int32)
    y_prompt, conv_p, h_p, ckv_p, kpe_p = trunk(
        x_prompt, c_prompt, pos_p, conv0, h0,
        lambda qn, qp, ckv, kpe: mla_attend_prompt(qn, qp, ckv, kpe, kv_w_uk, kv_w_uv))

    pos_s = PAST_LEN + jnp.arange(x_sample.shape[1], dtype=jnp.int32)
    y_sample, conv_s, h_s, ckv_s, kpe_s = trunk(
        x_sample, c_sample, pos_s, state_conv, state_rglru,
        lambda qn, qp, ckv, kpe: mla_attend_sample(qn, qp, ckv, kpe, cache_ckv, cache_kpe,
                                                   page_table, kv_w_uk, kv_w_uv))

    return (y_prompt, y_sample, conv_p, h_p, ckv_p, kpe_p, conv_s, h_s, ckv_s, kpe_s)
```

```python
import functools
import math
from typing import NamedTuple

import jax
import jax.numpy as jnp
from jax import lax
from jax.experimental import pallas as pl
from jax.experimental.pallas import tpu as pltpu

EPS = 1e-6
LRU_C = 8.0
ROPE_THETA = 10000.0
TOP_K = 2
LANE = 128
SUBLANE = 8
VMEM_LIMIT = 56 * 1024 * 1024
MOE_ROWS = 128
SAMPLE_ROWS = 64
DECODE_PAGES = 32
LOG2E = 1.4426950408889634

f32 = jnp.float32
bf16 = jnp.bfloat16


def _params(*sem):
    return pltpu.CompilerParams(dimension_semantics=sem, vmem_limit_bytes=VMEM_LIMIT)


def _pick_tile(n, cap, mult=LANE):
    if n <= cap:
        return n
    best = None
    for t in range(mult, cap + 1, mult):
        if n % t == 0:
            best = t
    assert best is not None, (n, cap, mult)
    return best


class _Group(NamedTuple):
    n: int
    tm: int
    seq: int
    per_token: bool


def _mod_operand(vec, grp):
    d = vec.shape[-1]
    if grp.per_token:
        arr = jnp.tile(vec, (grp.n // vec.shape[0], 1))
        return arr, pl.BlockSpec((grp.tm, d), lambda i, *_: (i, 0))
    per_seq = grp.seq // grp.tm
    return vec[:, None, :], pl.BlockSpec((None, 1, d), lambda i, *_: (i // per_seq, 0, 0))


def _pos_operand(tab, grp):
    if grp.per_token:
        arr = jnp.repeat(tab, grp.n // tab.shape[0], axis=0)
        return arr, pl.BlockSpec((grp.tm, LANE), lambda i, *_: (i, 0))
    per_seq = grp.seq // grp.tm
    return tab, pl.BlockSpec((grp.tm, LANE), lambda i, *_: (i % per_seq, 0))


def _row_spec(grp, width):
    return pl.BlockSpec((grp.tm, width), lambda i, *_: (i, 0))


def _full_spec(shape):
    nd = len(shape)
    return pl.BlockSpec(shape, lambda *_: (0,) * nd)


def _rms(x, gain):
    return x * lax.rsqrt(jnp.mean(x * x, axis=-1, keepdims=True) + EPS) * gain


def _normmod(x, gain, shift, scale):
    return _rms(x, gain) * (1.0 + scale) + shift


def _rope_tile(t, cos, sin, half):
    lane = lax.broadcasted_iota(jnp.int32, t.shape, 1)
    rot = jnp.where(lane < half, pltpu.roll(t, LANE - half, 1), pltpu.roll(t, half, 1))
    return t * cos + rot * sin


def _sigmoid(x):
    return 0.5 * (jnp.tanh(0.5 * x) + 1.0)


def _gelu(x):
    return 0.5 * x * (1.0 + jnp.tanh(math.sqrt(2.0 / math.pi) * (x + 0.044715 * (x * x * x))))


def _softplus(z):
    return jnp.maximum(z, 0.0) + jnp.log1p(jnp.exp(-jnp.abs(z)))


def _bias_matmul_kernel(a_ref, w_ref, b_ref, o_ref):
    o_ref[...] = jnp.dot(a_ref[...].astype(bf16), w_ref[...].astype(bf16),
                         preferred_element_type=f32) + b_ref[...]


def _bias_matmul(a, w3, layer, b):
    m, k = a.shape
    n = w3.shape[-1]
    tn = _pick_tile(n, 1024)
    return pl.pallas_call(
        _bias_matmul_kernel,
        grid=(n // tn,),
        in_specs=[_full_spec((m, k)),
                  pl.BlockSpec((None, k, tn), lambda j: (layer, 0, j)),
                  pl.BlockSpec((1, tn), lambda j: (0, j))],
        out_specs=pl.BlockSpec((m, tn), lambda j: (0, j)),
        out_shape=jax.ShapeDtypeStruct((m, n), f32),
        compiler_params=_params("arbitrary"),
    )(a, w3, b.reshape(1, n))


def _normmod_matmul_kernel(x_ref, gain_ref, shift_ref, scale_ref, w_ref, o_ref, xn_ref):
    @pl.when(pl.program_id(1) == 0)
    def _():
        xn_ref[...] = _normmod(x_ref[...], gain_ref[...], shift_ref[...], scale_ref[...]).astype(bf16)

    o_ref[...] = jnp.dot(xn_ref[...], w_ref[...], preferred_element_type=f32).astype(o_ref.dtype)


def _normmod_matmul(x, gain, shift, scale, w, grp, tn_cap=1024, out_dtype=f32):
    d = x.shape[1]
    n = w.shape[1]
    tn = _pick_tile(n, tn_cap)
    sh, sh_spec = _mod_operand(shift, grp)
    sc, sc_spec = _mod_operand(scale, grp)
    return pl.pallas_call(
        _normmod_matmul_kernel,
        grid=(grp.n // grp.tm, n // tn),
        in_specs=[_row_spec(grp, d), _full_spec((1, d)), sh_spec, sc_spec,
                  pl.BlockSpec((d, tn), lambda i, j: (0, j))],
        out_specs=pl.BlockSpec((grp.tm, tn), lambda i, j: (i, j)),
        out_shape=jax.ShapeDtypeStruct((grp.n, n), out_dtype),
        scratch_shapes=[pltpu.VMEM((grp.tm, d), bf16)],
        compiler_params=_params("arbitrary", "arbitrary"),
    )(x, gain.reshape(1, d), sh, sc, w)


def _matmul_res_kernel(a_ref, w_ref, res_ref, gate_ref, o_ref):
    y = jnp.dot(a_ref[...].astype(bf16), w_ref[...], preferred_element_type=f32)
    o_ref[...] = res_ref[...] + gate_ref[...] * y


def _matmul_res(a, w, res, gate, grp, tn_cap=1024):
    k = a.shape[1]
    n = w.shape[1]
    tn = _pick_tile(n, tn_cap)
    if grp.per_token:
        g_arr = jnp.tile(gate, (grp.n // gate.shape[0], 1))
        g_spec = pl.BlockSpec((grp.tm, tn), lambda j, i: (i, j))
    else:
        per_seq = grp.seq // grp.tm
        g_arr = gate[:, None, :]
        g_spec = pl.BlockSpec((None, 1, tn), lambda j, i: (i // per_seq, 0, j))
    return pl.pallas_call(
        _matmul_res_kernel,
        grid=(n // tn, grp.n // grp.tm),
        in_specs=[pl.BlockSpec((grp.tm, k), lambda j, i: (i, 0)),
                  pl.BlockSpec((k, tn), lambda j, i: (0, j)),
                  pl.BlockSpec((grp.tm, tn), lambda j, i: (i, j)),
                  g_spec],
        out_specs=pl.BlockSpec((grp.tm, tn), lambda j, i: (i, j)),
        out_shape=jax.ShapeDtypeStruct((grp.n, n), f32),
        compiler_params=_params("arbitrary", "arbitrary"),
    )(a, w, res, g_arr)


def _gate_windows(d_rnn, blk):
    nt = d_rnn // LANE
    raw, need = [], 0
    for j in range(nt):
        n0 = (LANE * j) // blk
        n1 = (LANE * j + LANE - 1) // blk
        s = (blk * n0) // LANE * LANE
        raw.append(s)
        need = max(need, blk * (n1 + 1) - s)
    win = min(d_rnn, -(-need // LANE) * LANE)
    return [min(s, d_rnn - win) for s in raw], win


def _gate_weights(w_gates, starts, win):
    nb, blk, _ = w_gates.shape
    nt = nb * blk // LANE
    s = jnp.asarray(starts, jnp.int32)[:, None, None]
    c = s + jnp.arange(win, dtype=jnp.int32)[None, :, None]
    o = (jnp.arange(nt, dtype=jnp.int32) * LANE)[:, None, None] + jnp.arange(LANE, dtype=jnp.int32)[None, None, :]
    n = o // blk
    oo = o % blk
    valid = (c // blk) == n
    ci = c % blk
    wr = jnp.where(valid, w_gates[n, ci, oo], 0.0)
    wi = jnp.where(valid, w_gates[n, ci, blk + oo], 0.0)
    return jnp.concatenate([wr, wi], axis=-1).astype(bf16)


def _lru_inputs(g, b_r, b_i, neg_c_sp, u):
    r = _sigmoid(g[:, :LANE] + b_r)
    i = _sigmoid(g[:, LANE:] + b_i)
    log_a = neg_c_sp * r
    a = jnp.exp(log_a)
    x = jnp.sqrt(-jnp.tanh(log_a) * (a * a + 1.0)) * (i * u)
    return a, x


def _rglru_prompt_kernel(proj_ref, h0_ref, cinit_ref, cw_ref, cb_ref, wg_ref, bgr_ref, bgi_ref, lam_ref,
                         hg_ref, hlast_ref, ubuf, ucf, ucb, a_scr, x_scr, hcar,
                         *, starts, win, tc, d_rnn, cwid):
    t = pl.program_id(1)
    hist = SUBLANE

    @pl.when(t == 0)
    def _():
        ubuf[0:hist, :] = cinit_ref[...]
        hcar[...] = h0_ref[...]

    ubuf[hist:hist + tc, :] = proj_ref[:, d_rnn:]
    uc = cb_ref[...]
    for k in range(cwid):
        off = hist - (cwid - 1) + k
        uc = uc + cw_ref[k:k + 1, :] * ubuf[off:off + tc, :]
    ucf[...] = uc
    ucb[...] = uc.astype(bf16)
    ubuf[0:hist, :] = ubuf[tc:tc + hist, :]

    neg_c_sp = -LRU_C * _softplus(-lam_ref[...])
    seg = tc // SUBLANE
    for j in range(d_rnn // LANE):
        cs = slice(j * LANE, (j + 1) * LANE)
        g = jnp.dot(ucb[:, starts[j]:starts[j] + win], wg_ref[j], preferred_element_type=f32)
        a, x = _lru_inputs(g, bgr_ref[:, cs], bgi_ref[:, cs], neg_c_sp[:, cs], ucf[:, cs])
        a_scr[j] = a
        x_scr[j] = x
        h = jnp.zeros((SUBLANE, LANE), f32)
        p = jnp.ones((SUBLANE, LANE), f32)
        for k in range(seg):
            ak = a_scr[j, pl.ds(k, SUBLANE, stride=seg), :]
            xk = x_scr[j, pl.ds(k, SUBLANE, stride=seg), :]
            h = ak * h + xk
            p = p * ak
            x_scr[j, pl.ds(k, SUBLANE, stride=seg), :] = h
            a_scr[j, pl.ds(k, SUBLANE, stride=seg), :] = p
        c = hcar[:, cs]
        outs = []
        for r in range(SUBLANE):
            rows = slice(r * seg, (r + 1) * seg)
            outs.append(x_scr[j, rows, :] + a_scr[j, rows, :] * c)
            c = p[r:r + 1, :] * c + h[r:r + 1, :]
        hcar[:, cs] = c
        hs = jnp.concatenate(outs, axis=0)
        hg_ref[:, cs] = (hs * _gelu(proj_ref[:, cs])).astype(hg_ref.dtype)
    hlast_ref[...] = hcar[...]


def _rglru_prompt(proj, h0, conv_init, conv_w, conv_b, wg, bgr, bgi, lam, starts, win, nb_seq, seq, tc):
    d_rnn = proj.shape[1] // 2
    cwid = conv_w.shape[0]
    nt = d_rnn // LANE
    per_seq = seq // tc
    kern = functools.partial(_rglru_prompt_kernel, starts=tuple(starts), win=win, tc=tc, d_rnn=d_rnn, cwid=cwid)
    return pl.pallas_call(
        kern,
        grid=(nb_seq, per_seq),
        in_specs=[pl.BlockSpec((tc, 2 * d_rnn), lambda b, t: (b * per_seq + t, 0)),
                  pl.BlockSpec((None, 1, d_rnn), lambda b, t: (b, 0, 0)),
                  pl.BlockSpec((None, SUBLANE, d_rnn), lambda b, t: (b, 0, 0)),
                  _full_spec((cwid, d_rnn)), _full_spec((1, d_rnn)),
                  _full_spec(wg.shape), _full_spec((1, d_rnn)), _full_spec((1, d_rnn)), _full_spec((1, d_rnn))],
        out_specs=[pl.BlockSpec((tc, d_rnn), lambda b, t: (b * per_seq + t, 0)),
                   pl.BlockSpec((None, 1, d_rnn), lambda b, t: (b, 0, 0))],
        out_shape=[jax.ShapeDtypeStruct((nb_seq * seq, d_rnn), bf16),
                   jax.ShapeDtypeStruct((nb_seq, 1, d_rnn), f32)],
        scratch_shapes=[pltpu.VMEM((tc + SUBLANE, d_rnn), f32),
                        pltpu.VMEM((tc, d_rnn), f32),
                        pltpu.VMEM((tc, d_rnn), bf16),
                        pltpu.VMEM((nt, tc, LANE), f32),
                        pltpu.VMEM((nt, tc, LANE), f32),
                        pltpu.VMEM((1, d_rnn), f32)],
        compiler_params=_params("arbitrary", "arbitrary"),
    )(proj, h0, conv_init, conv_w, conv_b, wg, bgr, bgi, lam)


def _rglru_sample_kernel(proj_ref, h0_ref, cst_ref, cw_ref, cb_ref, wg_ref, bgr_ref, bgi_ref, lam_ref,
                         hg_ref, hlast_ref, ucf, ucb, *, starts, win, steps, tb, d_rnn, cwid):
    def up(tp):
        if tp < cwid - 1:
            return cst_ref[tp]
        return proj_ref[tp - cwid + 1, :, d_rnn:]

    for t in range(steps):
        uc = cb_ref[...]
        for k in range(cwid):
            uc = uc + cw_ref[k:k + 1, :] * up(t + k)
        ucf[t * tb:(t + 1) * tb, :] = uc
        ucb[t * tb:(t + 1) * tb, :] = uc.astype(bf16)

    neg_c_sp = -LRU_C * _softplus(-lam_ref[...])
    for j in range(d_rnn // LANE):
        cs = slice(j * LANE, (j + 1) * LANE)
        g = jnp.dot(ucb[:, starts[j]:starts[j] + win], wg_ref[j], preferred_element_type=f32)
        h = h0_ref[:, cs]
        for t in range(steps):
            rows = slice(t * tb, (t + 1) * tb)
            a, x = _lru_inputs(g[rows], bgr_ref[:, cs], bgi_ref[:, cs], neg_c_sp[:, cs], ucf[rows, cs])
            h = a * h + x
            hg_ref[t, :, cs] = (h * _gelu(proj_ref[t, :, cs])).astype(hg_ref.dtype)
        hlast_ref[:, cs] = h


def _rglru_sample(proj, h0, conv_state, conv_w, conv_b, wg, bgr, bgi, lam, starts, win):
    steps, nb, _ = proj.shape
    d_rnn = proj.shape[2] // 2
    cwid = conv_w.shape[0]
    tb = min(SAMPLE_ROWS, nb)
    kern = functools.partial(_rglru_sample_kernel, starts=tuple(starts), win=win, steps=steps, tb=tb,
                             d_rnn=d_rnn, cwid=cwid)
    return pl.pallas_call(
        kern,
        grid=(nb // tb,),
        in_specs=[pl.BlockSpec((steps, tb, 2 * d_rnn), lambda i: (0, i, 0)),
                  pl.BlockSpec((tb, d_rnn), lambda i: (i, 0)),
                  pl.BlockSpec((cwid - 1, tb, d_rnn), lambda i: (0, i, 0)),
                  _full_spec((cwid, d_rnn)), _full_spec((1, d_rnn)),
                  _full_spec(wg.shape), _full_spec((1, d_rnn)), _full_spec((1, d_rnn)), _full_spec((1, d_rnn))],
        out_specs=[pl.BlockSpec((steps, tb, d_rnn), lambda i: (0, i, 0)),
                   pl.BlockSpec((tb, d_rnn), lambda i: (i, 0))],
        out_shape=[jax.ShapeDtypeStruct((steps, nb, d_rnn), bf16),
                   jax.ShapeDtypeStruct((nb, d_rnn), f32)],
        scratch_shapes=[pltpu.VMEM((steps * tb, d_rnn), f32), pltpu.VMEM((steps * tb, d_rnn), bf16)],
        compiler_params=_params("arbitrary"),
    )(proj, h0, conv_state, conv_w, conv_b, wg, bgr, bgi, lam)


def _router_kernel(x_ref, gain_ref, shift_ref, scale_ref, wh_ref, wl_ref, xn_ref, lg_ref):
    xn = _normmod(x_ref[...], gain_ref[...], shift_ref[...], scale_ref[...])
    xh = xn.astype(bf16)
    xl = (xn - xh.astype(f32)).astype(bf16)
    xn_ref[...] = xh
    lg_ref[...] = (jnp.dot(xh, wh_ref[...], preferred_element_type=f32)
                   + jnp.dot(xl, wh_ref[...], preferred_element_type=f32)
                   + jnp.dot(xh, wl_ref[...], preferred_element_type=f32)
                   + jnp.dot(xl, wl_ref[...], preferred_element_type=f32))


def _router(x, gain, shift, scale, wr, grp):
    d = x.shape[1]
    nr = wr.shape[1]
    wh = wr.astype(bf16)
    wl = (wr - wh.astype(f32)).astype(bf16)
    sh, sh_spec = _mod_operand(shift, grp)
    sc, sc_spec = _mod_operand(scale, grp)
    return pl.pallas_call(
        _router_kernel,
        grid=(grp.n // grp.tm,),
        in_specs=[_row_spec(grp, d), _full_spec((1, d)), sh_spec, sc_spec,
                  _full_spec((d, nr)), _full_spec((d, nr))],
        out_specs=[_row_spec(grp, d), _row_spec(grp, nr)],
        out_shape=[jax.ShapeDtypeStruct((grp.n, d), bf16), jax.ShapeDtypeStruct((grp.n, nr), f32)],
        compiler_params=_params("arbitrary"),
    )(x, gain.reshape(1, d), sh, sc, wh, wl)


def _experts_kernel(be_ref, nu_ref, xb_ref, w13_ref, w2_ref, o_ref, w13b, w2b, *, d_exp):
    i = pl.program_id(0)
    e = be_ref[i]
    prev = be_ref[jnp.maximum(i - 1, 0)]

    @pl.when((i == 0) | (e != prev))
    def _():
        w13b[...] = w13_ref[...].astype(bf16)
        w2b[...] = w2_ref[...].astype(bf16)

    @pl.when(i < nu_ref[0])
    def _():
        gu = jnp.dot(xb_ref[...], w13b[...], preferred_element_type=f32)
        g = gu[:, :d_exp]
        act = (g * _sigmoid(g)) * gu[:, d_exp:]
        o_ref[...] = jnp.dot(act.astype(bf16), w2b[...], preferred_element_type=f32)

    @pl.when(i >= nu_ref[0])
    def _():
        o_ref[...] = jnp.zeros_like(o_ref)


def _experts(xb, block_expert, n_used, w13, w2, layer):
    rows, d = xb.shape
    n_blocks = rows // MOE_ROWS
    d_exp = w2.shape[2]
    gs = pltpu.PrefetchScalarGridSpec(
        num_scalar_prefetch=2,
        grid=(n_blocks,),
        in_specs=[pl.BlockSpec((MOE_ROWS, d), lambda i, be, nu: (i, 0)),
                  pl.BlockSpec((None, None, d, 2 * d_exp), lambda i, be, nu: (layer, be[i], 0, 0)),
                  pl.BlockSpec((None, None, d_exp, d), lambda i, be, nu: (layer, be[i], 0, 0))],
        out_specs=pl.BlockSpec((MOE_ROWS, d), lambda i, be, nu: (i, 0)),
        scratch_shapes=[pltpu.VMEM((d, 2 * d_exp), bf16), pltpu.VMEM((d_exp, d), bf16)],
    )
    return pl.pallas_call(
        functools.partial(_experts_kernel, d_exp=d_exp),
        grid_spec=gs,
        out_shape=jax.ShapeDtypeStruct((rows, d), f32),
        compiler_params=_params("arbitrary"),
    )(block_expert, n_used, xb, w13, w2)


def _combine_kernel(x_ref, ya_ref, yb_ref, wt_ref, gate_ref, *rest, final):
    if final:
        fg_ref, o_ref = rest
    else:
        (o_ref,) = rest
    wt = wt_ref[...]
    y = x_ref[...] + gate_ref[...] * (wt[:, 0:1] * ya_ref[...] + wt[:, 1:2] * yb_ref[...])
    if final:
        y = _rms(y, fg_ref[...])
    o_ref[...] = y


def _combine(x, ya, yb, wts, gate, grp, final_gain=None):
    d = x.shape[1]
    g_arr, g_spec = _mod_operand(gate, grp)
    final = final_gain is not None
    args = [x, ya, yb, wts, g_arr]
    specs = [_row_spec(grp, d), _row_spec(grp, d), _row_spec(grp, d), _row_spec(grp, TOP_K), g_spec]
    if final:
        args.append(final_gain.reshape(1, d))
        specs.append(_full_spec((1, d)))
    return pl.pallas_call(
        functools.partial(_combine_kernel, final=final),
        grid=(grp.n // grp.tm,),
        in_specs=specs,
        out_specs=_row_spec(grp, d),
        out_shape=jax.ShapeDtypeStruct((grp.n, d), f32),
        compiler_params=_params("arbitrary"),
    )(*args)


def _route(logits, b_group, b_expert, n_groups):
    n = logits.shape[0]
    n_exp = b_expert.shape[0]
    per = n_exp // n_groups
    g_prob = jax.nn.softmax(logits[:, :n_groups] + b_group, axis=-1)
    g_w, g_idx = lax.top_k(g_prob, 1)
    e_log = (logits[:, n_groups:n_groups + n_exp] + b_expert).reshape(n, n_groups, per)
    e_sel = jnp.take_along_axis(e_log, g_idx[:, :, None], axis=1)[:, 0]
    top_l, top_i = lax.top_k(e_sel, TOP_K)
    weights = jax.nn.softmax(top_l, axis=-1) * g_w
    return g_idx * per + top_i, weights


def _dispatch(expert_ids, n_exp):
    n = expert_ids.shape[0]
    na = n * TOP_K
    flat_e = expert_ids.reshape(na).astype(jnp.int32)
    order = jnp.argsort(flat_e).astype(jnp.int32)
    se = flat_e[order]
    stok = order // TOP_K
    counts = jnp.bincount(flat_e, length=n_exp).astype(jnp.int32)
    padded = (counts + MOE_ROWS - 1) // MOE_ROWS * MOE_ROWS
    pad_end = jnp.cumsum(padded)
    pad_start = pad_end - padded
    start = jnp.cumsum(counts) - counts
    dest_sorted = pad_start[se] + (jnp.arange(na, dtype=jnp.int32) - start[se])
    n_blocks = (na + MOE_ROWS - 1) // MOE_ROWS + n_exp
    slot_tok = jnp.zeros((n_blocks * MOE_ROWS,), jnp.int32).at[dest_sorted].set(stok)
    block_start = jnp.arange(n_blocks, dtype=jnp.int32) * MOE_ROWS
    block_expert = jnp.minimum(jnp.searchsorted(pad_end, block_start, side="right"), n_exp - 1).astype(jnp.int32)
    n_used = (pad_end[-1:] // MOE_ROWS).astype(jnp.int32)
    dest = jnp.zeros((na,), jnp.int32).at[order].set(dest_sorted).reshape(n, TOP_K)
    return slot_tok, block_expert, n_used, dest


def _kv_kernel(x_ref, gain_ref, shift_ref, scale_ref, w_ref, lg_ref, cos_ref, sin_ref, *rest,
               r_kv, half, with_up):
    if with_up:
        wup_ref, ckv_ref, kpe_ref, kn_ref, v_ref, kpeb_ref = rest
    else:
        ckv_ref, kpe_ref = rest
    a = _normmod(x_ref[...], gain_ref[...], shift_ref[...], scale_ref[...]).astype(bf16)
    kv = jnp.dot(a, w_ref[...], preferred_element_type=f32)
    c = _rms(kv[:, :r_kv], lg_ref[...])
    ckv_ref[...] = c
    kp = _rope_tile(kv[:, r_kv:], cos_ref[...], sin_ref[...], half)
    kpe_ref[...] = kp
    if with_up:
        up = jnp.dot(c.astype(bf16), wup_ref[...], preferred_element_type=f32)
        hw = up.shape[1] // 2
        kn_ref[...] = up[:, :hw].astype(bf16)
        v_ref[...] = up[:, hw:].astype(bf16)
        kpeb_ref[...] = kp.astype(bf16)


def _kv_latent(x, gain, shift, scale, w_pad, latent_gain, cos, sin, grp, half, w_up=None):
    d = x.shape[1]
    r_kv = latent_gain.shape[0]
    with_up = w_up is not None
    sh, sh_spec = _mod_operand(shift, grp)
    sc, sc_spec = _mod_operand(scale, grp)
    cs, cs_spec = _pos_operand(cos, grp)
    sn, sn_spec = _pos_operand(sin, grp)
    args = [x, gain.reshape(1, d), sh, sc, w_pad, latent_gain.reshape(1, r_kv), cs, sn]
    specs = [_row_spec(grp, d), _full_spec((1, d)), sh_spec, sc_spec, _full_spec(w_pad.shape),
             _full_spec((1, r_kv)), cs_spec, sn_spec]
    out_specs = [_row_spec(grp, r_kv), _row_spec(grp, LANE)]
    out_shape = [jax.ShapeDtypeStruct((grp.n, r_kv), f32), jax.ShapeDtypeStruct((grp.n, LANE), f32)]
    if with_up:
        hw = w_up.shape[1] // 2
        args.append(w_up)
        specs.append(_full_spec(w_up.shape))
        out_specs += [_row_spec(grp, hw), _row_spec(grp, hw), _row_spec(grp, LANE)]
        out_shape += [jax.ShapeDtypeStruct((grp.n, hw), bf16), jax.ShapeDtypeStruct((grp.n, hw), bf16),
                      jax.ShapeDtypeStruct((grp.n, LANE), bf16)]
    return pl.pallas_call(
        functools.partial(_kv_kernel, r_kv=r_kv, half=half, with_up=with_up),
        grid=(grp.n // grp.tm,),
        in_specs=specs, out_specs=out_specs, out_shape=out_shape,
        compiler_params=_params("arbitrary"),
    )(*args)


def _q_kernel(x_ref, gain_ref, shift_ref, scale_ref, wdq_ref, qg_ref, wq_ref, cos_ref, sin_ref, q_ref,
              *, n_heads, half):
    a = _normmod(x_ref[...], gain_ref[...], shift_ref[...], scale_ref[...]).astype(bf16)
    ql = jnp.dot(a, wdq_ref[...], preferred_element_type=f32)
    qn = _rms(ql, qg_ref[...]).astype(bf16)
    cos = cos_ref[...]
    sin = sin_ref[...]
    for h in range(n_heads):
        q = jnp.dot(qn, wq_ref[:, 2 * LANE * h:2 * LANE * (h + 1)], preferred_element_type=f32)
        q_ref[:, 2 * LANE * h:2 * LANE * h + LANE] = q[:, :LANE].astype(bf16)
        q_ref[:, 2 * LANE * h + LANE:2 * LANE * (h + 1)] = _rope_tile(q[:, LANE:], cos, sin, half).astype(bf16)


def _q_proj(x, gain, shift, scale, w_dq, q_gain, wq, cos, sin, grp, n_heads, half):
    d = x.shape[1]
    rq = w_dq.shape[1]
    sh, sh_spec = _mod_operand(shift, grp)
    sc, sc_spec = _mod_operand(scale, grp)
    cs, cs_spec = _pos_operand(cos, grp)
    sn, sn_spec = _pos_operand(sin, grp)
    return pl.pallas_call(
        functools.partial(_q_kernel, n_heads=n_heads, half=half),
        grid=(grp.n // grp.tm,),
        in_specs=[_row_spec(grp, d), _full_spec((1, d)), sh_spec, sc_spec, _full_spec(w_dq.shape),
                  _full_spec((1, rq)), _full_spec(wq.shape), cs_spec, sn_spec],
        out_specs=_row_spec(grp, wq.shape[1]),
        out_shape=jax.ShapeDtypeStruct((grp.n, wq.shape[1]), bf16),
        compiler_params=_params("arbitrary"),
    )(x, gain.reshape(1, d), sh, sc, w_dq, q_gain.reshape(1, rq), wq, cs, sn)


def _attn_kernel(q_ref, kn_ref, v_ref, kpe_ref, o_ref, *, hg, tq, c):
    qi = pl.program_id(2)
    row = lax.broadcasted_iota(jnp.int32, (tq, tq), 0)
    col = lax.broadcasted_iota(jnp.int32, (tq, tq), 1)
    tri = col <= row
    for h in range(hg):
        q = q_ref[:, 2 * LANE * h:2 * LANE * (h + 1)]
        hs = slice(h * LANE, (h + 1) * LANE)

        def scores(ks, q=q, hs=hs):
            k = jnp.concatenate([kn_ref[pl.ds(ks, tq), hs], kpe_ref[pl.ds(ks, tq), :]], axis=-1)
            return lax.dot_general(q, k, (((1,), (1,)), ((), ())), preferred_element_type=f32)

        def update(carry, s, ks, hs=hs):
            m, l, acc = carry
            m_new = jnp.maximum(m, jnp.max(s, axis=-1, keepdims=True))
            p = jnp.exp2((s - m_new) * c)
            corr = jnp.exp2((m - m_new) * c)
            l = l * corr + jnp.sum(p, axis=-1, keepdims=True)
            acc = acc * corr + jnp.dot(p.astype(bf16), v_ref[pl.ds(ks, tq), hs], preferred_element_type=f32)
            return m_new, l, acc

        def body(kb, carry, scores=scores, update=update):
            ks = pl.multiple_of(kb * tq, tq)
            return update(carry, scores(ks), ks)

        init = (jnp.full((tq, 1), -jnp.inf, f32), jnp.zeros((tq, 1), f32), jnp.zeros((tq, LANE), f32))
        carry = lax.fori_loop(0, qi, body, init)
        ks = pl.multiple_of(qi * tq, tq)
        s = jnp.where(tri, scores(ks), -jnp.inf)
        _, l, acc = update(carry, s, ks)
        o_ref[:, hs] = (acc * (1.0 / l)).astype(o_ref.dtype)


def _attn_prompt(q, kn, v, kpe, nb_seq, seq, n_heads, scale, tq, hg):
    nq = seq // tq
    return pl.pallas_call(
        functools.partial(_attn_kernel, hg=hg, tq=tq, c=scale * LOG2E),
        grid=(nb_seq, n_heads // hg, nq),
        in_specs=[pl.BlockSpec((tq, hg * 2 * LANE), lambda b, g, i: (b * nq + i, g)),
                  pl.BlockSpec((seq, hg * LANE), lambda b, g, i: (b, g)),
                  pl.BlockSpec((seq, hg * LANE), lambda b, g, i: (b, g)),
                  pl.BlockSpec((seq, LANE), lambda b, g, i: (b, 0))],
        out_specs=pl.BlockSpec((tq, hg * LANE), lambda b, g, i: (b * nq + i, g)),
        out_shape=jax.ShapeDtypeStruct((nb_seq * seq, n_heads * LANE), bf16),
        compiler_params=_params("arbitrary", "arbitrary", "arbitrary"),
    )(q, kn, v, kpe)


def _head_matmul_kernel(a_ref, w_ref, o_ref):
    o_ref[...] = jnp.dot(a_ref[...].astype(bf16), w_ref[...].astype(bf16),
                         preferred_element_type=f32).astype(o_ref.dtype)


def _head_matmul(a, a_spec, w, w_spec, out_shape, out_spec, n_heads):
    return pl.pallas_call(
        _head_matmul_kernel,
        grid=(n_heads,),
        in_specs=[a_spec, w_spec],
        out_specs=out_spec,
        out_shape=out_shape,
        compiler_params=_params("arbitrary"),
    )(a, w)


def _decode_kernel(pt_ref, qlat_ref, qpe_ref, cnew_ref, knew_ref, cache_c, cache_k, o_ref,
                   cbuf, kbuf, sem, m_scr, l_scr, acc_scr, *, ppc, n_chunks, n_pages, steps, c):
    b = pl.program_id(0)
    ch = pl.program_id(1)
    step = b * n_chunks + ch
    slot = step % 2
    total = pl.num_programs(0) * n_chunks

    def copies(bb, cc, sl, p):
        phys = pt_ref[bb * n_pages + cc * ppc + p]
        return (pltpu.make_async_copy(cache_c.at[phys], cbuf.at[sl, p], sem.at[0, sl]),
                pltpu.make_async_copy(cache_k.at[phys], kbuf.at[sl, p], sem.at[1, sl]))

    def issue(bb, cc, sl):
        for p in range(ppc):
            for cp in copies(bb, cc, sl, p):
                cp.start()

    @pl.when(step == 0)
    def _():
        issue(0, 0, 0)

    @pl.when(step + 1 < total)
    def _():
        nxt = step + 1
        issue(nxt // n_chunks, nxt % n_chunks, 1 - slot)

    for p in range(ppc):
        for cp in copies(b, ch, slot, p):
            cp.wait()

    @pl.when(ch == 0)
    def _():
        m_scr[...] = jnp.full_like(m_scr, -jnp.inf)
        l_scr[...] = jnp.zeros_like(l_scr)
        acc_scr[...] = jnp.zeros_like(acc_scr)

    qlat = qlat_ref[...]
    qpe = qpe_ref[...]
    rope = kbuf.shape[-1]
    nt = (((1,), (1,)), ((), ()))

    def update(s, vals):
        m = m_scr[...]
        m_new = jnp.maximum(m, jnp.max(s, axis=-1, keepdims=True))
        p = jnp.exp2((s - m_new) * c)
        corr = jnp.exp2((m - m_new) * c)
        l_scr[...] = l_scr[...] * corr + jnp.sum(p, axis=-1, keepdims=True)
        acc_scr[...] = acc_scr[...] * corr + jnp.dot(p.astype(bf16), vals, preferred_element_type=f32)
        m_scr[...] = m_new

    page = cbuf.shape[2]
    ck = cbuf[slot].reshape(ppc * page, cbuf.shape[3]).astype(bf16)
    kk = kbuf[slot].reshape(ppc * page, rope).astype(bf16)
    s = (lax.dot_general(qlat, ck, nt, preferred_element_type=f32)
         + lax.dot_general(qpe[:, :rope], kk, nt, preferred_element_type=f32))
    update(s, ck)

    @pl.when(ch == n_chunks - 1)
    def _():
        cn = cnew_ref[...]
        sn = (lax.dot_general(qlat, cn, nt, preferred_element_type=f32)
              + lax.dot_general(qpe, knew_ref[...], nt, preferred_element_type=f32))
        t_row = lax.broadcasted_iota(jnp.int32, sn.shape, 0) % steps
        key = lax.broadcasted_iota(jnp.int32, sn.shape, 1)
        update(jnp.where(key <= t_row, sn, -jnp.inf), cn)
        o_ref[...] = (acc_scr[...] * (1.0 / l_scr[...])).astype(o_ref.dtype)


def _decode_attn(page_table, qlat, qpe, cnew, knew, cache_c, cache_k, scale, steps, ppc):
    nb, rows, r_kv = qlat.shape
    n_pages = page_table.shape[1]
    page = cache_c.shape[1]
    rope = cache_k.shape[2]
    n_chunks = n_pages // ppc
    gs = pltpu.PrefetchScalarGridSpec(
        num_scalar_prefetch=1,
        grid=(nb, n_chunks),
        in_specs=[pl.BlockSpec((None, rows, r_kv), lambda b, ch, pt: (b, 0, 0)),
                  pl.BlockSpec((None, rows, LANE), lambda b, ch, pt: (b, 0, 0)),
                  pl.BlockSpec((None, LANE, r_kv), lambda b, ch, pt: (b, 0, 0)),
                  pl.BlockSpec((None, LANE, LANE), lambda b, ch, pt: (b, 0, 0)),
                  pl.BlockSpec(memory_space=pl.ANY),
                  pl.BlockSpec(memory_space=pl.ANY)],
        out_specs=pl.BlockSpec((None, rows, r_kv), lambda b, ch, pt: (b, 0, 0)),
        scratch_shapes=[pltpu.VMEM((2, ppc, page, r_kv), f32),
                        pltpu.VMEM((2, ppc, page, rope), f32),
                        pltpu.SemaphoreType.DMA((2, 2)),
                        pltpu.VMEM((rows, 1), f32), pltpu.VMEM((rows, 1), f32),
                        pltpu.VMEM((rows, r_kv), f32)],
    )
    return pl.pallas_call(
        functools.partial(_decode_kernel, ppc=ppc, n_chunks=n_chunks, n_pages=n_pages, steps=steps,
                          c=scale * LOG2E),
        grid_spec=gs,
        out_shape=jax.ShapeDtypeStruct((nb, rows, r_kv), bf16),
        compiler_params=_params("arbitrary", "arbitrary"),
    )(page_table.reshape(-1), qlat, qpe, cnew, knew, cache_c, cache_k)


def _rope_tables(pos, rope):
    half = rope // 2
    inv_freq = ROPE_THETA ** (-jnp.arange(half, dtype=f32) / half)
    ang = pos.astype(f32)[:, None] * inv_freq
    cos, sin = jnp.cos(ang), jnp.sin(ang)
    pad = jnp.zeros((pos.shape[0], LANE - rope), f32)
    return (jnp.concatenate([cos, cos, pad], axis=-1), jnp.concatenate([-sin, sin, pad], axis=-1))


def kernel(x_prompt, x_sample, cache_ckv, cache_kpe, state_conv, state_rglru, page_table, c_prompt, c_sample, mod_w, mod_b, mix_norm, ffn_norm, rg_w_in, rg_conv_w, rg_conv_b, rg_w_gates, rg_b_gates, rg_lambda, rg_w_out, kv_mod_w, kv_mod_b, kv_norm, kv_w_dkv, kv_latent_norm, kv_w_uk, kv_w_uv, mla_w_dq, mla_q_norm, mla_w_uq, mla_w_o, moe_w_group, moe_b_group, moe_w_expert, moe_b_expert, moe_w13, moe_w2, final_norm):
    bp, seq, d = x_prompt.shape
    bs, steps, _ = x_sample.shape
    d_rnn = rg_conv_w.shape[-1]
    cwid = rg_conv_w.shape[1]
    nb_rnn, rnn_blk = rg_w_gates.shape[1], rg_w_gates.shape[2]
    r_kv, n_heads, d_nope = kv_w_uk.shape
    d_v = kv_w_uv.shape[2]
    rope = cache_kpe.shape[-1]
    half = rope // 2
    page = cache_ckv.shape[1]
    past = page_table.shape[1] * page
    n_groups = moe_w_group.shape[-1]
    n_exp = moe_w_expert.shape[-1]
    scale = 1.0 / math.sqrt(d_nope + rope)
    assert d_nope == LANE and d_v == LANE and rope <= LANE and d_rnn % LANE == 0
    assert rg_w_in.shape[0] == 1 and mla_w_dq.shape[0] == 1 and mod_w.shape[0] == 2
    assert steps >= cwid - 1 and seq >= cwid - 1

    n_p, n_s = bp * seq, steps * bs
    gp = _Group(n_p, min(512, seq), seq, False)
    gs_ = _Group(n_s, min(512, n_s), 0, True)
    groups = (gp, gs_)

    xp = x_prompt.reshape(n_p, d)
    xs = jnp.swapaxes(x_sample, 0, 1).reshape(n_s, d)

    n_c = bp + bs
    n_c_pad = -(-n_c // SUBLANE) * SUBLANE
    c_all = jnp.concatenate([c_prompt, c_sample, jnp.zeros((n_c_pad - n_c, d), f32)], axis=0)
    mods = [_bias_matmul(c_all, mod_w, layer, mod_b[layer]) for layer in range(2)]
    kv_mods = _bias_matmul(c_all, kv_mod_w[None], 0, kv_mod_b)

    def mod_vec(m, idx, grp_i):
        rows = slice(0, bp) if grp_i == 0 else slice(bp, bp + bs)
        return m[rows, idx * d:(idx + 1) * d]

    w_in = rg_w_in[0].astype(bf16)
    w_out = rg_w_out[0].astype(bf16)
    starts, win = _gate_windows(d_rnn, rnn_blk)
    wg = _gate_weights(rg_w_gates[0], starts, win)
    bgr = rg_b_gates[0][:, :rnn_blk].reshape(1, d_rnn)
    bgi = rg_b_gates[0][:, rnn_blk:].reshape(1, d_rnn)
    conv_w = rg_conv_w[0]
    conv_b = rg_conv_b[0].reshape(1, d_rnn)
    lam = rg_lambda[0].reshape(1, d_rnn)

    w_dkv = jnp.concatenate([kv_w_dkv, jnp.zeros((d, LANE - rope), f32)], axis=1).astype(bf16)
    w_up = jnp.concatenate([kv_w_uk.reshape(r_kv, n_heads * d_nope), kv_w_uv.reshape(r_kv, n_heads * d_v)],
                           axis=1).astype(bf16)
    w_dq = mla_w_dq[0].astype(bf16)
    wq3 = mla_w_uq[0].reshape(-1, n_heads, d_nope + rope)
    wq = jnp.concatenate([wq3, jnp.zeros((wq3.shape[0], n_heads, LANE - rope), f32)], axis=-1)
    wq = wq.reshape(-1, n_heads * 2 * LANE).astype(bf16)
    w_o = mla_w_o[0].astype(bf16)
    w_uk_t = jnp.transpose(kv_w_uk, (1, 2, 0))
    w_uv_flat = kv_w_uv.reshape(r_kv, n_heads * d_v)

    cos_p, sin_p = _rope_tables(jnp.arange(seq, dtype=jnp.int32), rope)
    cos_s, sin_s = _rope_tables(past + jnp.arange(steps, dtype=jnp.int32), rope)
    tabs = ((cos_p, sin_p), (cos_s, sin_s))

    def moe(xs_in, layer):
        xn_l, lg_l = [], []
        nr = -(-(n_groups + n_exp) // LANE) * LANE
        wr = jnp.concatenate([moe_w_group[layer], moe_w_expert[layer],
                              jnp.zeros((d, nr - n_groups - n_exp), f32)], axis=1)
        for gi, grp in enumerate(groups):
            xn, lg = _router(xs_in[gi], ffn_norm[layer], mod_vec(mods[layer], 3, gi),
                             mod_vec(mods[layer], 4, gi), wr, grp)
            xn_l.append(xn)
            lg_l.append(lg)
        xn_all = jnp.concatenate(xn_l, axis=0)
        logits = jnp.concatenate(lg_l, axis=0)
        expert_ids, weights = _route(logits, moe_b_group[layer], moe_b_expert[layer], n_groups)
        slot_tok, block_expert, n_used, dest = _dispatch(expert_ids, n_exp)
        xb = xn_all[slot_tok]
        yb = _experts(xb, block_expert, n_used, moe_w13, moe_w2, layer)
        ya = yb[dest[:, 0]]
        yc = yb[dest[:, 1]]
        out = []
        lo = 0
        for grp in groups:
            out.append((ya[lo:lo + grp.n], yc[lo:lo + grp.n], weights[lo:lo + grp.n]))
            lo += grp.n
        return out

    x_cur = [xp, xs]
    projs, h_last = [], []
    for gi, grp in enumerate(groups):
        proj = _normmod_matmul(x_cur[gi], mix_norm[0], mod_vec(mods[0], 0, gi), mod_vec(mods[0], 1, gi),
                               w_in, grp, tn_cap=896)
        projs.append(proj)
        if gi == 0:
            hg, hl = _rglru_prompt(proj, jnp.zeros((bp, 1, d_rnn), f32), jnp.zeros((bp, SUBLANE, d_rnn), f32),
                                   conv_w, conv_b, wg, bgr, bgi, lam, starts, win, bp, seq, min(256, seq))
            hl = hl.reshape(bp, d_rnn)
        else:
            cst = jnp.swapaxes(state_conv[0], 0, 1)
            hg, hl = _rglru_sample(proj.reshape(steps, bs, 2 * d_rnn), state_rglru[0], cst, conv_w, conv_b,
                                   wg, bgr, bgi, lam, starts, win)
            hg = hg.reshape(n_s, d_rnn)
        h_last.append(hl)
        x_cur[gi] = _matmul_res(hg, w_out, x_cur[gi], mod_vec(mods[0], 2, gi), grp)
    moe_out = moe(x_cur, 0)
    for gi, grp in enumerate(groups):
        ya, yc, wts = moe_out[gi]
        x_cur[gi] = _combine(x_cur[gi], ya, yc, wts, mod_vec(mods[0], 5, gi), grp)

    kv_out = []
    for gi, grp in enumerate(groups):
        kv_out.append(_kv_latent(x_cur[gi], kv_norm, kv_mods[:, :d][(slice(0, bp) if gi == 0 else slice(bp, bp + bs))],
                                 kv_mods[:, d:][(slice(0, bp) if gi == 0 else slice(bp, bp + bs))],
                                 w_dkv, kv_latent_norm, tabs[gi][0], tabs[gi][1], grp, half,
                                 w_up=w_up if gi == 0 else None))
    qs = [_q_proj(x_cur[gi], mix_norm[1], mod_vec(mods[1], 0, gi), mod_vec(mods[1], 1, gi), w_dq,
                  mla_q_norm[0], wq, tabs[gi][0], tabs[gi][1], grp, n_heads, half)
          for gi, grp in enumerate(groups)]

    ckv_p, kpe_p, kn_p, v_p, kpeb_p = kv_out[0]
    tq = min(256, seq)
    hgrp = 4 if n_heads % 4 == 0 else 1
    attn_p = _attn_prompt(qs[0], kn_p, v_p, kpeb_p, bp, seq, n_heads, scale, tq, hgrp)

    ckv_s, kpe_s = kv_out[1]
    rows = n_heads * steps
    qlat = _head_matmul(
        qs[1], pl.BlockSpec((n_s, LANE), lambda h: (0, 2 * h)),
        w_uk_t, pl.BlockSpec((None, d_nope, r_kv), lambda h: (h, 0, 0)),
        jax.ShapeDtypeStruct((n_heads, n_s, r_kv), bf16),
        pl.BlockSpec((None, n_s, r_kv), lambda h: (h, 0, 0)), n_heads)
    qlat = jnp.transpose(qlat.reshape(n_heads, steps, bs, r_kv), (2, 0, 1, 3)).reshape(bs, rows, r_kv)
    qpe = qs[1].reshape(steps, bs, n_heads, 2, LANE)[:, :, :, 1]
    qpe = jnp.transpose(qpe, (1, 2, 0, 3)).reshape(bs, rows, LANE)

    def new_keys(a):
        a = jnp.swapaxes(a.reshape(steps, bs, a.shape[-1]), 0, 1).astype(bf16)
        return jnp.concatenate([a, jnp.zeros((bs, LANE - steps, a.shape[-1]), bf16)], axis=1)

    ppc = _pick_tile(page_table.shape[1], DECODE_PAGES, 1)
    olat = _decode_attn(page_table, qlat, qpe, new_keys(ckv_s), new_keys(kpe_s), cache_ckv, cache_kpe,
                        scale, steps, ppc)
    olat = jnp.transpose(olat.reshape(bs, n_heads, steps, r_kv), (1, 2, 0, 3)).reshape(n_heads, n_s, r_kv)
    attn_s = _head_matmul(
        olat, pl.BlockSpec((None, n_s, r_kv), lambda h: (h, 0, 0)),
        w_uv_flat, pl.BlockSpec((r_kv, d_v), lambda h: (0, h)),
        jax.ShapeDtypeStruct((n_s, n_heads * d_v), bf16),
        pl.BlockSpec((n_s, d_v), lambda h: (0, h)), n_heads)

    attn = [attn_p, attn_s]
    for gi, grp in enumerate(groups):
        x_cur[gi] = _matmul_res(attn[gi], w_o, x_cur[gi], mod_vec(mods[1], 2, gi), grp)
    moe_out = moe(x_cur, 1)
    ys = []
    for gi, grp in enumerate(groups):
        ya, yc, wts = moe_out[gi]
        ys.append(_combine(x_cur[gi], ya, yc, wts, mod_vec(mods[1], 5, gi), grp, final_gain=final_norm))

    y_prompt = ys[0].reshape(bp, seq, d)
    y_sample = jnp.swapaxes(ys[1].reshape(steps, bs, d), 0, 1)
    conv_p = projs[0].reshape(bp, seq, 2 * d_rnn)[:, seq - (cwid - 1):, d_rnn:][None]
    conv_s = jnp.swapaxes(projs[1].reshape(steps, bs, 2 * d_rnn)[steps - (cwid - 1):, :, d_rnn:], 0, 1)[None]
    h_p = h_last[0][None]
    h_s = h_last[1][None]
    ckv_prompt = ckv_p.reshape(bp, seq, r_kv)
    kpe_prompt = kpe_p[:, :rope].reshape(bp, seq, rope)
    ckv_sample = jnp.swapaxes(ckv_s.reshape(steps, bs, r_kv), 0, 1)
    kpe_sample = jnp.swapaxes(kpe_s[:, :rope].reshape(steps, bs, rope), 0, 1)
    return (y_prompt, y_sample, conv_p, h_p, ckv_prompt, kpe_prompt, conv_s, h_s, ckv_sample, kpe_sample)
```

```python
import functools
import math
from typing import NamedTuple

import jax
import jax.numpy as jnp
from jax import lax
from jax.experimental import pallas as pl
from jax.experimental.pallas import tpu as pltpu

EPS = 1e-6
LRU_C = 8.0
ROPE_THETA = 10000.0
TOP_K = 2
LANE = 128
SUBLANE = 8
VMEM_LIMIT = 56 * 1024 * 1024
MOE_ROWS = 128
ATTN_ROWS = 512
SAMPLE_ROWS = 64
DECODE_PAGES = 32
LOG2E = 1.4426950408889634

f32 = jnp.float32
bf16 = jnp.bfloat16


def _params(*sem):
    return pltpu.CompilerParams(dimension_semantics=sem, vmem_limit_bytes=VMEM_LIMIT)


def _pick_tile(n, cap, mult=LANE):
    if n <= cap:
        return n
    best = None
    for t in range(mult, cap + 1, mult):
        if n % t == 0:
            best = t
    assert best is not None, (n, cap, mult)
    return best


class _Group(NamedTuple):
    n: int
    tm: int
    seq: int
    per_token: bool


def _mod_operand(vec, grp):
    d = vec.shape[-1]
    if grp.per_token:
        arr = jnp.tile(vec, (grp.n // vec.shape[0], 1))
        return arr, pl.BlockSpec((grp.tm, d), lambda i, *_: (i, 0))
    per_seq = grp.seq // grp.tm
    return vec[:, None, :], pl.BlockSpec((None, 1, d), lambda i, *_: (i // per_seq, 0, 0))


def _pos_operand(tab, grp):
    if grp.per_token:
        arr = jnp.repeat(tab, grp.n // tab.shape[0], axis=0)
        return arr, pl.BlockSpec((grp.tm, LANE), lambda i, *_: (i, 0))
    per_seq = grp.seq // grp.tm
    return tab, pl.BlockSpec((grp.tm, LANE), lambda i, *_: (i % per_seq, 0))


def _row_spec(grp, width):
    return pl.BlockSpec((grp.tm, width), lambda i, *_: (i, 0))


def _full_spec(shape):
    nd = len(shape)
    return pl.BlockSpec(shape, lambda *_: (0,) * nd)


def _rms(x, gain):
    return x * lax.rsqrt(jnp.mean(x * x, axis=-1, keepdims=True) + EPS) * gain


def _normmod(x, gain, shift, scale):
    return _rms(x, gain) * (1.0 + scale) + shift


def _rope_tile(t, cos, sin, half):
    lane = lax.broadcasted_iota(jnp.int32, t.shape, 1)
    rot = jnp.where(lane < half, pltpu.roll(t, LANE - half, 1), pltpu.roll(t, half, 1))
    return t * cos + rot * sin


def _sigmoid(x):
    return 0.5 * (jnp.tanh(0.5 * x) + 1.0)


def _gelu(x):
    return 0.5 * x * (1.0 + jnp.tanh(math.sqrt(2.0 / math.pi) * (x + 0.044715 * (x * x * x))))


def _softplus(z):
    return jnp.maximum(z, 0.0) + jnp.log1p(jnp.exp(-jnp.abs(z)))


def _bias_matmul_kernel(a_ref, w_ref, b_ref, o_ref):
    o_ref[...] = jnp.dot(a_ref[...].astype(bf16), w_ref[...].astype(bf16),
                         preferred_element_type=f32) + b_ref[...]


def _bias_matmul(a, w3, layer, b):
    m, k = a.shape
    n = w3.shape[-1]
    tn = _pick_tile(n, 1024)
    return pl.pallas_call(
        _bias_matmul_kernel,
        grid=(n // tn,),
        in_specs=[_full_spec((m, k)),
                  pl.BlockSpec((None, k, tn), lambda j: (layer, 0, j)),
                  pl.BlockSpec((1, tn), lambda j: (0, j))],
        out_specs=pl.BlockSpec((m, tn), lambda j: (0, j)),
        out_shape=jax.ShapeDtypeStruct((m, n), f32),
        compiler_params=_params("arbitrary"),
    )(a, w3, b.reshape(1, n))


def _normmod_matmul_kernel(x_ref, gain_ref, shift_ref, scale_ref, w_ref, o_ref, xn_ref):
    @pl.when(pl.program_id(1) == 0)
    def _():
        xn_ref[...] = _normmod(x_ref[...], gain_ref[...], shift_ref[...], scale_ref[...]).astype(bf16)

    o_ref[...] = jnp.dot(xn_ref[...], w_ref[...], preferred_element_type=f32).astype(o_ref.dtype)


def _normmod_matmul(x, gain, shift, scale, w, grp, tn_cap=1024, out_dtype=f32):
    d = x.shape[1]
    n = w.shape[1]
    tn = _pick_tile(n, tn_cap)
    sh, sh_spec = _mod_operand(shift, grp)
    sc, sc_spec = _mod_operand(scale, grp)
    return pl.pallas_call(
        _normmod_matmul_kernel,
        grid=(grp.n // grp.tm, n // tn),
        in_specs=[_row_spec(grp, d), _full_spec((1, d)), sh_spec, sc_spec,
                  pl.BlockSpec((d, tn), lambda i, j: (0, j))],
        out_specs=pl.BlockSpec((grp.tm, tn), lambda i, j: (i, j)),
        out_shape=jax.ShapeDtypeStruct((grp.n, n), out_dtype),
        scratch_shapes=[pltpu.VMEM((grp.tm, d), bf16)],
        compiler_params=_params("arbitrary", "arbitrary"),
    )(x, gain.reshape(1, d), sh, sc, w)


def _matmul_res_kernel(a_ref, w_ref, res_ref, gate_ref, o_ref):
    y = jnp.dot(a_ref[...].astype(bf16), w_ref[...], preferred_element_type=f32)
    o_ref[...] = res_ref[...] + gate_ref[...] * y


def _matmul_res(a, w, res, gate, grp, tn_cap=1024):
    k = a.shape[1]
    n = w.shape[1]
    tn = _pick_tile(n, tn_cap)
    if grp.per_token:
        g_arr = jnp.tile(gate, (grp.n // gate.shape[0], 1))
        g_spec = pl.BlockSpec((grp.tm, tn), lambda j, i: (i, j))
    else:
        per_seq = grp.seq // grp.tm
        g_arr = gate[:, None, :]
        g_spec = pl.BlockSpec((None, 1, tn), lambda j, i: (i // per_seq, 0, j))
    return pl.pallas_call(
        _matmul_res_kernel,
        grid=(n // tn, grp.n // grp.tm),
        in_specs=[pl.BlockSpec((grp.tm, k), lambda j, i: (i, 0)),
                  pl.BlockSpec((k, tn), lambda j, i: (0, j)),
                  pl.BlockSpec((grp.tm, tn), lambda j, i: (i, j)),
                  g_spec],
        out_specs=pl.BlockSpec((grp.tm, tn), lambda j, i: (i, j)),
        out_shape=jax.ShapeDtypeStruct((grp.n, n), f32),
        compiler_params=_params("arbitrary", "arbitrary"),
    )(a, w, res, g_arr)


def _gate_windows(d_rnn, blk):
    nt = d_rnn // LANE
    raw, need = [], 0
    for j in range(nt):
        n0 = (LANE * j) // blk
        n1 = (LANE * j + LANE - 1) // blk
        s = (blk * n0) // LANE * LANE
        raw.append(s)
        need = max(need, blk * (n1 + 1) - s)
    win = min(d_rnn, -(-need // LANE) * LANE)
    return [min(s, d_rnn - win) for s in raw], win


def _gate_weights(w_gates, starts, win):
    nb, blk, _ = w_gates.shape
    d = nb * blk
    eye = jnp.eye(nb, dtype=w_gates.dtype)

    def dense(w):
        return (w[:, :, None, :] * eye[:, None, :, None]).reshape(d, d).astype(bf16)

    wr = dense(w_gates[:, :, :blk])
    wi = dense(w_gates[:, :, blk:])
    tiles = [jnp.concatenate([wr[s:s + win, j * LANE:(j + 1) * LANE], wi[s:s + win, j * LANE:(j + 1) * LANE]], axis=1)
             for j, s in enumerate(starts)]
    return jnp.stack(tiles)


def _lru_inputs(g, b_r, b_i, neg_c_sp, u):
    r = _sigmoid(g[:, :LANE] + b_r)
    i = _sigmoid(g[:, LANE:] + b_i)
    log_a = neg_c_sp * r
    a = jnp.exp(log_a)
    x = jnp.sqrt(-jnp.tanh(log_a) * (a * a + 1.0)) * (i * u)
    return a, x


def _rglru_prompt_kernel(proj_ref, h0_ref, cinit_ref, cw_ref, cb_ref, wg_ref, bgr_ref, bgi_ref, lam_ref,
                         hg_ref, hlast_ref, ubuf, ucf, ucb, a_scr, x_scr, hcar,
                         *, starts, win, tc, d_rnn, cwid):
    t = pl.program_id(1)
    hist = SUBLANE

    @pl.when(t == 0)
    def _():
        ubuf[0:hist, :] = cinit_ref[...]
        hcar[...] = h0_ref[...]

    ubuf[hist:hist + tc, :] = proj_ref[:, d_rnn:]
    uc = cb_ref[...]
    for k in range(cwid):
        off = hist - (cwid - 1) + k
        uc = uc + cw_ref[k:k + 1, :] * ubuf[off:off + tc, :]
    ucf[...] = uc
    ucb[...] = uc.astype(bf16)
    ubuf[0:hist, :] = ubuf[tc:tc + hist, :]

    neg_c_sp = -LRU_C * _softplus(-lam_ref[...])
    seg = tc // SUBLANE
    for j in range(d_rnn // LANE):
        cs = slice(j * LANE, (j + 1) * LANE)
        g = jnp.dot(ucb[:, starts[j]:starts[j] + win], wg_ref[j], preferred_element_type=f32)
        a, x = _lru_inputs(g, bgr_ref[:, cs], bgi_ref[:, cs], neg_c_sp[:, cs], ucf[:, cs])
        a_scr[j] = a
        x_scr[j] = x
        h = jnp.zeros((SUBLANE, LANE), f32)
        p = jnp.ones((SUBLANE, LANE), f32)
        for k in range(seg):
            ak = a_scr[j, pl.ds(k, SUBLANE, stride=seg), :]
            xk = x_scr[j, pl.ds(k, SUBLANE, stride=seg), :]
            h = ak * h + xk
            p = p * ak
            x_scr[j, pl.ds(k, SUBLANE, stride=seg), :] = h
            a_scr[j, pl.ds(k, SUBLANE, stride=seg), :] = p
        c = hcar[:, cs]
        outs = []
        for r in range(SUBLANE):
            rows = slice(r * seg, (r + 1) * seg)
            outs.append(x_scr[j, rows, :] + a_scr[j, rows, :] * c)
            c = p[r:r + 1, :] * c + h[r:r + 1, :]
        hcar[:, cs] = c
        hs = jnp.concatenate(outs, axis=0)
        hg_ref[:, cs] = (hs * _gelu(proj_ref[:, cs])).astype(hg_ref.dtype)
    hlast_ref[...] = hcar[...]


def _rglru_prompt(proj, h0, conv_init, conv_w, conv_b, wg, bgr, bgi, lam, starts, win, nb_seq, seq, tc):
    d_rnn = proj.shape[1] // 2
    cwid = conv_w.shape[0]
    nt = d_rnn // LANE
    per_seq = seq // tc
    kern = functools.partial(_rglru_prompt_kernel, starts=tuple(starts), win=win, tc=tc, d_rnn=d_rnn, cwid=cwid)
    return pl.pallas_call(
        kern,
        grid=(nb_seq, per_seq),
        in_specs=[pl.BlockSpec((tc, 2 * d_rnn), lambda b, t: (b * per_seq + t, 0)),
                  pl.BlockSpec((None, 1, d_rnn), lambda b, t: (b, 0, 0)),
                  pl.BlockSpec((None, SUBLANE, d_rnn), lambda b, t: (b, 0, 0)),
                  _full_spec((cwid, d_rnn)), _full_spec((1, d_rnn)),
                  _full_spec(wg.shape), _full_spec((1, d_rnn)), _full_spec((1, d_rnn)), _full_spec((1, d_rnn))],
        out_specs=[pl.BlockSpec((tc, d_rnn), lambda b, t: (b * per_seq + t, 0)),
                   pl.BlockSpec((None, 1, d_rnn), lambda b, t: (b, 0, 0))],
        out_shape=[jax.ShapeDtypeStruct((nb_seq * seq, d_rnn), bf16),
                   jax.ShapeDtypeStruct((nb_seq, 1, d_rnn), f32)],
        scratch_shapes=[pltpu.VMEM((tc + SUBLANE, d_rnn), f32),
                        pltpu.VMEM((tc, d_rnn), f32),
                        pltpu.VMEM((tc, d_rnn), bf16),
                        pltpu.VMEM((nt, tc, LANE), f32),
                        pltpu.VMEM((nt, tc, LANE), f32),
                        pltpu.VMEM((1, d_rnn), f32)],
        compiler_params=_params("arbitrary", "arbitrary"),
    )(proj, h0, conv_init, conv_w, conv_b, wg, bgr, bgi, lam)


def _rglru_sample_kernel(proj_ref, h0_ref, cst_ref, cw_ref, cb_ref, wg_ref, bgr_ref, bgi_ref, lam_ref,
                         hg_ref, hlast_ref, ucf, ucb, *, starts, win, steps, tb, d_rnn, cwid):
    def up(tp):
        if tp < cwid - 1:
            return cst_ref[tp]
        return proj_ref[tp - cwid + 1, :, d_rnn:]

    for t in range(steps):
        uc = cb_ref[...]
        for k in range(cwid):
            uc = uc + cw_ref[k:k + 1, :] * up(t + k)
        ucf[t * tb:(t + 1) * tb, :] = uc
        ucb[t * tb:(t + 1) * tb, :] = uc.astype(bf16)

    neg_c_sp = -LRU_C * _softplus(-lam_ref[...])
    for j in range(d_rnn // LANE):
        cs = slice(j * LANE, (j + 1) * LANE)
        g = jnp.dot(ucb[:, starts[j]:starts[j] + win], wg_ref[j], preferred_element_type=f32)
        h = h0_ref[:, cs]
        for t in range(steps):
            rows = slice(t * tb, (t + 1) * tb)
            a, x = _lru_inputs(g[rows], bgr_ref[:, cs], bgi_ref[:, cs], neg_c_sp[:, cs], ucf[rows, cs])
            h = a * h + x
            hg_ref[t, :, cs] = (h * _gelu(proj_ref[t, :, cs])).astype(hg_ref.dtype)
        hlast_ref[:, cs] = h


def _rglru_sample(proj, h0, conv_state, conv_w, conv_b, wg, bgr, bgi, lam, starts, win):
    steps, nb, _ = proj.shape
    d_rnn = proj.shape[2] // 2
    cwid = conv_w.shape[0]
    tb = min(SAMPLE_ROWS, nb)
    kern = functools.partial(_rglru_sample_kernel, starts=tuple(starts), win=win, steps=steps, tb=tb,
                             d_rnn=d_rnn, cwid=cwid)
    return pl.pallas_call(
        kern,
        grid=(nb // tb,),
        in_specs=[pl.BlockSpec((steps, tb, 2 * d_rnn), lambda i: (0, i, 0)),
                  pl.BlockSpec((tb, d_rnn), lambda i: (i, 0)),
                  pl.BlockSpec((cwid - 1, tb, d_rnn), lambda i: (0, i, 0)),
                  _full_spec((cwid, d_rnn)), _full_spec((1, d_rnn)),
                  _full_spec(wg.shape), _full_spec((1, d_rnn)), _full_spec((1, d_rnn)), _full_spec((1, d_rnn))],
        out_specs=[pl.BlockSpec((steps, tb, d_rnn), lambda i: (0, i, 0)),
                   pl.BlockSpec((tb, d_rnn), lambda i: (i, 0))],
        out_shape=[jax.ShapeDtypeStruct((steps, nb, d_rnn), bf16),
                   jax.ShapeDtypeStruct((nb, d_rnn), f32)],
        scratch_shapes=[pltpu.VMEM((steps * tb, d_rnn), f32), pltpu.VMEM((steps * tb, d_rnn), bf16)],
        compiler_params=_params("arbitrary"),
    )(proj, h0, conv_state, conv_w, conv_b, wg, bgr, bgi, lam)


def _first_max(vals, ids):
    m = jnp.max(vals, axis=-1, keepdims=True)
    idx = jnp.min(jnp.where(vals == m, ids, jnp.int32(2 ** 30)), axis=-1, keepdims=True)
    return m, idx


def _router_kernel(x_ref, gain_ref, shift_ref, scale_ref, wh_ref, wl_ref, b_ref, xn_ref, rt_ref,
                   *, n_groups, n_exp):
    xn = _normmod(x_ref[...], gain_ref[...], shift_ref[...], scale_ref[...])
    xn_ref[...] = xn
    xh = xn.astype(bf16)
    xl = (xn - xh.astype(f32)).astype(bf16)
    lg = (jnp.dot(xh, wh_ref[...], preferred_element_type=f32)
          + jnp.dot(xl, wh_ref[...], preferred_element_type=f32)
          + jnp.dot(xh, wl_ref[...], preferred_element_type=f32)
          + jnp.dot(xl, wl_ref[...], preferred_element_type=f32)) + b_ref[...]
    per = n_exp // n_groups
    lane = lax.broadcasted_iota(jnp.int32, lg.shape, 1)
    neg = -jnp.inf
    gl = jnp.where(lane < n_groups, lg, neg)
    gmax, g_idx = _first_max(gl, lane)
    g_w = 1.0 / jnp.sum(jnp.exp(gl - gmax), axis=-1, keepdims=True)
    e_id = lane - n_groups
    lo = g_idx * per
    el = jnp.where(e_id >= lo, jnp.where(e_id < lo + per, lg, neg), neg)
    l1, i1 = _first_max(el, e_id)
    el2 = jnp.where(e_id == i1, neg, el)
    l2, i2 = _first_max(el2, e_id)
    e = jnp.exp(l2 - l1)
    w1 = g_w / (1.0 + e)
    w2 = w1 * e
    rt_ref[...] = jnp.where(lane == 0, i1.astype(f32),
                            jnp.where(lane == 1, i2.astype(f32),
                                      jnp.where(lane == 2, w1, jnp.where(lane == 3, w2, 0.0))))


def _router(x, gain, shift, scale, wr, br, grp, n_groups, n_exp):
    d = x.shape[1]
    nr = wr.shape[1]
    wh = wr.astype(bf16)
    wl = (wr - wh.astype(f32)).astype(bf16)
    sh, sh_spec = _mod_operand(shift, grp)
    sc, sc_spec = _mod_operand(scale, grp)
    return pl.pallas_call(
        functools.partial(_router_kernel, n_groups=n_groups, n_exp=n_exp),
        grid=(grp.n // grp.tm,),
        in_specs=[_row_spec(grp, d), _full_spec((1, d)), sh_spec, sc_spec,
                  _full_spec((d, nr)), _full_spec((d, nr)), _full_spec((1, nr))],
        out_specs=[_row_spec(grp, d), _row_spec(grp, nr)],
        out_shape=[jax.ShapeDtypeStruct((grp.n, d), f32), jax.ShapeDtypeStruct((grp.n, nr), f32)],
        compiler_params=_params("arbitrary"),
    )(x, gain.reshape(1, d), sh, sc, wh, wl, br)


def _experts_kernel(be_ref, nu_ref, xb_ref, w13_ref, w2_ref, o_ref, w13b, w2b, *, d_exp):
    i = pl.program_id(0)
    e = be_ref[i]
    prev = be_ref[jnp.maximum(i - 1, 0)]

    @pl.when((i == 0) | (e != prev))
    def _():
        w13b[...] = w13_ref[...].astype(bf16)
        w2b[...] = w2_ref[...].astype(bf16)

    @pl.when(i < nu_ref[0])
    def _():
        gu = jnp.dot(xb_ref[...].astype(bf16), w13b[...], preferred_element_type=f32)
        g = gu[:, :d_exp]
        act = (g * _sigmoid(g)) * gu[:, d_exp:]
        o_ref[...] = jnp.dot(act.astype(bf16), w2b[...], preferred_element_type=f32)

    @pl.when(i >= nu_ref[0])
    def _():
        o_ref[...] = jnp.zeros_like(o_ref)


def _experts(xb, block_expert, n_used, w13, w2, layer):
    rows, d = xb.shape
    n_blocks = rows // MOE_ROWS
    d_exp = w2.shape[2]
    gs = pltpu.PrefetchScalarGridSpec(
        num_scalar_prefetch=2,
        grid=(n_blocks,),
        in_specs=[pl.BlockSpec((MOE_ROWS, d), lambda i, be, nu: (i, 0)),
                  pl.BlockSpec((None, None, d, 2 * d_exp), lambda i, be, nu: (layer, be[i], 0, 0)),
                  pl.BlockSpec((None, None, d_exp, d), lambda i, be, nu: (layer, be[i], 0, 0))],
        out_specs=pl.BlockSpec((MOE_ROWS, d), lambda i, be, nu: (i, 0)),
        scratch_shapes=[pltpu.VMEM((d, 2 * d_exp), bf16), pltpu.VMEM((d_exp, d), bf16)],
    )
    return pl.pallas_call(
        functools.partial(_experts_kernel, d_exp=d_exp),
        grid_spec=gs,
        out_shape=jax.ShapeDtypeStruct((rows, d), f32),
        compiler_params=_params("arbitrary"),
    )(block_expert, n_used, xb, w13, w2)


def _combine_kernel(x_ref, ya_ref, yb_ref, wt_ref, gate_ref, *rest, final):
    if final:
        fg_ref, o_ref = rest
    else:
        (o_ref,) = rest
    wt = wt_ref[...]
    y = x_ref[...] + gate_ref[...] * (wt[:, 0:1] * ya_ref[...] + wt[:, 1:2] * yb_ref[...])
    if final:
        y = _rms(y, fg_ref[...])
    o_ref[...] = y


def _combine(x, ya, yb, wts, gate, grp, final_gain=None):
    d = x.shape[1]
    g_arr, g_spec = _mod_operand(gate, grp)
    final = final_gain is not None
    args = [x, ya, yb, wts, g_arr]
    specs = [_row_spec(grp, d), _row_spec(grp, d), _row_spec(grp, d), _row_spec(grp, TOP_K), g_spec]
    if final:
        args.append(final_gain.reshape(1, d))
        specs.append(_full_spec((1, d)))
    return pl.pallas_call(
        functools.partial(_combine_kernel, final=final),
        grid=(grp.n // grp.tm,),
        in_specs=specs,
        out_specs=_row_spec(grp, d),
        out_shape=jax.ShapeDtypeStruct((grp.n, d), f32),
        compiler_params=_params("arbitrary"),
    )(*args)


def _dispatch(expert_ids, n_exp):
    n = expert_ids.shape[0]
    na = n * TOP_K
    flat_e = expert_ids.reshape(na)
    onehot = (flat_e[:, None] == jnp.arange(n_exp, dtype=jnp.int32)[None, :]).astype(jnp.int32)
    incl = jnp.cumsum(onehot, axis=0)
    counts = incl[-1]
    padded = (counts + MOE_ROWS - 1) // MOE_ROWS * MOE_ROWS
    pad_end = jnp.cumsum(padded)
    pad_start = pad_end - padded
    dest = jnp.sum(onehot * (pad_start[None, :] + incl - 1), axis=1)
    n_blocks = (na + MOE_ROWS - 1) // MOE_ROWS + n_exp
    tok = jnp.arange(na, dtype=jnp.int32) // TOP_K
    slot_tok = jnp.zeros((n_blocks * MOE_ROWS,), jnp.int32).at[dest].set(tok)
    block_start = jnp.arange(n_blocks, dtype=jnp.int32) * MOE_ROWS
    block_expert = jnp.minimum(jnp.sum((pad_end[None, :] <= block_start[:, None]).astype(jnp.int32), axis=1),
                               n_exp - 1)
    n_used = pad_end[-1:] // MOE_ROWS
    return slot_tok, block_expert, n_used, dest.reshape(n, TOP_K)


def _kv_kernel(x_ref, gain_ref, shift_ref, scale_ref, w_ref, lg_ref, cos_ref, sin_ref, *rest,
               r_kv, half, with_up):
    if with_up:
        wup_ref, ckv_ref, kpe_ref, kn_ref, v_ref, kpeb_ref = rest
    else:
        ckv_ref, kpe_ref = rest
    a = _normmod(x_ref[...], gain_ref[...], shift_ref[...], scale_ref[...]).astype(bf16)
    kv = jnp.dot(a, w_ref[...], preferred_element_type=f32)
    c = _rms(kv[:, :r_kv], lg_ref[...])
    ckv_ref[...] = c
    kp = _rope_tile(kv[:, r_kv:], cos_ref[...], sin_ref[...], half)
    kpe_ref[...] = kp
    if with_up:
        up = jnp.dot(c.astype(bf16), wup_ref[...], preferred_element_type=f32)
        hw = up.shape[1] // 2
        kn_ref[...] = up[:, :hw].astype(bf16)
        v_ref[...] = up[:, hw:].astype(bf16)
        kpeb_ref[...] = kp.astype(bf16)


def _kv_latent(x, gain, shift, scale, w_pad, latent_gain, cos, sin, grp, half, w_up=None):
    d = x.shape[1]
    r_kv = latent_gain.shape[0]
    with_up = w_up is not None
    sh, sh_spec = _mod_operand(shift, grp)
    sc, sc_spec = _mod_operand(scale, grp)
    cs, cs_spec = _pos_operand(cos, grp)
    sn, sn_spec = _pos_operand(sin, grp)
    args = [x, gain.reshape(1, d), sh, sc, w_pad, latent_gain.reshape(1, r_kv), cs, sn]
    specs = [_row_spec(grp, d), _full_spec((1, d)), sh_spec, sc_spec, _full_spec(w_pad.shape),
             _full_spec((1, r_kv)), cs_spec, sn_spec]
    out_specs = [_row_spec(grp, r_kv), _row_spec(grp, LANE)]
    out_shape = [jax.ShapeDtypeStruct((grp.n, r_kv), f32), jax.ShapeDtypeStruct((grp.n, LANE), f32)]
    if with_up:
        hw = w_up.shape[1] // 2
        args.append(w_up)
        specs.append(_full_spec(w_up.shape))
        out_specs += [_row_spec(grp, hw), _row_spec(grp, hw), _row_spec(grp, LANE)]
        out_shape += [jax.ShapeDtypeStruct((grp.n, hw), bf16), jax.ShapeDtypeStruct((grp.n, hw), bf16),
                      jax.ShapeDtypeStruct((grp.n, LANE), bf16)]
    return pl.pallas_call(
        functools.partial(_kv_kernel, r_kv=r_kv, half=half, with_up=with_up),
        grid=(grp.n // grp.tm,),
        in_specs=specs, out_specs=out_specs, out_shape=out_shape,
        compiler_params=_params("arbitrary"),
    )(*args)


def _q_kernel(x_ref, gain_ref, shift_ref, scale_ref, wdq_ref, qg_ref, wq_ref, cos_ref, sin_ref, q_ref,
              *, n_heads, half):
    a = _normmod(x_ref[...], gain_ref[...], shift_ref[...], scale_ref[...]).astype(bf16)
    ql = jnp.dot(a, wdq_ref[...], preferred_element_type=f32)
    qn = _rms(ql, qg_ref[...]).astype(bf16)
    cos = cos_ref[...]
    sin = sin_ref[...]
    for h in range(n_heads):
        q = jnp.dot(qn, wq_ref[:, 2 * LANE * h:2 * LANE * (h + 1)], preferred_element_type=f32)
        q_ref[:, 2 * LANE * h:2 * LANE * h + LANE] = q[:, :LANE].astype(bf16)
        q_ref[:, 2 * LANE * h + LANE:2 * LANE * (h + 1)] = _rope_tile(q[:, LANE:], cos, sin, half).astype(bf16)


def _q_proj(x, gain, shift, scale, w_dq, q_gain, wq, cos, sin, grp, n_heads, half):
    d = x.shape[1]
    rq = w_dq.shape[1]
    sh, sh_spec = _mod_operand(shift, grp)
    sc, sc_spec = _mod_operand(scale, grp)
    cs, cs_spec = _pos_operand(cos, grp)
    sn, sn_spec = _pos_operand(sin, grp)
    return pl.pallas_call(
        functools.partial(_q_kernel, n_heads=n_heads, half=half),
        grid=(grp.n // grp.tm,),
        in_specs=[_row_spec(grp, d), _full_spec((1, d)), sh_spec, sc_spec, _full_spec(w_dq.shape),
                  _full_spec((1, rq)), _full_spec(wq.shape), cs_spec, sn_spec],
        out_specs=_row_spec(grp, wq.shape[1]),
        out_shape=jax.ShapeDtypeStruct((grp.n, wq.shape[1]), bf16),
        compiler_params=_params("arbitrary"),
    )(x, gain.reshape(1, d), sh, sc, w_dq, q_gain.reshape(1, rq), wq, cs, sn)


def _attn_kernel(q_ref, kn_ref, v_ref, kpe_ref, o_ref, m_scr, l_scr, acc_scr, *, hg, tq, c):
    qi = pl.program_id(2)
    m_scr[...] = jnp.full_like(m_scr, -jnp.inf)
    l_scr[...] = jnp.zeros_like(l_scr)
    acc_scr[...] = jnp.zeros_like(acc_scr)

    def block(ks, masked):
        kpe = kpe_ref[pl.ds(ks, tq), :]
        for h in range(hg):
            hs = slice(h * LANE, (h + 1) * LANE)
            k = jnp.concatenate([kn_ref[pl.ds(ks, tq), hs], kpe], axis=-1)
            s = lax.dot_general(q_ref[:, 2 * LANE * h:2 * LANE * (h + 1)], k, (((1,), (1,)), ((), ())),
                                preferred_element_type=f32)
            if masked:
                row = lax.broadcasted_iota(jnp.int32, (tq, tq), 0)
                col = lax.broadcasted_iota(jnp.int32, (tq, tq), 1)
                s = jnp.where(col <= row, s, -jnp.inf)
            m = m_scr[h]
            m_new = jnp.maximum(m, jnp.max(s, axis=-1, keepdims=True))
            p = jnp.exp2((s - m_new) * c)
            corr = jnp.exp2((m - m_new) * c)
            l_scr[h] = l_scr[h] * corr + jnp.sum(p, axis=-1, keepdims=True)
            acc_scr[h] = acc_scr[h] * corr + jnp.dot(p.astype(bf16), v_ref[pl.ds(ks, tq), hs],
                                                     preferred_element_type=f32)
            m_scr[h] = m_new

    def body(kb, carry):
        block(pl.multiple_of(kb * tq, tq), False)
        return carry

    lax.fori_loop(0, qi, body, 0)
    block(pl.multiple_of(qi * tq, tq), True)
    for h in range(hg):
        o_ref[:, h * LANE:(h + 1) * LANE] = (acc_scr[h] * (1.0 / l_scr[h])).astype(o_ref.dtype)


def _attn_prompt(q, kn, v, kpe, nb_seq, seq, n_heads, scale, tq, hg):
    nq = seq // tq
    return pl.pallas_call(
        functools.partial(_attn_kernel, hg=hg, tq=tq, c=scale * LOG2E),
        grid=(nb_seq, n_heads // hg, nq),
        in_specs=[pl.BlockSpec((tq, hg * 2 * LANE), lambda b, g, i: (b * nq + i, g)),
                  pl.BlockSpec((seq, hg * LANE), lambda b, g, i: (b, g)),
                  pl.BlockSpec((seq, hg * LANE), lambda b, g, i: (b, g)),
                  pl.BlockSpec((seq, LANE), lambda b, g, i: (b, 0))],
        out_specs=pl.BlockSpec((tq, hg * LANE), lambda b, g, i: (b * nq + i, g)),
        out_shape=jax.ShapeDtypeStruct((nb_seq * seq, n_heads * LANE), bf16),
        scratch_shapes=[pltpu.VMEM((hg, tq, 1), f32), pltpu.VMEM((hg, tq, 1), f32),
                        pltpu.VMEM((hg, tq, LANE), f32)],
        compiler_params=_params("arbitrary", "arbitrary", "arbitrary"),
    )(q, kn, v, kpe)


def _head_matmul_kernel(a_ref, w_ref, o_ref):
    o_ref[...] = jnp.dot(a_ref[...].astype(bf16), w_ref[...].astype(bf16),
                         preferred_element_type=f32).astype(o_ref.dtype)


def _head_matmul(a, a_spec, w, w_spec, out_shape, out_spec, n_heads):
    return pl.pallas_call(
        _head_matmul_kernel,
        grid=(n_heads,),
        in_specs=[a_spec, w_spec],
        out_specs=out_spec,
        out_shape=out_shape,
        compiler_params=_params("arbitrary"),
    )(a, w)


def _decode_kernel(pt_ref, qlat_ref, qpe_ref, cnew_ref, knew_ref, cache_c, cache_k, o_ref,
                   cbuf, kbuf, sem, m_scr, l_scr, acc_scr, *, ppc, n_chunks, n_pages, steps, c):
    b = pl.program_id(0)
    ch = pl.program_id(1)
    step = b * n_chunks + ch
    slot = step % 2
    total = pl.num_programs(0) * n_chunks

    page = cbuf.shape[2]

    def copies(bb, cc, sl, p):
        phys = pt_ref[bb * n_pages + cc * ppc + p]
        return (pltpu.make_async_copy(cache_c.at[phys], cbuf.at[sl, p], sem.at[0, sl]),
                pltpu.make_async_copy(cache_k.at[phys], kbuf.at[sl, :, pl.ds(p * page, page)], sem.at[1, sl]))

    def issue(bb, cc, sl):
        for p in range(ppc):
            for cp in copies(bb, cc, sl, p):
                cp.start()

    @pl.when(step == 0)
    def _():
        issue(0, 0, 0)

    @pl.when(step + 1 < total)
    def _():
        nxt = step + 1
        issue(nxt // n_chunks, nxt % n_chunks, 1 - slot)

    for p in range(ppc):
        for cp in copies(b, ch, slot, p):
            cp.wait()

    @pl.when(ch == 0)
    def _():
        m_scr[...] = jnp.full_like(m_scr, -jnp.inf)
        l_scr[...] = jnp.zeros_like(l_scr)
        acc_scr[...] = jnp.zeros_like(acc_scr)

    qlat = qlat_ref[...]
    qpe = qpe_ref[...]
    rope = kbuf.shape[1]
    nt = (((1,), (1,)), ((), ()))

    def update(s, vals):
        m = m_scr[...]
        m_new = jnp.maximum(m, jnp.max(s, axis=-1, keepdims=True))
        p = jnp.exp2((s - m_new) * c)
        corr = jnp.exp2((m - m_new) * c)
        l_scr[...] = l_scr[...] * corr + jnp.sum(p, axis=-1, keepdims=True)
        acc_scr[...] = acc_scr[...] * corr + jnp.dot(p.astype(bf16), vals, preferred_element_type=f32)
        m_scr[...] = m_new

    ck = cbuf[slot].reshape(ppc * page, cbuf.shape[3]).astype(bf16)
    s = (lax.dot_general(qlat, ck, nt, preferred_element_type=f32)
         + jnp.dot(qpe[:, :rope], kbuf[slot].astype(bf16), preferred_element_type=f32))
    update(s, ck)

    @pl.when(ch == n_chunks - 1)
    def _():
        cn = cnew_ref[...]
        sn = (lax.dot_general(qlat, cn, nt, preferred_element_type=f32)
              + lax.dot_general(qpe, knew_ref[...], nt, preferred_element_type=f32))
        t_row = lax.broadcasted_iota(jnp.int32, sn.shape, 0) % steps
        key = lax.broadcasted_iota(jnp.int32, sn.shape, 1)
        update(jnp.where(key <= t_row, sn, -jnp.inf), cn)
        o_ref[...] = (acc_scr[...] * (1.0 / l_scr[...])).astype(o_ref.dtype)


def _decode_attn(page_table, qlat, qpe, cnew, knew, cache_c, cache_k, scale, steps, ppc):
    nb, rows, r_kv = qlat.shape
    n_pages = page_table.shape[1]
    page = cache_c.shape[1]
    rope = cache_k.shape[1]
    n_chunks = n_pages // ppc
    gs = pltpu.PrefetchScalarGridSpec(
        num_scalar_prefetch=1,
        grid=(nb, n_chunks),
        in_specs=[pl.BlockSpec((None, rows, r_kv), lambda b, ch, pt: (b, 0, 0)),
                  pl.BlockSpec((None, rows, LANE), lambda b, ch, pt: (b, 0, 0)),
                  pl.BlockSpec((None, LANE, r_kv), lambda b, ch, pt: (b, 0, 0)),
                  pl.BlockSpec((None, LANE, LANE), lambda b, ch, pt: (b, 0, 0)),
                  pl.BlockSpec(memory_space=pl.ANY),
                  pl.BlockSpec(memory_space=pl.ANY)],
        out_specs=pl.BlockSpec((None, rows, r_kv), lambda b, ch, pt: (b, 0, 0)),
        scratch_shapes=[pltpu.VMEM((2, ppc, page, r_kv), f32),
                        pltpu.VMEM((2, rope, ppc * page), f32),
                        pltpu.SemaphoreType.DMA((2, 2)),
                        pltpu.VMEM((rows, 1), f32), pltpu.VMEM((rows, 1), f32),
                        pltpu.VMEM((rows, r_kv), f32)],
    )
    return pl.pallas_call(
        functools.partial(_decode_kernel, ppc=ppc, n_chunks=n_chunks, n_pages=n_pages, steps=steps,
                          c=scale * LOG2E),
        grid_spec=gs,
        out_shape=jax.ShapeDtypeStruct((nb, rows, r_kv), bf16),
        compiler_params=_params("arbitrary", "arbitrary"),
    )(page_table.reshape(-1), qlat, qpe, cnew, knew, cache_c, cache_k)


def _rope_tables(pos, rope):
    half = rope // 2
    inv_freq = ROPE_THETA ** (-jnp.arange(half, dtype=f32) / half)
    ang = pos.astype(f32)[:, None] * inv_freq
    cos, sin = jnp.cos(ang), jnp.sin(ang)
    pad = jnp.zeros((pos.shape[0], LANE - rope), f32)
    return (jnp.concatenate([cos, cos, pad], axis=-1), jnp.concatenate([-sin, sin, pad], axis=-1))


def kernel(x_prompt, x_sample, cache_ckv, cache_kpe, state_conv, state_rglru, page_table, c_prompt, c_sample, mod_w, mod_b, mix_norm, ffn_norm, rg_w_in, rg_conv_w, rg_conv_b, rg_w_gates, rg_b_gates, rg_lambda, rg_w_out, kv_mod_w, kv_mod_b, kv_norm, kv_w_dkv, kv_latent_norm, kv_w_uk, kv_w_uv, mla_w_dq, mla_q_norm, mla_w_uq, mla_w_o, moe_w_group, moe_b_group, moe_w_expert, moe_b_expert, moe_w13, moe_w2, final_norm):
    bp, seq, d = x_prompt.shape
    bs, steps, _ = x_sample.shape
    d_rnn = rg_conv_w.shape[-1]
    cwid = rg_conv_w.shape[1]
    nb_rnn, rnn_blk = rg_w_gates.shape[1], rg_w_gates.shape[2]
    r_kv, n_heads, d_nope = kv_w_uk.shape
    d_v = kv_w_uv.shape[2]
    rope = cache_kpe.shape[-1]
    half = rope // 2
    page = cache_ckv.shape[1]
    past = page_table.shape[1] * page
    n_groups = moe_w_group.shape[-1]
    n_exp = moe_w_expert.shape[-1]
    scale = 1.0 / math.sqrt(d_nope + rope)
    assert d_nope == LANE and d_v == LANE and rope <= LANE and d_rnn % LANE == 0
    assert rg_w_in.shape[0] == 1 and mla_w_dq.shape[0] == 1 and mod_w.shape[0] == 2
    assert steps >= cwid - 1 and seq >= cwid - 1

    n_p, n_s = bp * seq, steps * bs
    gp = _Group(n_p, min(512, seq), seq, False)
    gs_ = _Group(n_s, min(512, n_s), 0, True)
    groups = (gp, gs_)

    xp = x_prompt.reshape(n_p, d)
    xs = jnp.swapaxes(x_sample, 0, 1).reshape(n_s, d)

    n_c = bp + bs
    n_c_pad = -(-n_c // SUBLANE) * SUBLANE
    c_all = jnp.concatenate([c_prompt, c_sample, jnp.zeros((n_c_pad - n_c, d), f32)], axis=0)
    mods = [_bias_matmul(c_all, mod_w, layer, mod_b[layer]) for layer in range(2)]
    kv_mods = _bias_matmul(c_all, kv_mod_w[None], 0, kv_mod_b)

    def mod_vec(m, idx, grp_i):
        rows = slice(0, bp) if grp_i == 0 else slice(bp, bp + bs)
        return m[rows, idx * d:(idx + 1) * d]

    w_in = rg_w_in[0].astype(bf16)
    w_out = rg_w_out[0].astype(bf16)
    starts, win = _gate_windows(d_rnn, rnn_blk)
    wg = _gate_weights(rg_w_gates[0], starts, win)
    bgr = rg_b_gates[0][:, :rnn_blk].reshape(1, d_rnn)
    bgi = rg_b_gates[0][:, rnn_blk:].reshape(1, d_rnn)
    conv_w = rg_conv_w[0]
    conv_b = rg_conv_b[0].reshape(1, d_rnn)
    lam = rg_lambda[0].reshape(1, d_rnn)

    w_dkv = jnp.concatenate([kv_w_dkv, jnp.zeros((d, LANE - rope), f32)], axis=1).astype(bf16)
    w_up = jnp.concatenate([kv_w_uk.reshape(r_kv, n_heads * d_nope), kv_w_uv.reshape(r_kv, n_heads * d_v)],
                           axis=1).astype(bf16)
    w_dq = mla_w_dq[0].astype(bf16)
    wq3 = mla_w_uq[0].reshape(-1, n_heads, d_nope + rope)
    wq = jnp.concatenate([wq3, jnp.zeros((wq3.shape[0], n_heads, LANE - rope), f32)], axis=-1)
    wq = wq.reshape(-1, n_heads * 2 * LANE).astype(bf16)
    w_o = mla_w_o[0].astype(bf16)
    w_uk_t = jnp.transpose(kv_w_uk, (1, 2, 0))
    w_uv_flat = kv_w_uv.reshape(r_kv, n_heads * d_v)

    cos_p, sin_p = _rope_tables(jnp.arange(seq, dtype=jnp.int32), rope)
    cos_s, sin_s = _rope_tables(past + jnp.arange(steps, dtype=jnp.int32), rope)
    tabs = ((cos_p, sin_p), (cos_s, sin_s))

    def moe(xs_in, layer):
        xn_l, rt_l = [], []
        n_pad = LANE - n_groups - n_exp
        wr = jnp.concatenate([moe_w_group[layer], moe_w_expert[layer], jnp.zeros((d, n_pad), f32)], axis=1)
        br = jnp.concatenate([moe_b_group[layer], moe_b_expert[layer], jnp.zeros((n_pad,), f32)]).reshape(1, LANE)
        for gi, grp in enumerate(groups):
            xn, rt = _router(xs_in[gi], ffn_norm[layer], mod_vec(mods[layer], 3, gi),
                             mod_vec(mods[layer], 4, gi), wr, br, grp, n_groups, n_exp)
            xn_l.append(xn)
            rt_l.append(rt)
        xn_all = jnp.concatenate(xn_l, axis=0)
        route = jnp.concatenate(rt_l, axis=0)
        expert_ids = route[:, :TOP_K].astype(jnp.int32)
        weights = route[:, TOP_K:2 * TOP_K]
        slot_tok, block_expert, n_used, dest = _dispatch(expert_ids, n_exp)
        xb = xn_all[slot_tok]
        yb = _experts(xb, block_expert, n_used, moe_w13, moe_w2, layer)
        ya = yb[dest[:, 0]]
        yc = yb[dest[:, 1]]
        out = []
        lo = 0
        for grp in groups:
            out.append((ya[lo:lo + grp.n], yc[lo:lo + grp.n], weights[lo:lo + grp.n]))
            lo += grp.n
        return out

    x_cur = [xp, xs]
    projs, h_last = [], []
    for gi, grp in enumerate(groups):
        proj = _normmod_matmul(x_cur[gi], mix_norm[0], mod_vec(mods[0], 0, gi), mod_vec(mods[0], 1, gi),
                               w_in, grp, tn_cap=896)
        projs.append(proj)
        if gi == 0:
            hg, hl = _rglru_prompt(proj, jnp.zeros((bp, 1, d_rnn), f32), jnp.zeros((bp, SUBLANE, d_rnn), f32),
                                   conv_w, conv_b, wg, bgr, bgi, lam, starts, win, bp, seq, min(256, seq))
            hl = hl.reshape(bp, d_rnn)
        else:
            cst = jnp.swapaxes(state_conv[0], 0, 1)
            hg, hl = _rglru_sample(proj.reshape(steps, bs, 2 * d_rnn), state_rglru[0], cst, conv_w, conv_b,
                                   wg, bgr, bgi, lam, starts, win)
            hg = hg.reshape(n_s, d_rnn)
        h_last.append(hl)
        x_cur[gi] = _matmul_res(hg, w_out, x_cur[gi], mod_vec(mods[0], 2, gi), grp)
    moe_out = moe(x_cur, 0)
    for gi, grp in enumerate(groups):
        ya, yc, wts = moe_out[gi]
        x_cur[gi] = _combine(x_cur[gi], ya, yc, wts, mod_vec(mods[0], 5, gi), grp)

    kv_out = []
    for gi, grp in enumerate(groups):
        kv_out.append(_kv_latent(x_cur[gi], kv_norm, kv_mods[:, :d][(slice(0, bp) if gi == 0 else slice(bp, bp + bs))],
                                 kv_mods[:, d:][(slice(0, bp) if gi == 0 else slice(bp, bp + bs))],
                                 w_dkv, kv_latent_norm, tabs[gi][0], tabs[gi][1], grp, half,
                                 w_up=w_up if gi == 0 else None))
    qs = [_q_proj(x_cur[gi], mix_norm[1], mod_vec(mods[1], 0, gi), mod_vec(mods[1], 1, gi), w_dq,
                  mla_q_norm[0], wq, tabs[gi][0], tabs[gi][1], grp, n_heads, half)
          for gi, grp in enumerate(groups)]

    ckv_p, kpe_p, kn_p, v_p, kpeb_p = kv_out[0]
    tq = min(ATTN_ROWS, seq)
    hgrp = 4 if n_heads % 4 == 0 else 1
    attn_p = _attn_prompt(qs[0], kn_p, v_p, kpeb_p, bp, seq, n_heads, scale, tq, hgrp)

    ckv_s, kpe_s = kv_out[1]
    rows = n_heads * steps
    qlat = _head_matmul(
        qs[1], pl.BlockSpec((n_s, LANE), lambda h: (0, 2 * h)),
        w_uk_t, pl.BlockSpec((None, d_nope, r_kv), lambda h: (h, 0, 0)),
        jax.ShapeDtypeStruct((n_heads, n_s, r_kv), bf16),
        pl.BlockSpec((None, n_s, r_kv), lambda h: (h, 0, 0)), n_heads)
    qlat = jnp.transpose(qlat.reshape(n_heads, steps, bs, r_kv), (2, 0, 1, 3)).reshape(bs, rows, r_kv)
    qpe = qs[1].reshape(steps, bs, n_heads, 2, LANE)[:, :, :, 1]
    qpe = jnp.transpose(qpe, (1, 2, 0, 3)).reshape(bs, rows, LANE)

    def new_keys(a):
        a = jnp.swapaxes(a.reshape(steps, bs, a.shape[-1]), 0, 1).astype(bf16)
        return jnp.concatenate([a, jnp.zeros((bs, LANE - steps, a.shape[-1]), bf16)], axis=1)

    ppc = _pick_tile(page_table.shape[1], DECODE_PAGES, 1)
    olat = _decode_attn(page_table, qlat, qpe, new_keys(ckv_s), new_keys(kpe_s), cache_ckv,
                        jnp.swapaxes(cache_kpe, 1, 2), scale, steps, ppc)
    olat = jnp.transpose(olat.reshape(bs, n_heads, steps, r_kv), (1, 2, 0, 3)).reshape(n_heads, n_s, r_kv)
    attn_s = _head_matmul(
        olat, pl.BlockSpec((None, n_s, r_kv), lambda h: (h, 0, 0)),
        w_uv_flat, pl.BlockSpec((r_kv, d_v), lambda h: (0, h)),
        jax.ShapeDtypeStruct((n_s, n_heads * d_v), bf16),
        pl.BlockSpec((n_s, d_v), lambda h: (0, h)), n_heads)

    attn = [attn_p, attn_s]
    for gi, grp in enumerate(groups):
        x_cur[gi] = _matmul_res(attn[gi], w_o, x_cur[gi], mod_vec(mods[1], 2, gi), grp)
    moe_out = moe(x_cur, 1)
    ys = []
    for gi, grp in enumerate(groups):
        ya, yc, wts = moe_out[gi]
        ys.append(_combine(x_cur[gi], ya, yc, wts, mod_vec(mods[1], 5, gi), grp, final_gain=final_norm))

    y_prompt = ys[0].reshape(bp, seq, d)
    y_sample = jnp.swapaxes(ys[1].reshape(steps, bs, d), 0, 1)
    conv_p = projs[0].reshape(bp, seq, 2 * d_rnn)[:, seq - (cwid - 1):, d_rnn:][None]
    conv_s = jnp.swapaxes(projs[1].reshape(steps, bs, 2 * d_rnn)[steps - (cwid - 1):, :, d_rnn:], 0, 1)[None]
    h_p = h_last[0][None]
    h_s = h_last[1][None]
    ckv_prompt = ckv_p.reshape(bp, seq, r_kv)
    kpe_prompt = kpe_p[:, :rope].reshape(bp, seq, rope)
    ckv_sample = jnp.swapaxes(ckv_s.reshape(steps, bs, r_kv), 0, 1)
    kpe_sample = jnp.swapaxes(kpe_s[:, :rope].reshape(steps, bs, rope), 0, 1)
    return (y_prompt, y_sample, conv_p, h_p, ckv_prompt, kpe_prompt, conv_s, h_s, ckv_sample, kpe_sample)
```

```python
import functools
import math
from typing import NamedTuple

import jax
import jax.numpy as jnp
from jax import lax
from jax.experimental import pallas as pl
from jax.experimental.pallas import tpu as pltpu

EPS = 1e-6
LRU_C = 8.0
ROPE_THETA = 10000.0
TOP_K = 2
LANE = 128
SUBLANE = 8
VMEM_LIMIT = 56 * 1024 * 1024
MOE_ROWS = 128
ATTN_ROWS = 512
SAMPLE_ROWS = 64
DECODE_PAGES = 32
LOG2E = 1.4426950408889634

f32 = jnp.float32
bf16 = jnp.bfloat16


def _params(*sem):
    return pltpu.CompilerParams(dimension_semantics=sem, vmem_limit_bytes=VMEM_LIMIT)


def _pick_tile(n, cap, mult=LANE):
    if n <= cap:
        return n
    best = None
    for t in range(mult, cap + 1, mult):
        if n % t == 0:
            best = t
    assert best is not None, (n, cap, mult)
    return best


class _Group(NamedTuple):
    n: int
    tm: int
    seq: int
    per_token: bool


def _mod_operand(vec, grp):
    d = vec.shape[-1]
    if grp.per_token:
        arr = jnp.tile(vec, (grp.n // vec.shape[0], 1))
        return arr, pl.BlockSpec((grp.tm, d), lambda i, *_: (i, 0))
    per_seq = grp.seq // grp.tm
    return vec[:, None, :], pl.BlockSpec((None, 1, d), lambda i, *_: (i // per_seq, 0, 0))


def _pos_operand(tab, grp):
    if grp.per_token:
        arr = jnp.repeat(tab, grp.n // tab.shape[0], axis=0)
        return arr, pl.BlockSpec((grp.tm, LANE), lambda i, *_: (i, 0))
    per_seq = grp.seq // grp.tm
    return tab, pl.BlockSpec((grp.tm, LANE), lambda i, *_: (i % per_seq, 0))


def _row_spec(grp, width):
    return pl.BlockSpec((grp.tm, width), lambda i, *_: (i, 0))


def _full_spec(shape):
    nd = len(shape)
    return pl.BlockSpec(shape, lambda *_: (0,) * nd)


def _rms(x, gain):
    return x * lax.rsqrt(jnp.mean(x * x, axis=-1, keepdims=True) + EPS) * gain


def _normmod(x, gain, shift, scale):
    return _rms(x, gain) * (1.0 + scale) + shift


def _rope_tile(t, cos, sin, half):
    lane = lax.broadcasted_iota(jnp.int32, t.shape, 1)
    rot = jnp.where(lane < half, pltpu.roll(t, LANE - half, 1), pltpu.roll(t, half, 1))
    return t * cos + rot * sin


def _sigmoid(x):
    return 0.5 * (jnp.tanh(0.5 * x) + 1.0)


def _gelu(x):
    return 0.5 * x * (1.0 + jnp.tanh(math.sqrt(2.0 / math.pi) * (x + 0.044715 * (x * x * x))))


def _softplus(z):
    return jnp.maximum(z, 0.0) + jnp.log1p(jnp.exp(-jnp.abs(z)))


def _bias_matmul_kernel(a_ref, w_ref, b_ref, o_ref):
    o_ref[...] = jnp.dot(a_ref[...].astype(bf16), w_ref[...].astype(bf16),
                         preferred_element_type=f32) + b_ref[...]


def _bias_matmul(a, w3, layer, b):
    m, k = a.shape
    n = w3.shape[-1]
    tn = _pick_tile(n, 1024)
    return pl.pallas_call(
        _bias_matmul_kernel,
        grid=(n // tn,),
        in_specs=[_full_spec((m, k)),
                  pl.BlockSpec((None, k, tn), lambda j: (layer, 0, j)),
                  pl.BlockSpec((1, tn), lambda j: (0, j))],
        out_specs=pl.BlockSpec((m, tn), lambda j: (0, j)),
        out_shape=jax.ShapeDtypeStruct((m, n), f32),
        compiler_params=_params("arbitrary"),
    )(a, w3, b.reshape(1, n))


def _normmod_matmul_kernel(x_ref, gain_ref, shift_ref, scale_ref, w_ref, o_ref, xn_ref):
    @pl.when(pl.program_id(1) == 0)
    def _():
        xn_ref[...] = _normmod(x_ref[...], gain_ref[...], shift_ref[...], scale_ref[...]).astype(bf16)

    o_ref[...] = jnp.dot(xn_ref[...], w_ref[...], preferred_element_type=f32).astype(o_ref.dtype)


def _normmod_matmul(x, gain, shift, scale, w, grp, tn_cap=1024, out_dtype=f32):
    d = x.shape[1]
    n = w.shape[1]
    tn = _pick_tile(n, tn_cap)
    sh, sh_spec = _mod_operand(shift, grp)
    sc, sc_spec = _mod_operand(scale, grp)
    return pl.pallas_call(
        _normmod_matmul_kernel,
        grid=(grp.n // grp.tm, n // tn),
        in_specs=[_row_spec(grp, d), _full_spec((1, d)), sh_spec, sc_spec,
                  pl.BlockSpec((d, tn), lambda i, j: (0, j))],
        out_specs=pl.BlockSpec((grp.tm, tn), lambda i, j: (i, j)),
        out_shape=jax.ShapeDtypeStruct((grp.n, n), out_dtype),
        scratch_shapes=[pltpu.VMEM((grp.tm, d), bf16)],
        compiler_params=_params("arbitrary", "arbitrary"),
    )(x, gain.reshape(1, d), sh, sc, w)


def _matmul_res_kernel(a_ref, w_ref, res_ref, gate_ref, o_ref):
    y = jnp.dot(a_ref[...].astype(bf16), w_ref[...], preferred_element_type=f32)
    o_ref[...] = res_ref[...] + gate_ref[...] * y


def _matmul_res(a, w, res, gate, grp, tn_cap=1024):
    k = a.shape[1]
    n = w.shape[1]
    tn = _pick_tile(n, tn_cap)
    if grp.per_token:
        g_arr = jnp.tile(gate, (grp.n // gate.shape[0], 1))
        g_spec = pl.BlockSpec((grp.tm, tn), lambda j, i: (i, j))
    else:
        per_seq = grp.seq // grp.tm
        g_arr = gate[:, None, :]
        g_spec = pl.BlockSpec((None, 1, tn), lambda j, i: (i // per_seq, 0, j))
    return pl.pallas_call(
        _matmul_res_kernel,
        grid=(n // tn, grp.n // grp.tm),
        in_specs=[pl.BlockSpec((grp.tm, k), lambda j, i: (i, 0)),
                  pl.BlockSpec((k, tn), lambda j, i: (0, j)),
                  pl.BlockSpec((grp.tm, tn), lambda j, i: (i, j)),
                  g_spec],
        out_specs=pl.BlockSpec((grp.tm, tn), lambda j, i: (i, j)),
        out_shape=jax.ShapeDtypeStruct((grp.n, n), f32),
        compiler_params=_params("arbitrary", "arbitrary"),
    )(a, w, res, g_arr)


def _gate_windows(d_rnn, blk):
    nt = d_rnn // LANE
    raw, need = [], 0
    for j in range(nt):
        n0 = (LANE * j) // blk
        n1 = (LANE * j + LANE - 1) // blk
        s = (blk * n0) // LANE * LANE
        raw.append(s)
        need = max(need, blk * (n1 + 1) - s)
    win = min(d_rnn, -(-need // LANE) * LANE)
    return [min(s, d_rnn - win) for s in raw], win


def _gate_weights(w_gates, starts, win):
    nb, blk, _ = w_gates.shape
    d = nb * blk
    eye = jnp.eye(nb, dtype=w_gates.dtype)

    def dense(w):
        return (w[:, :, None, :] * eye[:, None, :, None]).reshape(d, d).astype(bf16)

    wr = dense(w_gates[:, :, :blk])
    wi = dense(w_gates[:, :, blk:])
    tiles = [jnp.concatenate([wr[s:s + win, j * LANE:(j + 1) * LANE], wi[s:s + win, j * LANE:(j + 1) * LANE]], axis=1)
             for j, s in enumerate(starts)]
    return jnp.stack(tiles)


def _lru_inputs(g, b_r, b_i, neg_c_sp, u):
    r = _sigmoid(g[:, :LANE] + b_r)
    i = _sigmoid(g[:, LANE:] + b_i)
    log_a = neg_c_sp * r
    a = jnp.exp(log_a)
    x = jnp.sqrt(-jnp.tanh(log_a) * (a * a + 1.0)) * (i * u)
    return a, x


def _rglru_prompt_kernel(proj_ref, h0_ref, cinit_ref, cw_ref, cb_ref, wg_ref, bgr_ref, bgi_ref, lam_ref,
                         hg_ref, hlast_ref, ubuf, ucf, ucb, a_scr, x_scr, hcar,
                         *, starts, win, tc, d_rnn, cwid):
    t = pl.program_id(1)
    hist = SUBLANE

    @pl.when(t == 0)
    def _():
        ubuf[0:hist, :] = cinit_ref[...]
        hcar[...] = h0_ref[...]

    ubuf[hist:hist + tc, :] = proj_ref[:, d_rnn:]
    uc = cb_ref[...]
    for k in range(cwid):
        off = hist - (cwid - 1) + k
        uc = uc + cw_ref[k:k + 1, :] * ubuf[off:off + tc, :]
    ucf[...] = uc
    ucb[...] = uc.astype(bf16)
    ubuf[0:hist, :] = ubuf[tc:tc + hist, :]

    neg_c_sp = -LRU_C * _softplus(-lam_ref[...])
    seg = tc // SUBLANE
    for j in range(d_rnn // LANE):
        cs = slice(j * LANE, (j + 1) * LANE)
        g = jnp.dot(ucb[:, starts[j]:starts[j] + win], wg_ref[j], preferred_element_type=f32)
        a, x = _lru_inputs(g, bgr_ref[:, cs], bgi_ref[:, cs], neg_c_sp[:, cs], ucf[:, cs])
        a_scr[j] = a
        x_scr[j] = x
        h = jnp.zeros((SUBLANE, LANE), f32)
        p = jnp.ones((SUBLANE, LANE), f32)
        for k in range(seg):
            ak = a_scr[j, pl.ds(k, SUBLANE, stride=seg), :]
            xk = x_scr[j, pl.ds(k, SUBLANE, stride=seg), :]
            h = ak * h + xk
            p = p * ak
            x_scr[j, pl.ds(k, SUBLANE, stride=seg), :] = h
            a_scr[j, pl.ds(k, SUBLANE, stride=seg), :] = p
        c = hcar[:, cs]
        outs = []
        for r in range(SUBLANE):
            rows = slice(r * seg, (r + 1) * seg)
            outs.append(x_scr[j, rows, :] + a_scr[j, rows, :] * c)
            c = p[r:r + 1, :] * c + h[r:r + 1, :]
        hcar[:, cs] = c
        hs = jnp.concatenate(outs, axis=0)
        hg_ref[:, cs] = (hs * _gelu(proj_ref[:, cs])).astype(hg_ref.dtype)
    hlast_ref[...] = hcar[...]


def _rglru_prompt(proj, h0, conv_init, conv_w, conv_b, wg, bgr, bgi, lam, starts, win, nb_seq, seq, tc):
    d_rnn = proj.shape[1] // 2
    cwid = conv_w.shape[0]
    nt = d_rnn // LANE
    per_seq = seq // tc
    kern = functools.partial(_rglru_prompt_kernel, starts=tuple(starts), win=win, tc=tc, d_rnn=d_rnn, cwid=cwid)
    return pl.pallas_call(
        kern,
        grid=(nb_seq, per_seq),
        in_specs=[pl.BlockSpec((tc, 2 * d_rnn), lambda b, t: (b * per_seq + t, 0)),
                  pl.BlockSpec((None, 1, d_rnn), lambda b, t: (b, 0, 0)),
                  pl.BlockSpec((None, SUBLANE, d_rnn), lambda b, t: (b, 0, 0)),
                  _full_spec((cwid, d_rnn)), _full_spec((1, d_rnn)),
                  _full_spec(wg.shape), _full_spec((1, d_rnn)), _full_spec((1, d_rnn)), _full_spec((1, d_rnn))],
        out_specs=[pl.BlockSpec((tc, d_rnn), lambda b, t: (b * per_seq + t, 0)),
                   pl.BlockSpec((None, 1, d_rnn), lambda b, t: (b, 0, 0))],
        out_shape=[jax.ShapeDtypeStruct((nb_seq * seq, d_rnn), bf16),
                   jax.ShapeDtypeStruct((nb_seq, 1, d_rnn), f32)],
        scratch_shapes=[pltpu.VMEM((tc + SUBLANE, d_rnn), f32),
                        pltpu.VMEM((tc, d_rnn), f32),
                        pltpu.VMEM((tc, d_rnn), bf16),
                        pltpu.VMEM((nt, tc, LANE), f32),
                        pltpu.VMEM((nt, tc, LANE), f32),
                        pltpu.VMEM((1, d_rnn), f32)],
        compiler_params=_params("arbitrary", "arbitrary"),
    )(proj, h0, conv_init, conv_w, conv_b, wg, bgr, bgi, lam)


def _rglru_sample_kernel(proj_ref, h0_ref, cst_ref, cw_ref, cb_ref, wg_ref, bgr_ref, bgi_ref, lam_ref,
                         hg_ref, hlast_ref, ucf, ucb, *, starts, win, steps, tb, d_rnn, cwid):
    def up(tp):
        if tp < cwid - 1:
            return cst_ref[tp]
        return proj_ref[tp - cwid + 1, :, d_rnn:]

    for t in range(steps):
        uc = cb_ref[...]
        for k in range(cwid):
            uc = uc + cw_ref[k:k + 1, :] * up(t + k)
        ucf[t * tb:(t + 1) * tb, :] = uc
        ucb[t * tb:(t + 1) * tb, :] = uc.astype(bf16)

    neg_c_sp = -LRU_C * _softplus(-lam_ref[...])
    for j in range(d_rnn // LANE):
        cs = slice(j * LANE, (j + 1) * LANE)
        g = jnp.dot(ucb[:, starts[j]:starts[j] + win], wg_ref[j], preferred_element_type=f32)
        h = h0_ref[:, cs]
        for t in range(steps):
            rows = slice(t * tb, (t + 1) * tb)
            a, x = _lru_inputs(g[rows], bgr_ref[:, cs], bgi_ref[:, cs], neg_c_sp[:, cs], ucf[rows, cs])
            h = a * h + x
            hg_ref[t, :, cs] = (h * _gelu(proj_ref[t, :, cs])).astype(hg_ref.dtype)
        hlast_ref[:, cs] = h


def _rglru_sample(proj, h0, conv_state, conv_w, conv_b, wg, bgr, bgi, lam, starts, win):
    steps, nb, _ = proj.shape
    d_rnn = proj.shape[2] // 2
    cwid = conv_w.shape[0]
    tb = min(SAMPLE_ROWS, nb)
    kern = functools.partial(_rglru_sample_kernel, starts=tuple(starts), win=win, steps=steps, tb=tb,
                             d_rnn=d_rnn, cwid=cwid)
    return pl.pallas_call(
        kern,
        grid=(nb // tb,),
        in_specs=[pl.BlockSpec((steps, tb, 2 * d_rnn), lambda i: (0, i, 0)),
                  pl.BlockSpec((tb, d_rnn), lambda i: (i, 0)),
                  pl.BlockSpec((cwid - 1, tb, d_rnn), lambda i: (0, i, 0)),
                  _full_spec((cwid, d_rnn)), _full_spec((1, d_rnn)),
                  _full_spec(wg.shape), _full_spec((1, d_rnn)), _full_spec((1, d_rnn)), _full_spec((1, d_rnn))],
        out_specs=[pl.BlockSpec((steps, tb, d_rnn), lambda i: (0, i, 0)),
                   pl.BlockSpec((tb, d_rnn), lambda i: (i, 0))],
        out_shape=[jax.ShapeDtypeStruct((steps, nb, d_rnn), bf16),
                   jax.ShapeDtypeStruct((nb, d_rnn), f32)],
        scratch_shapes=[pltpu.VMEM((steps * tb, d_rnn), f32), pltpu.VMEM((steps * tb, d_rnn), bf16)],
        compiler_params=_params("arbitrary"),
    )(proj, h0, conv_state, conv_w, conv_b, wg, bgr, bgi, lam)


def _first_max(vals, ids):
    m = jnp.max(vals, axis=-1, keepdims=True)
    idx = jnp.min(jnp.where(vals == m, ids, jnp.int32(2 ** 30)), axis=-1, keepdims=True)
    return m, idx


def _router_kernel(x_ref, gain_ref, shift_ref, scale_ref, wh_ref, wl_ref, b_ref, xn_ref, rt_ref,
                   *, n_groups, n_exp):
    xn = _normmod(x_ref[...], gain_ref[...], shift_ref[...], scale_ref[...])
    xn_ref[...] = xn
    xh = xn.astype(bf16)
    xl = (xn - xh.astype(f32)).astype(bf16)
    lg = (jnp.dot(xh, wh_ref[...], preferred_element_type=f32)
          + jnp.dot(xl, wh_ref[...], preferred_element_type=f32)
          + jnp.dot(xh, wl_ref[...], preferred_element_type=f32)
          + jnp.dot(xl, wl_ref[...], preferred_element_type=f32)) + b_ref[...]
    per = n_exp // n_groups
    lane = lax.broadcasted_iota(jnp.int32, lg.shape, 1)
    neg = -jnp.inf
    gl = jnp.where(lane < n_groups, lg, neg)
    gmax, g_idx = _first_max(gl, lane)
    g_w = 1.0 / jnp.sum(jnp.exp(gl - gmax), axis=-1, keepdims=True)
    e_id = lane - n_groups
    lo = g_idx * per
    el = jnp.where(e_id >= lo, jnp.where(e_id < lo + per, lg, neg), neg)
    l1, i1 = _first_max(el, e_id)
    el2 = jnp.where(e_id == i1, neg, el)
    l2, i2 = _first_max(el2, e_id)
    e = jnp.exp(l2 - l1)
    w1 = g_w / (1.0 + e)
    w2 = w1 * e
    rt_ref[...] = jnp.where(lane == 0, i1.astype(f32),
                            jnp.where(lane == 1, i2.astype(f32),
                                      jnp.where(lane == 2, w1, jnp.where(lane == 3, w2, 0.0))))


def _router(x, gain, shift, scale, wr, br, grp, n_groups, n_exp):
    d = x.shape[1]
    nr = wr.shape[1]
    wh = wr.astype(bf16)
    wl = (wr - wh.astype(f32)).astype(bf16)
    sh, sh_spec = _mod_operand(shift, grp)
    sc, sc_spec = _mod_operand(scale, grp)
    return pl.pallas_call(
        functools.partial(_router_kernel, n_groups=n_groups, n_exp=n_exp),
        grid=(grp.n // grp.tm,),
        in_specs=[_row_spec(grp, d), _full_spec((1, d)), sh_spec, sc_spec,
                  _full_spec((d, nr)), _full_spec((d, nr)), _full_spec((1, nr))],
        out_specs=[_row_spec(grp, d), _row_spec(grp, nr)],
        out_shape=[jax.ShapeDtypeStruct((grp.n, d), f32), jax.ShapeDtypeStruct((grp.n, nr), f32)],
        compiler_params=_params("arbitrary"),
    )(x, gain.reshape(1, d), sh, sc, wh, wl, br)


def _experts_kernel(be_ref, nu_ref, st_ref, nx_ref, xn_hbm, w13_hbm, w2_hbm, o_ref,
                    xbuf, sem, w13f, w2f, wsem, w13b, w2b, *, d_exp, layer):
    i = pl.program_id(0)
    n_used = nu_ref[0]
    slot = i % 2
    e = be_ref[i]
    prev = be_ref[jnp.maximum(i - 1, 0)]

    def row_copy(tok, sl, r):
        return pltpu.make_async_copy(xn_hbm.at[pl.ds(tok, 1), :], xbuf.at[sl, pl.ds(r, 1), :], sem.at[sl])

    def gather(blk, sl):
        for r in range(MOE_ROWS):
            row_copy(st_ref[blk * MOE_ROWS + r], sl, r).start()

    def weight_copies(ex):
        return (pltpu.make_async_copy(w13_hbm.at[layer, ex], w13f, wsem.at[0]),
                pltpu.make_async_copy(w2_hbm.at[layer, ex], w2f, wsem.at[1]))

    @pl.when((i == 0) & (n_used > 0))
    def _():
        gather(0, 0)
        for cp in weight_copies(e):
            cp.start()

    @pl.when(i + 1 < n_used)
    def _():
        gather(i + 1, 1 - slot)

    @pl.when((i < n_used) & ((i == 0) | (e != prev)))
    def _():
        for cp in weight_copies(e):
            cp.wait()
        w13b[...] = w13f[...].astype(bf16)
        w2b[...] = w2f[...].astype(bf16)

        @pl.when(nx_ref[i] >= 0)
        def _():
            for cp in weight_copies(nx_ref[i]):
                cp.start()

    @pl.when(i < n_used)
    def _():
        for r in range(MOE_ROWS):
            row_copy(0, slot, r).wait()
        gu = jnp.dot(xbuf[slot].astype(bf16), w13b[...], preferred_element_type=f32)
        g = gu[:, :d_exp]
        act = (g * _sigmoid(g)) * gu[:, d_exp:]
        o_ref[...] = jnp.dot(act.astype(bf16), w2b[...], preferred_element_type=f32)

    @pl.when(i >= n_used)
    def _():
        o_ref[...] = jnp.zeros_like(o_ref)


def _experts(xn, slot_tok, block_expert, n_used, next_expert, w13, w2, layer):
    d = xn.shape[1]
    n_blocks = slot_tok.shape[0] // MOE_ROWS
    d_exp = w2.shape[2]
    gs = pltpu.PrefetchScalarGridSpec(
        num_scalar_prefetch=4,
        grid=(n_blocks,),
        in_specs=[pl.BlockSpec(memory_space=pl.ANY), pl.BlockSpec(memory_space=pl.ANY),
                  pl.BlockSpec(memory_space=pl.ANY)],
        out_specs=pl.BlockSpec((MOE_ROWS, d), lambda i, *_: (i, 0)),
        scratch_shapes=[pltpu.VMEM((2, MOE_ROWS, d), f32), pltpu.SemaphoreType.DMA((2,)),
                        pltpu.VMEM((d, 2 * d_exp), f32), pltpu.VMEM((d_exp, d), f32),
                        pltpu.SemaphoreType.DMA((2,)),
                        pltpu.VMEM((d, 2 * d_exp), bf16), pltpu.VMEM((d_exp, d), bf16)],
    )
    return pl.pallas_call(
        functools.partial(_experts_kernel, d_exp=d_exp, layer=layer),
        grid_spec=gs,
        out_shape=jax.ShapeDtypeStruct((n_blocks * MOE_ROWS, d), f32),
        compiler_params=_params("arbitrary"),
    )(block_expert, n_used, slot_tok, next_expert, xn, w13, w2)


def _combine_kernel(x_ref, ya_ref, yb_ref, wt_ref, gate_ref, *rest, final):
    if final:
        fg_ref, o_ref = rest
    else:
        (o_ref,) = rest
    wt = wt_ref[...]
    y = x_ref[...] + gate_ref[...] * (wt[:, 0:1] * ya_ref[...] + wt[:, 1:2] * yb_ref[...])
    if final:
        y = _rms(y, fg_ref[...])
    o_ref[...] = y


def _combine(x, ya, yb, wts, gate, grp, final_gain=None):
    d = x.shape[1]
    g_arr, g_spec = _mod_operand(gate, grp)
    final = final_gain is not None
    args = [x, ya, yb, wts, g_arr]
    specs = [_row_spec(grp, d), _row_spec(grp, d), _row_spec(grp, d), _row_spec(grp, TOP_K), g_spec]
    if final:
        args.append(final_gain.reshape(1, d))
        specs.append(_full_spec((1, d)))
    return pl.pallas_call(
        functools.partial(_combine_kernel, final=final),
        grid=(grp.n // grp.tm,),
        in_specs=specs,
        out_specs=_row_spec(grp, d),
        out_shape=jax.ShapeDtypeStruct((grp.n, d), f32),
        compiler_params=_params("arbitrary"),
    )(*args)


def _dispatch(expert_ids, n_exp):
    n = expert_ids.shape[0]
    na = n * TOP_K
    flat_e = expert_ids.reshape(na)
    onehot = (flat_e[:, None] == jnp.arange(n_exp, dtype=jnp.int32)[None, :]).astype(jnp.int32)
    incl = jnp.cumsum(onehot, axis=0)
    counts = incl[-1]
    padded = (counts + MOE_ROWS - 1) // MOE_ROWS * MOE_ROWS
    pad_end = jnp.cumsum(padded)
    pad_start = pad_end - padded
    dest = jnp.sum(onehot * (pad_start[None, :] + incl - 1), axis=1)
    n_blocks = (na + MOE_ROWS - 1) // MOE_ROWS + n_exp
    tok = jnp.arange(na, dtype=jnp.int32) // TOP_K
    slot_tok = jnp.zeros((n_blocks * MOE_ROWS,), jnp.int32).at[dest].set(tok)
    block_start = jnp.arange(n_blocks, dtype=jnp.int32) * MOE_ROWS
    block_expert = jnp.minimum(jnp.sum((pad_end[None, :] <= block_start[:, None]).astype(jnp.int32), axis=1),
                               n_exp - 1)
    n_used = pad_end[-1:] // MOE_ROWS
    bidx = jnp.arange(n_blocks, dtype=jnp.int32)
    is_first = (bidx == 0) | (block_expert != jnp.roll(block_expert, 1))
    pos = jnp.where(is_first & (bidx < n_used[0]), bidx, n_blocks)
    nxt = jnp.min(jnp.where(bidx[None, :] > bidx[:, None], pos[None, :], n_blocks), axis=1)
    next_expert = jnp.where(nxt < n_blocks, block_expert[jnp.minimum(nxt, n_blocks - 1)], -1)
    return slot_tok, block_expert, n_used, next_expert, dest.reshape(n, TOP_K)


def _kv_kernel(x_ref, gain_ref, shift_ref, scale_ref, w_ref, lg_ref, cos_ref, sin_ref, *rest,
               r_kv, half, with_up):
    if with_up:
        wup_ref, ckv_ref, kpe_ref, kn_ref, v_ref, kpeb_ref = rest
    else:
        ckv_ref, kpe_ref = rest
    a = _normmod(x_ref[...], gain_ref[...], shift_ref[...], scale_ref[...]).astype(bf16)
    kv = jnp.dot(a, w_ref[...], preferred_element_type=f32)
    c = _rms(kv[:, :r_kv], lg_ref[...])
    ckv_ref[...] = c
    kp = _rope_tile(kv[:, r_kv:], cos_ref[...], sin_ref[...], half)
    kpe_ref[...] = kp
    if with_up:
        up = jnp.dot(c.astype(bf16), wup_ref[...], preferred_element_type=f32)
        hw = up.shape[1] // 2
        kn_ref[...] = up[:, :hw].astype(bf16)
        v_ref[...] = up[:, hw:].astype(bf16)
        kpeb_ref[...] = kp.astype(bf16)


def _kv_latent(x, gain, shift, scale, w_pad, latent_gain, cos, sin, grp, half, w_up=None):
    d = x.shape[1]
    r_kv = latent_gain.shape[0]
    with_up = w_up is not None
    sh, sh_spec = _mod_operand(shift, grp)
    sc, sc_spec = _mod_operand(scale, grp)
    cs, cs_spec = _pos_operand(cos, grp)
    sn, sn_spec = _pos_operand(sin, grp)
    args = [x, gain.reshape(1, d), sh, sc, w_pad, latent_gain.reshape(1, r_kv), cs, sn]
    specs = [_row_spec(grp, d), _full_spec((1, d)), sh_spec, sc_spec, _full_spec(w_pad.shape),
             _full_spec((1, r_kv)), cs_spec, sn_spec]
    out_specs = [_row_spec(grp, r_kv), _row_spec(grp, LANE)]
    out_shape = [jax.ShapeDtypeStruct((grp.n, r_kv), f32), jax.ShapeDtypeStruct((grp.n, LANE), f32)]
    if with_up:
        hw = w_up.shape[1] // 2
        args.append(w_up)
        specs.append(_full_spec(w_up.shape))
        out_specs += [_row_spec(grp, hw), _row_spec(grp, hw), _row_spec(grp, LANE)]
        out_shape += [jax.ShapeDtypeStruct((grp.n, hw), bf16), jax.ShapeDtypeStruct((grp.n, hw), bf16),
                      jax.ShapeDtypeStruct((grp.n, LANE), bf16)]
    return pl.pallas_call(
        functools.partial(_kv_kernel, r_kv=r_kv, half=half, with_up=with_up),
        grid=(grp.n // grp.tm,),
        in_specs=specs, out_specs=out_specs, out_shape=out_shape,
        compiler_params=_params("arbitrary"),
    )(*args)


def _q_kernel(x_ref, gain_ref, shift_ref, scale_ref, wdq_ref, qg_ref, wq_ref, cos_ref, sin_ref, q_ref,
              *, n_heads, half):
    a = _normmod(x_ref[...], gain_ref[...], shift_ref[...], scale_ref[...]).astype(bf16)
    ql = jnp.dot(a, wdq_ref[...], preferred_element_type=f32)
    qn = _rms(ql, qg_ref[...]).astype(bf16)
    cos = cos_ref[...]
    sin = sin_ref[...]
    for h in range(n_heads):
        q = jnp.dot(qn, wq_ref[:, 2 * LANE * h:2 * LANE * (h + 1)], preferred_element_type=f32)
        q_ref[:, 2 * LANE * h:2 * LANE * h + LANE] = q[:, :LANE].astype(bf16)
        q_ref[:, 2 * LANE * h + LANE:2 * LANE * (h + 1)] = _rope_tile(q[:, LANE:], cos, sin, half).astype(bf16)


def _q_proj(x, gain, shift, scale, w_dq, q_gain, wq, cos, sin, grp, n_heads, half):
    d = x.shape[1]
    rq = w_dq.shape[1]
    sh, sh_spec = _mod_operand(shift, grp)
    sc, sc_spec = _mod_operand(scale, grp)
    cs, cs_spec = _pos_operand(cos, grp)
    sn, sn_spec = _pos_operand(sin, grp)
    return pl.pallas_call(
        functools.partial(_q_kernel, n_heads=n_heads, half=half),
        grid=(grp.n // grp.tm,),
        in_specs=[_row_spec(grp, d), _full_spec((1, d)), sh_spec, sc_spec, _full_spec(w_dq.shape),
                  _full_spec((1, rq)), _full_spec(wq.shape), cs_spec, sn_spec],
        out_specs=_row_spec(grp, wq.shape[1]),
        out_shape=jax.ShapeDtypeStruct((grp.n, wq.shape[1]), bf16),
        compiler_params=_params("arbitrary"),
    )(x, gain.reshape(1, d), sh, sc, w_dq, q_gain.reshape(1, rq), wq, cs, sn)


def _attn_kernel(q_ref, kn_ref, v_ref, kpe_ref, o_ref, m_scr, acc_scr, *, hg, tq, c):
    qi = pl.program_id(2)
    m_scr[...] = jnp.full_like(m_scr, -jnp.inf)
    acc_scr[...] = jnp.zeros_like(acc_scr)
    ones = jnp.ones((tq, LANE), bf16)
    n_lt = tq // LANE

    def block(ks, masked):
        kpe = kpe_ref[pl.ds(ks, tq), :]
        for h in range(hg):
            hs = slice(h * LANE, (h + 1) * LANE)
            k = jnp.concatenate([kn_ref[pl.ds(ks, tq), hs], kpe], axis=-1)
            s = lax.dot_general(q_ref[:, 2 * LANE * h:2 * LANE * (h + 1)], k, (((1,), (1,)), ((), ())),
                                preferred_element_type=f32)
            if masked:
                row = lax.broadcasted_iota(jnp.int32, (tq, tq), 0)
                col = lax.broadcasted_iota(jnp.int32, (tq, tq), 1)
                s = jnp.where(col <= row, s, -jnp.inf)
            tiles = [s[:, t * LANE:(t + 1) * LANE] for t in range(n_lt)]
            mx = tiles[0]
            for t in tiles[1:]:
                mx = jnp.maximum(mx, t)
            m = m_scr[h]
            m_new = jnp.maximum(m, jnp.max(mx, axis=-1, keepdims=True))
            p = jnp.concatenate([jnp.exp2((t - m_new) * c) for t in tiles], axis=-1).astype(bf16)
            corr = jnp.exp2((m - m_new) * c)
            v_aug = jnp.concatenate([v_ref[pl.ds(ks, tq), hs], ones], axis=-1)
            pv = jnp.dot(p, v_aug, preferred_element_type=f32)
            acc_scr[h, :, :LANE] = acc_scr[h, :, :LANE] * corr + pv[:, :LANE]
            acc_scr[h, :, LANE:] = acc_scr[h, :, LANE:] * corr + pv[:, LANE:]
            m_scr[h] = m_new

    def body(kb, carry):
        block(pl.multiple_of(kb * tq, tq), False)
        return carry

    lax.fori_loop(0, qi, body, 0)
    block(pl.multiple_of(qi * tq, tq), True)
    for h in range(hg):
        o_ref[:, h * LANE:(h + 1) * LANE] = (acc_scr[h, :, :LANE] * (1.0 / acc_scr[h, :, LANE:])).astype(o_ref.dtype)


def _attn_prompt(q, kn, v, kpe, nb_seq, seq, n_heads, scale, tq, hg):
    nq = seq // tq
    return pl.pallas_call(
        functools.partial(_attn_kernel, hg=hg, tq=tq, c=scale * LOG2E),
        grid=(nb_seq, n_heads // hg, nq),
        in_specs=[pl.BlockSpec((tq, hg * 2 * LANE), lambda b, g, i: (b * nq + i, g)),
                  pl.BlockSpec((seq, hg * LANE), lambda b, g, i: (b, g)),
                  pl.BlockSpec((seq, hg * LANE), lambda b, g, i: (b, g)),
                  pl.BlockSpec((seq, LANE), lambda b, g, i: (b, 0))],
        out_specs=pl.BlockSpec((tq, hg * LANE), lambda b, g, i: (b * nq + i, g)),
        out_shape=jax.ShapeDtypeStruct((nb_seq * seq, n_heads * LANE), bf16),
        scratch_shapes=[pltpu.VMEM((hg, tq, LANE), f32), pltpu.VMEM((hg, tq, 2 * LANE), f32)],
        compiler_params=_params("arbitrary", "arbitrary", "arbitrary"),
    )(q, kn, v, kpe)


def _head_matmul_kernel(a_ref, w_ref, o_ref):
    o_ref[...] = jnp.dot(a_ref[...].astype(bf16), w_ref[...].astype(bf16),
                         preferred_element_type=f32).astype(o_ref.dtype)


def _head_matmul(a, a_spec, w, w_spec, out_shape, out_spec, n_heads):
    return pl.pallas_call(
        _head_matmul_kernel,
        grid=(n_heads,),
        in_specs=[a_spec, w_spec],
        out_specs=out_spec,
        out_shape=out_shape,
        compiler_params=_params("arbitrary"),
    )(a, w)


def _decode_kernel(pt_ref, qlat_ref, qpe_ref, cnew_ref, knew_ref, cache_c, cache_k, o_ref,
                   cbuf, kbuf, sem, m_scr, l_scr, acc_scr, *, ppc, n_chunks, n_pages, steps, c):
    b = pl.program_id(0)
    ch = pl.program_id(1)
    step = b * n_chunks + ch
    slot = step % 2
    total = pl.num_programs(0) * n_chunks

    page = cbuf.shape[2]

    def copies(bb, cc, sl, p):
        phys = pt_ref[bb * n_pages + cc * ppc + p]
        return (pltpu.make_async_copy(cache_c.at[phys], cbuf.at[sl, p], sem.at[0, sl]),
                pltpu.make_async_copy(cache_k.at[phys], kbuf.at[sl, :, pl.ds(p * page, page)], sem.at[1, sl]))

    def issue(bb, cc, sl):
        for p in range(ppc):
            for cp in copies(bb, cc, sl, p):
                cp.start()

    @pl.when(step == 0)
    def _():
        issue(0, 0, 0)

    @pl.when(step + 1 < total)
    def _():
        nxt = step + 1
        issue(nxt // n_chunks, nxt % n_chunks, 1 - slot)

    for p in range(ppc):
        for cp in copies(b, ch, slot, p):
            cp.wait()

    @pl.when(ch == 0)
    def _():
        m_scr[...] = jnp.full_like(m_scr, -jnp.inf)
        l_scr[...] = jnp.zeros_like(l_scr)
        acc_scr[...] = jnp.zeros_like(acc_scr)

    qlat = qlat_ref[...]
    qpe = qpe_ref[...]
    rope = kbuf.shape[1]
    nt = (((1,), (1,)), ((), ()))

    def update(s, vals):
        m = m_scr[...]
        m_new = jnp.maximum(m, jnp.max(s, axis=-1, keepdims=True))
        p = jnp.exp2((s - m_new) * c)
        corr = jnp.exp2((m - m_new) * c)
        l_scr[...] = l_scr[...] * corr + jnp.sum(p, axis=-1, keepdims=True)
        acc_scr[...] = acc_scr[...] * corr + jnp.dot(p.astype(bf16), vals, preferred_element_type=f32)
        m_scr[...] = m_new

    ck = cbuf[slot].reshape(ppc * page, cbuf.shape[3]).astype(bf16)
    s = (lax.dot_general(qlat, ck, nt, preferred_element_type=f32)
         + jnp.dot(qpe[:, :rope], kbuf[slot].astype(bf16), preferred_element_type=f32))
    update(s, ck)

    @pl.when(ch == n_chunks - 1)
    def _():
        cn = cnew_ref[...]
        sn = (lax.dot_general(qlat, cn, nt, preferred_element_type=f32)
              + lax.dot_general(qpe, knew_ref[...], nt, preferred_element_type=f32))
        t_row = lax.broadcasted_iota(jnp.int32, sn.shape, 0) % steps
        key = lax.broadcasted_iota(jnp.int32, sn.shape, 1)
        update(jnp.where(key <= t_row, sn, -jnp.inf), cn)
        o_ref[...] = (acc_scr[...] * (1.0 / l_scr[...])).astype(o_ref.dtype)


def _decode_attn(page_table, qlat, qpe, cnew, knew, cache_c, cache_k, scale, steps, ppc):
    nb, rows, r_kv = qlat.shape
    n_pages = page_table.shape[1]
    page = cache_c.shape[1]
    rope = cache_k.shape[1]
    n_chunks = n_pages // ppc
    gs = pltpu.PrefetchScalarGridSpec(
        num_scalar_prefetch=1,
        grid=(nb, n_chunks),
        in_specs=[pl.BlockSpec((None, rows, r_kv), lambda b, ch, pt: (b, 0, 0)),
                  pl.BlockSpec((None, rows, LANE), lambda b, ch, pt: (b, 0, 0)),
                  pl.BlockSpec((None, LANE, r_kv), lambda b, ch, pt: (b, 0, 0)),
                  pl.BlockSpec((None, LANE, LANE), lambda b, ch, pt: (b, 0, 0)),
                  pl.BlockSpec(memory_space=pl.ANY),
                  pl.BlockSpec(memory_space=pl.ANY)],
        out_specs=pl.BlockSpec((None, rows, r_kv), lambda b, ch, pt: (b, 0, 0)),
        scratch_shapes=[pltpu.VMEM((2, ppc, page, r_kv), f32),
                        pltpu.VMEM((2, rope, ppc * page), f32),
                        pltpu.SemaphoreType.DMA((2, 2)),
                        pltpu.VMEM((rows, 1), f32), pltpu.VMEM((rows, 1), f32),
                        pltpu.VMEM((rows, r_kv), f32)],
    )
    return pl.pallas_call(
        functools.partial(_decode_kernel, ppc=ppc, n_chunks=n_chunks, n_pages=n_pages, steps=steps,
                          c=scale * LOG2E),
        grid_spec=gs,
        out_shape=jax.ShapeDtypeStruct((nb, rows, r_kv), bf16),
        compiler_params=_params("arbitrary", "arbitrary"),
    )(page_table.reshape(-1), qlat, qpe, cnew, knew, cache_c, cache_k)


def _rope_tables(pos, rope):
    half = rope // 2
    inv_freq = ROPE_THETA ** (-jnp.arange(half, dtype=f32) / half)
    ang = pos.astype(f32)[:, None] * inv_freq
    cos, sin = jnp.cos(ang), jnp.sin(ang)
    pad = jnp.zeros((pos.shape[0], LANE - rope), f32)
    return (jnp.concatenate([cos, cos, pad], axis=-1), jnp.concatenate([-sin, sin, pad], axis=-1))


def kernel(x_prompt, x_sample, cache_ckv, cache_kpe, state_conv, state_rglru, page_table, c_prompt, c_sample, mod_w, mod_b, mix_norm, ffn_norm, rg_w_in, rg_conv_w, rg_conv_b, rg_w_gates, rg_b_gates, rg_lambda, rg_w_out, kv_mod_w, kv_mod_b, kv_norm, kv_w_dkv, kv_latent_norm, kv_w_uk, kv_w_uv, mla_w_dq, mla_q_norm, mla_w_uq, mla_w_o, moe_w_group, moe_b_group, moe_w_expert, moe_b_expert, moe_w13, moe_w2, final_norm):
    bp, seq, d = x_prompt.shape
    bs, steps, _ = x_sample.shape
    d_rnn = rg_conv_w.shape[-1]
    cwid = rg_conv_w.shape[1]
    nb_rnn, rnn_blk = rg_w_gates.shape[1], rg_w_gates.shape[2]
    r_kv, n_heads, d_nope = kv_w_uk.shape
    d_v = kv_w_uv.shape[2]
    rope = cache_kpe.shape[-1]
    half = rope // 2
    page = cache_ckv.shape[1]
    past = page_table.shape[1] * page
    n_groups = moe_w_group.shape[-1]
    n_exp = moe_w_expert.shape[-1]
    scale = 1.0 / math.sqrt(d_nope + rope)
    assert d_nope == LANE and d_v == LANE and rope <= LANE and d_rnn % LANE == 0
    assert rg_w_in.shape[0] == 1 and mla_w_dq.shape[0] == 1 and mod_w.shape[0] == 2
    assert steps >= cwid - 1 and seq >= cwid - 1

    n_p, n_s = bp * seq, steps * bs
    gp = _Group(n_p, min(512, seq), seq, False)
    gs_ = _Group(n_s, min(512, n_s), 0, True)
    groups = (gp, gs_)

    xp = x_prompt.reshape(n_p, d)
    xs = jnp.swapaxes(x_sample, 0, 1).reshape(n_s, d)

    n_c = bp + bs
    n_c_pad = -(-n_c // SUBLANE) * SUBLANE
    c_all = jnp.concatenate([c_prompt, c_sample, jnp.zeros((n_c_pad - n_c, d), f32)], axis=0)
    mods = [_bias_matmul(c_all, mod_w, layer, mod_b[layer]) for layer in range(2)]
    kv_mods = _bias_matmul(c_all, kv_mod_w[None], 0, kv_mod_b)

    def mod_vec(m, idx, grp_i):
        rows = slice(0, bp) if grp_i == 0 else slice(bp, bp + bs)
        return m[rows, idx * d:(idx + 1) * d]

    w_in = rg_w_in[0].astype(bf16)
    w_out = rg_w_out[0].astype(bf16)
    starts, win = _gate_windows(d_rnn, rnn_blk)
    wg = _gate_weights(rg_w_gates[0], starts, win)
    bgr = rg_b_gates[0][:, :rnn_blk].reshape(1, d_rnn)
    bgi = rg_b_gates[0][:, rnn_blk:].reshape(1, d_rnn)
    conv_w = rg_conv_w[0]
    conv_b = rg_conv_b[0].reshape(1, d_rnn)
    lam = rg_lambda[0].reshape(1, d_rnn)

    w_dkv = jnp.concatenate([kv_w_dkv, jnp.zeros((d, LANE - rope), f32)], axis=1).astype(bf16)
    w_up = jnp.concatenate([kv_w_uk.reshape(r_kv, n_heads * d_nope), kv_w_uv.reshape(r_kv, n_heads * d_v)],
                           axis=1).astype(bf16)
    w_dq = mla_w_dq[0].astype(bf16)
    wq3 = mla_w_uq[0].reshape(-1, n_heads, d_nope + rope)
    wq = jnp.concatenate([wq3, jnp.zeros((wq3.shape[0], n_heads, LANE - rope), f32)], axis=-1)
    wq = wq.reshape(-1, n_heads * 2 * LANE).astype(bf16)
    w_o = mla_w_o[0].astype(bf16)
    w_uk_t = jnp.transpose(kv_w_uk, (1, 2, 0))
    w_uv_flat = kv_w_uv.reshape(r_kv, n_heads * d_v)

    cos_p, sin_p = _rope_tables(jnp.arange(seq, dtype=jnp.int32), rope)
    cos_s, sin_s = _rope_tables(past + jnp.arange(steps, dtype=jnp.int32), rope)
    tabs = ((cos_p, sin_p), (cos_s, sin_s))

    def moe(xs_in, layer):
        xn_l, rt_l = [], []
        n_pad = LANE - n_groups - n_exp
        wr = jnp.concatenate([moe_w_group[layer], moe_w_expert[layer], jnp.zeros((d, n_pad), f32)], axis=1)
        br = jnp.concatenate([moe_b_group[layer], moe_b_expert[layer], jnp.zeros((n_pad,), f32)]).reshape(1, LANE)
        for gi, grp in enumerate(groups):
            xn, rt = _router(xs_in[gi], ffn_norm[layer], mod_vec(mods[layer], 3, gi),
                             mod_vec(mods[layer], 4, gi), wr, br, grp, n_groups, n_exp)
            xn_l.append(xn)
            rt_l.append(rt)
        xn_all = jnp.concatenate(xn_l, axis=0)
        route = jnp.concatenate(rt_l, axis=0)
        expert_ids = route[:, :TOP_K].astype(jnp.int32)
        weights = route[:, TOP_K:2 * TOP_K]
        slot_tok, block_expert, n_used, next_expert, dest = _dispatch(expert_ids, n_exp)
        yb = _experts(xn_all, slot_tok, block_expert, n_used, next_expert, moe_w13, moe_w2, layer)
        out = []
        lo = 0
        for grp in groups:
            dg = dest[lo:lo + grp.n]
            out.append((yb[dg[:, 0]], yb[dg[:, 1]], weights[lo:lo + grp.n]))
            lo += grp.n
        return out

    x_cur = [xp, xs]
    projs, h_last = [], []
    for gi, grp in enumerate(groups):
        proj = _normmod_matmul(x_cur[gi], mix_norm[0], mod_vec(mods[0], 0, gi), mod_vec(mods[0], 1, gi),
                               w_in, grp, tn_cap=896)
        projs.append(proj)
        if gi == 0:
            hg, hl = _rglru_prompt(proj, jnp.zeros((bp, 1, d_rnn), f32), jnp.zeros((bp, SUBLANE, d_rnn), f32),
                                   conv_w, conv_b, wg, bgr, bgi, lam, starts, win, bp, seq, min(256, seq))
            hl = hl.reshape(bp, d_rnn)
        else:
            cst = jnp.swapaxes(state_conv[0], 0, 1)
            hg, hl = _rglru_sample(proj.reshape(steps, bs, 2 * d_rnn), state_rglru[0], cst, conv_w, conv_b,
                                   wg, bgr, bgi, lam, starts, win)
            hg = hg.reshape(n_s, d_rnn)
        h_last.append(hl)
        x_cur[gi] = _matmul_res(hg, w_out, x_cur[gi], mod_vec(mods[0], 2, gi), grp)
    moe_out = moe(x_cur, 0)
    for gi, grp in enumerate(groups):
        ya, yc, wts = moe_out[gi]
        x_cur[gi] = _combine(x_cur[gi], ya, yc, wts, mod_vec(mods[0], 5, gi), grp)

    kv_out = []
    for gi, grp in enumerate(groups):
        kv_out.append(_kv_latent(x_cur[gi], kv_norm, kv_mods[:, :d][(slice(0, bp) if gi == 0 else slice(bp, bp + bs))],
                                 kv_mods[:, d:][(slice(0, bp) if gi == 0 else slice(bp, bp + bs))],
                                 w_dkv, kv_latent_norm, tabs[gi][0], tabs[gi][1], grp, half,
                                 w_up=w_up if gi == 0 else None))
    qs = [_q_proj(x_cur[gi], mix_norm[1], mod_vec(mods[1], 0, gi), mod_vec(mods[1], 1, gi), w_dq,
                  mla_q_norm[0], wq, tabs[gi][0], tabs[gi][1], grp, n_heads, half)
          for gi, grp in enumerate(groups)]

    ckv_p, kpe_p, kn_p, v_p, kpeb_p = kv_out[0]
    tq = min(ATTN_ROWS, seq)
    hgrp = 4 if n_heads % 4 == 0 else 1
    attn_p = _attn_prompt(qs[0], kn_p, v_p, kpeb_p, bp, seq, n_heads, scale, tq, hgrp)

    ckv_s, kpe_s = kv_out[1]
    rows = n_heads * steps
    qlat = _head_matmul(
        qs[1], pl.BlockSpec((n_s, LANE), lambda h: (0, 2 * h)),
        w_uk_t, pl.BlockSpec((None, d_nope, r_kv), lambda h: (h, 0, 0)),
        jax.ShapeDtypeStruct((n_heads, n_s, r_kv), bf16),
        pl.BlockSpec((None, n_s, r_kv), lambda h: (h, 0, 0)), n_heads)
    qlat = jnp.transpose(qlat.reshape(n_heads, steps, bs, r_kv), (2, 0, 1, 3)).reshape(bs, rows, r_kv)
    qpe = qs[1].reshape(steps, bs, n_heads, 2, LANE)[:, :, :, 1]
    qpe = jnp.transpose(qpe, (1, 2, 0, 3)).reshape(bs, rows, LANE)

    def new_keys(a):
        a = jnp.swapaxes(a.reshape(steps, bs, a.shape[-1]), 0, 1).astype(bf16)
        return jnp.concatenate([a, jnp.zeros((bs, LANE - steps, a.shape[-1]), bf16)], axis=1)

    ppc = _pick_tile(page_table.shape[1], DECODE_PAGES, 1)
    olat = _decode_attn(page_table, qlat, qpe, new_keys(ckv_s), new_keys(kpe_s), cache_ckv,
                        jnp.swapaxes(cache_kpe, 1, 2), scale, steps, ppc)
    olat = jnp.transpose(olat.reshape(bs, n_heads, steps, r_kv), (1, 2, 0, 3)).reshape(n_heads, n_s, r_kv)
    attn_s = _head_matmul(
        olat, pl.BlockSpec((None, n_s, r_kv), lambda h: (h, 0, 0)),
        w_uv_flat, pl.BlockSpec((r_kv, d_v), lambda h: (0, h)),
        jax.ShapeDtypeStruct((n_s, n_heads * d_v), bf16),
        pl.BlockSpec((n_s, d_v), lambda h: (0, h)), n_heads)

    attn = [attn_p, attn_s]
    for gi, grp in enumerate(groups):
        x_cur[gi] = _matmul_res(attn[gi], w_o, x_cur[gi], mod_vec(mods[1], 2, gi), grp)
    moe_out = moe(x_cur, 1)
    ys = []
    for gi, grp in enumerate(groups):
        ya, yc, wts = moe_out[gi]
        ys.append(_combine(x_cur[gi], ya, yc, wts, mod_vec(mods[1], 5, gi), grp, final_gain=final_norm))

    y_prompt = ys[0].reshape(bp, seq, d)
    y_sample = jnp.swapaxes(ys[1].reshape(steps, bs, d), 0, 1)
    conv_p = projs[0].reshape(bp, seq, 2 * d_rnn)[:, seq - (cwid - 1):, d_rnn:][None]
    conv_s = jnp.swapaxes(projs[1].reshape(steps, bs, 2 * d_rnn)[steps - (cwid - 1):, :, d_rnn:], 0, 1)[None]
    h_p = h_last[0][None]
    h_s = h_last[1][None]
    ckv_prompt = ckv_p.reshape(bp, seq, r_kv)
    kpe_prompt = kpe_p[:, :rope].reshape(bp, seq, rope)
    ckv_sample = jnp.swapaxes(ckv_s.reshape(steps, bs, r_kv), 0, 1)
    kpe_sample = jnp.swapaxes(kpe_s[:, :rope].reshape(steps, bs, rope), 0, 1)
    return (y_prompt, y_sample, conv_p, h_p, ckv_prompt, kpe_prompt, conv_s, h_s, ckv_sample, kpe_sample)
```

```python
import functools
import math
from typing import NamedTuple

import jax
import jax.numpy as jnp
from jax import lax
from jax.experimental import pallas as pl
from jax.experimental.pallas import tpu as pltpu

EPS = 1e-6
LRU_C = 8.0
ROPE_THETA = 10000.0
TOP_K = 2
LANE = 128
SUBLANE = 8
VMEM_LIMIT = 56 * 1024 * 1024
MOE_ROWS = 128
ATTN_ROWS = 512
SAMPLE_ROWS = 64
DECODE_PAGES = 32
DECODE_BATCH = 1
LOG2E = 1.4426950408889634

f32 = jnp.float32
bf16 = jnp.bfloat16


def _params(*sem):
    return pltpu.CompilerParams(dimension_semantics=sem, vmem_limit_bytes=VMEM_LIMIT)


def _pick_tile(n, cap, mult=LANE):
    if n <= cap:
        return n
    best = None
    for t in range(mult, cap + 1, mult):
        if n % t == 0:
            best = t
    assert best is not None, (n, cap, mult)
    return best


class _Group(NamedTuple):
    n: int
    tm: int
    seq: int
    per_token: bool


def _mod_operand(vec, grp):
    d = vec.shape[-1]
    if grp.per_token:
        arr = jnp.tile(vec, (grp.n // vec.shape[0], 1))
        return arr, pl.BlockSpec((grp.tm, d), lambda i, *_: (i, 0))
    per_seq = grp.seq // grp.tm
    return vec[:, None, :], pl.BlockSpec((None, 1, d), lambda i, *_: (i // per_seq, 0, 0))


def _pos_operand(tab, grp):
    if grp.per_token:
        arr = jnp.repeat(tab, grp.n // tab.shape[0], axis=0)
        return arr, pl.BlockSpec((grp.tm, LANE), lambda i, *_: (i, 0))
    per_seq = grp.seq // grp.tm
    return tab, pl.BlockSpec((grp.tm, LANE), lambda i, *_: (i % per_seq, 0))


def _row_spec(grp, width):
    return pl.BlockSpec((grp.tm, width), lambda i, *_: (i, 0))


def _full_spec(shape):
    nd = len(shape)
    return pl.BlockSpec(shape, lambda *_: (0,) * nd)


def _rms(x, gain):
    return x * lax.rsqrt(jnp.mean(x * x, axis=-1, keepdims=True) + EPS) * gain


def _normmod(x, gain, shift, scale):
    return _rms(x, gain) * (1.0 + scale) + shift


def _rope_tile(t, cos, sin, half):
    lane = lax.broadcasted_iota(jnp.int32, t.shape, 1)
    rot = jnp.where(lane < half, pltpu.roll(t, LANE - half, 1), pltpu.roll(t, half, 1))
    return t * cos + rot * sin


def _sigmoid(x):
    return 0.5 * (jnp.tanh(0.5 * x) + 1.0)


def _gelu(x):
    return 0.5 * x * (1.0 + jnp.tanh(math.sqrt(2.0 / math.pi) * (x + 0.044715 * (x * x * x))))


def _softplus(z):
    return jnp.maximum(z, 0.0) + jnp.log1p(jnp.exp(-jnp.abs(z)))


def _bias_matmul_kernel(a_ref, w_ref, b_ref, o_ref):
    o_ref[...] = jnp.dot(a_ref[...].astype(bf16), w_ref[...].astype(bf16),
                         preferred_element_type=f32) + b_ref[...]


def _bias_matmul(a, w3, layer, b):
    m, k = a.shape
    n = w3.shape[-1]
    tn = _pick_tile(n, 1024)
    return pl.pallas_call(
        _bias_matmul_kernel,
        grid=(n // tn,),
        in_specs=[_full_spec((m, k)),
                  pl.BlockSpec((None, k, tn), lambda j: (layer, 0, j)),
                  pl.BlockSpec((1, tn), lambda j: (0, j))],
        out_specs=pl.BlockSpec((m, tn), lambda j: (0, j)),
        out_shape=jax.ShapeDtypeStruct((m, n), f32),
        compiler_params=_params("arbitrary"),
    )(a, w3, b.reshape(1, n))


def _normmod_matmul_kernel(x_ref, gain_ref, shift_ref, scale_ref, w_ref, o_ref, xn_ref):
    @pl.when(pl.program_id(1) == 0)
    def _():
        xn_ref[...] = _normmod(x_ref[...], gain_ref[...], shift_ref[...], scale_ref[...]).astype(bf16)

    o_ref[...] = jnp.dot(xn_ref[...], w_ref[...], preferred_element_type=f32).astype(o_ref.dtype)


def _normmod_matmul(x, gain, shift, scale, w, grp, tn_cap=1024, out_dtype=f32):
    d = x.shape[1]
    n = w.shape[1]
    tn = _pick_tile(n, tn_cap)
    sh, sh_spec = _mod_operand(shift, grp)
    sc, sc_spec = _mod_operand(scale, grp)
    return pl.pallas_call(
        _normmod_matmul_kernel,
        grid=(grp.n // grp.tm, n // tn),
        in_specs=[_row_spec(grp, d), _full_spec((1, d)), sh_spec, sc_spec,
                  pl.BlockSpec((d, tn), lambda i, j: (0, j))],
        out_specs=pl.BlockSpec((grp.tm, tn), lambda i, j: (i, j)),
        out_shape=jax.ShapeDtypeStruct((grp.n, n), out_dtype),
        scratch_shapes=[pltpu.VMEM((grp.tm, d), bf16)],
        compiler_params=_params("arbitrary", "arbitrary"),
    )(x, gain.reshape(1, d), sh, sc, w)


def _matmul_res_kernel(a_ref, w_ref, res_ref, gate_ref, o_ref):
    y = jnp.dot(a_ref[...].astype(bf16), w_ref[...], preferred_element_type=f32)
    o_ref[...] = res_ref[...] + gate_ref[...] * y


def _matmul_res(a, w, res, gate, grp, tn_cap=1024):
    k = a.shape[1]
    n = w.shape[1]
    tn = _pick_tile(n, tn_cap)
    if grp.per_token:
        g_arr = jnp.tile(gate, (grp.n // gate.shape[0], 1))
        g_spec = pl.BlockSpec((grp.tm, tn), lambda j, i: (i, j))
    else:
        per_seq = grp.seq // grp.tm
        g_arr = gate[:, None, :]
        g_spec = pl.BlockSpec((None, 1, tn), lambda j, i: (i // per_seq, 0, j))
    return pl.pallas_call(
        _matmul_res_kernel,
        grid=(n // tn, grp.n // grp.tm),
        in_specs=[pl.BlockSpec((grp.tm, k), lambda j, i: (i, 0)),
                  pl.BlockSpec((k, tn), lambda j, i: (0, j)),
                  pl.BlockSpec((grp.tm, tn), lambda j, i: (i, j)),
                  g_spec],
        out_specs=pl.BlockSpec((grp.tm, tn), lambda j, i: (i, j)),
        out_shape=jax.ShapeDtypeStruct((grp.n, n), f32),
        compiler_params=_params("arbitrary", "arbitrary"),
    )(a, w, res, g_arr)


def _gate_windows(d_rnn, blk):
    nt = d_rnn // LANE
    raw, need = [], 0
    for j in range(nt):
        n0 = (LANE * j) // blk
        n1 = (LANE * j + LANE - 1) // blk
        s = (blk * n0) // LANE * LANE
        raw.append(s)
        need = max(need, blk * (n1 + 1) - s)
    win = min(d_rnn, -(-need // LANE) * LANE)
    return [min(s, d_rnn - win) for s in raw], win


def _gate_weights(w_gates, starts, win):
    nb, blk, _ = w_gates.shape
    wb = w_gates.astype(bf16)
    tiles = []
    for j, s in enumerate(starts):
        halves = []
        for off in (0, blk):
            acc = None
            for n in range((LANE * j) // blk, (LANE * j + LANE - 1) // blk + 1):
                c_lo = max(blk * n, LANE * j)
                c_hi = min(blk * (n + 1), LANE * (j + 1))
                r_off = blk * n - s
                assert 0 <= r_off and r_off + blk <= win
                piece = wb[n, :, off + c_lo - blk * n:off + c_hi - blk * n]
                piece = jnp.pad(piece, ((r_off, win - blk - r_off), (c_lo - LANE * j, LANE * (j + 1) - c_hi)))
                acc = piece if acc is None else acc + piece
            halves.append(acc)
        tiles.append(jnp.concatenate(halves, axis=1))
    return jnp.stack(tiles)


def _lru_inputs(g, b_r, b_i, neg_c_sp, u):
    r = _sigmoid(g[:, :LANE] + b_r)
    i = _sigmoid(g[:, LANE:] + b_i)
    log_a = neg_c_sp * r
    a = jnp.exp(log_a)
    x = jnp.sqrt(-jnp.tanh(log_a) * (a * a + 1.0)) * (i * u)
    return a, x


def _rglru_prompt_kernel(proj_ref, h0_ref, cinit_ref, cw_ref, cb_ref, wg_ref, bgr_ref, bgi_ref, lam_ref,
                         hg_ref, hlast_ref, ubuf, ucf, ucb, a_scr, x_scr, hcar,
                         *, starts, win, tc, d_rnn, cwid):
    t = pl.program_id(1)
    hist = SUBLANE

    @pl.when(t == 0)
    def _():
        ubuf[0:hist, :] = cinit_ref[...]
        hcar[...] = h0_ref[...]

    ubuf[hist:hist + tc, :] = proj_ref[:, d_rnn:]
    uc = cb_ref[...]
    for k in range(cwid):
        off = hist - (cwid - 1) + k
        uc = uc + cw_ref[k:k + 1, :] * ubuf[off:off + tc, :]
    ucf[...] = uc
    ucb[...] = uc.astype(bf16)
    ubuf[0:hist, :] = ubuf[tc:tc + hist, :]

    neg_c_sp = -LRU_C * _softplus(-lam_ref[...])
    seg = tc // SUBLANE
    for j in range(d_rnn // LANE):
        cs = slice(j * LANE, (j + 1) * LANE)
        g = jnp.dot(ucb[:, starts[j]:starts[j] + win], wg_ref[j], preferred_element_type=f32)
        a, x = _lru_inputs(g, bgr_ref[:, cs], bgi_ref[:, cs], neg_c_sp[:, cs], ucf[:, cs])
        a_scr[j] = a
        x_scr[j] = x
        h = jnp.zeros((SUBLANE, LANE), f32)
        p = jnp.ones((SUBLANE, LANE), f32)
        for k in range(seg):
            ak = a_scr[j, pl.ds(k, SUBLANE, stride=seg), :]
            xk = x_scr[j, pl.ds(k, SUBLANE, stride=seg), :]
            h = ak * h + xk
            p = p * ak
            x_scr[j, pl.ds(k, SUBLANE, stride=seg), :] = h
            a_scr[j, pl.ds(k, SUBLANE, stride=seg), :] = p
        c = hcar[:, cs]
        outs = []
        for r in range(SUBLANE):
            rows = slice(r * seg, (r + 1) * seg)
            outs.append(x_scr[j, rows, :] + a_scr[j, rows, :] * c)
            c = p[r:r + 1, :] * c + h[r:r + 1, :]
        hcar[:, cs] = c
        hs = jnp.concatenate(outs, axis=0)
        hg_ref[:, cs] = (hs * _gelu(proj_ref[:, cs])).astype(hg_ref.dtype)
    hlast_ref[...] = hcar[...]


def _rglru_prompt(proj, h0, conv_init, conv_w, conv_b, wg, bgr, bgi, lam, starts, win, nb_seq, seq, tc):
    d_rnn = proj.shape[1] // 2
    cwid = conv_w.shape[0]
    nt = d_rnn // LANE
    per_seq = seq // tc
    kern = functools.partial(_rglru_prompt_kernel, starts=tuple(starts), win=win, tc=tc, d_rnn=d_rnn, cwid=cwid)
    return pl.pallas_call(
        kern,
        grid=(nb_seq, per_seq),
        in_specs=[pl.BlockSpec((tc, 2 * d_rnn), lambda b, t: (b * per_seq + t, 0)),
                  pl.BlockSpec((None, 1, d_rnn), lambda b, t: (b, 0, 0)),
                  pl.BlockSpec((None, SUBLANE, d_rnn), lambda b, t: (b, 0, 0)),
                  _full_spec((cwid, d_rnn)), _full_spec((1, d_rnn)),
                  _full_spec(wg.shape), _full_spec((1, d_rnn)), _full_spec((1, d_rnn)), _full_spec((1, d_rnn))],
        out_specs=[pl.BlockSpec((tc, d_rnn), lambda b, t: (b * per_seq + t, 0)),
                   pl.BlockSpec((None, 1, d_rnn), lambda b, t: (b, 0, 0))],
        out_shape=[jax.ShapeDtypeStruct((nb_seq * seq, d_rnn), bf16),
                   jax.ShapeDtypeStruct((nb_seq, 1, d_rnn), f32)],
        scratch_shapes=[pltpu.VMEM((tc + SUBLANE, d_rnn), f32),
                        pltpu.VMEM((tc, d_rnn), f32),
                        pltpu.VMEM((tc, d_rnn), bf16),
                        pltpu.VMEM((nt, tc, LANE), f32),
                        pltpu.VMEM((nt, tc, LANE), f32),
                        pltpu.VMEM((1, d_rnn), f32)],
        compiler_params=_params("arbitrary", "arbitrary"),
    )(proj, h0, conv_init, conv_w, conv_b, wg, bgr, bgi, lam)


def _rglru_sample_kernel(proj_ref, h0_ref, cst_ref, cw_ref, cb_ref, wg_ref, bgr_ref, bgi_ref, lam_ref,
                         hg_ref, hlast_ref, ucf, ucb, *, starts, win, steps, tb, d_rnn, cwid):
    def up(tp):
        if tp < cwid - 1:
            return cst_ref[tp]
        return proj_ref[tp - cwid + 1, :, d_rnn:]

    for t in range(steps):
        uc = cb_ref[...]
        for k in range(cwid):
            uc = uc + cw_ref[k:k + 1, :] * up(t + k)
        ucf[t * tb:(t + 1) * tb, :] = uc
        ucb[t * tb:(t + 1) * tb, :] = uc.astype(bf16)

    neg_c_sp = -LRU_C * _softplus(-lam_ref[...])
    for j in range(d_rnn // LANE):
        cs = slice(j * LANE, (j + 1) * LANE)
        g = jnp.dot(ucb[:, starts[j]:starts[j] + win], wg_ref[j], preferred_element_type=f32)
        h = h0_ref[:, cs]
        for t in range(steps):
            rows = slice(t * tb, (t + 1) * tb)
            a, x = _lru_inputs(g[rows], bgr_ref[:, cs], bgi_ref[:, cs], neg_c_sp[:, cs], ucf[rows, cs])
            h = a * h + x
            hg_ref[t, :, cs] = (h * _gelu(proj_ref[t, :, cs])).astype(hg_ref.dtype)
        hlast_ref[:, cs] = h


def _rglru_sample(proj, h0, conv_state, conv_w, conv_b, wg, bgr, bgi, lam, starts, win):
    steps, nb, _ = proj.shape
    d_rnn = proj.shape[2] // 2
    cwid = conv_w.shape[0]
    tb = min(SAMPLE_ROWS, nb)
    kern = functools.partial(_rglru_sample_kernel, starts=tuple(starts), win=win, steps=steps, tb=tb,
                             d_rnn=d_rnn, cwid=cwid)
    return pl.pallas_call(
        kern,
        grid=(nb // tb,),
        in_specs=[pl.BlockSpec((steps, tb, 2 * d_rnn), lambda i: (0, i, 0)),
                  pl.BlockSpec((tb, d_rnn), lambda i: (i, 0)),
                  pl.BlockSpec((cwid - 1, tb, d_rnn), lambda i: (0, i, 0)),
                  _full_spec((cwid, d_rnn)), _full_spec((1, d_rnn)),
                  _full_spec(wg.shape), _full_spec((1, d_rnn)), _full_spec((1, d_rnn)), _full_spec((1, d_rnn))],
        out_specs=[pl.BlockSpec((steps, tb, d_rnn), lambda i: (0, i, 0)),
                   pl.BlockSpec((tb, d_rnn), lambda i: (i, 0))],
        out_shape=[jax.ShapeDtypeStruct((steps, nb, d_rnn), bf16),
                   jax.ShapeDtypeStruct((nb, d_rnn), f32)],
        scratch_shapes=[pltpu.VMEM((steps * tb, d_rnn), f32), pltpu.VMEM((steps * tb, d_rnn), bf16)],
        compiler_params=_params("arbitrary"),
    )(proj, h0, conv_state, conv_w, conv_b, wg, bgr, bgi, lam)


def _first_max(vals, ids):
    m = jnp.max(vals, axis=-1, keepdims=True)
    idx = jnp.min(jnp.where(vals == m, ids, jnp.int32(2 ** 30)), axis=-1, keepdims=True)
    return m, idx


def _router_kernel(x_ref, gain_ref, shift_ref, scale_ref, wh_ref, wl_ref, b_ref, *rest, n_groups, n_exp):
    xn_ref, rt_ref = rest[-2:]
    xn = _normmod(x_ref[...], gain_ref[...], shift_ref[...], scale_ref[...])
    xn_ref[...] = xn
    xh = xn.astype(bf16)
    xl = (xn - xh.astype(f32)).astype(bf16)
    lg = (jnp.dot(xh, wh_ref[...], preferred_element_type=f32)
          + jnp.dot(xl, wh_ref[...], preferred_element_type=f32)
          + jnp.dot(xh, wl_ref[...], preferred_element_type=f32)
          + jnp.dot(xl, wl_ref[...], preferred_element_type=f32)) + b_ref[...]
    per = n_exp // n_groups
    lane = lax.broadcasted_iota(jnp.int32, lg.shape, 1)
    neg = -jnp.inf
    gl = jnp.where(lane < n_groups, lg, neg)
    gmax, g_idx = _first_max(gl, lane)
    g_w = 1.0 / jnp.sum(jnp.exp(gl - gmax), axis=-1, keepdims=True)
    e_id = lane - n_groups
    lo = g_idx * per
    el = jnp.where(e_id >= lo, jnp.where(e_id < lo + per, lg, neg), neg)
    l1, i1 = _first_max(el, e_id)
    el2 = jnp.where(e_id == i1, neg, el)
    l2, i2 = _first_max(el2, e_id)
    e = jnp.exp(l2 - l1)
    w1 = g_w / (1.0 + e)
    w2 = w1 * e
    rt_ref[...] = jnp.where(lane == 0, i1.astype(f32),
                            jnp.where(lane == 1, i2.astype(f32),
                                      jnp.where(lane == 2, w1, jnp.where(lane == 3, w2, 0.0))))


def _router(x, gain, shift, scale, wr, br, grp, n_groups, n_exp, n_total, row0, xn_all=None):
    d = x.shape[1]
    nr = wr.shape[1]
    wh = wr.astype(bf16)
    wl = (wr - wh.astype(f32)).astype(bf16)
    sh, sh_spec = _mod_operand(shift, grp)
    sc, sc_spec = _mod_operand(scale, grp)
    assert row0 % grp.tm == 0 and n_total % grp.tm == 0
    t0 = row0 // grp.tm
    n_tiles = grp.n // grp.tm
    n_steps = n_tiles if xn_all is not None else n_total // grp.tm

    def clamp(spec):
        return pl.BlockSpec(spec.block_shape, lambda i: spec.index_map(jnp.minimum(i, n_tiles - 1)))

    args = [x, gain.reshape(1, d), sh, sc, wh, wl, br]
    specs = [clamp(_row_spec(grp, d)), _full_spec((1, d)), clamp(sh_spec), clamp(sc_spec),
             _full_spec((d, nr)), _full_spec((d, nr)), _full_spec((1, nr))]
    aliases = {}
    if xn_all is not None:
        aliases = {len(args): 0}
        args.append(xn_all)
        specs.append(pl.BlockSpec(memory_space=pl.ANY))
    return pl.pallas_call(
        functools.partial(_router_kernel, n_groups=n_groups, n_exp=n_exp),
        grid=(n_steps,),
        in_specs=specs,
        out_specs=[pl.BlockSpec((grp.tm, d), lambda i: (i + t0, 0)), clamp(_row_spec(grp, nr))],
        out_shape=[jax.ShapeDtypeStruct((n_total, d), f32), jax.ShapeDtypeStruct((grp.n, nr), f32)],
        input_output_aliases=aliases,
        compiler_params=_params("arbitrary"),
    )(*args)


def _experts_kernel(be_ref, nu_ref, st_ref, nx_ref, xn_hbm, w13_hbm, w2_hbm, o_ref,
                    xbuf, sem, w13f, w2f, wsem, w13b, w2b, *, d_exp, layer):
    i = pl.program_id(0)
    n_used = nu_ref[0]
    slot = i % 2
    e = be_ref[i]
    prev = be_ref[jnp.maximum(i - 1, 0)]

    def row_copy(tok, sl, r):
        return pltpu.make_async_copy(xn_hbm.at[pl.ds(tok, 1), :], xbuf.at[sl, pl.ds(r, 1), :], sem.at[sl])

    def gather(blk, sl):
        for r in range(MOE_ROWS):
            row_copy(st_ref[blk * MOE_ROWS + r], sl, r).start()

    def weight_copies(ex):
        return (pltpu.make_async_copy(w13_hbm.at[layer, ex], w13f, wsem.at[0]),
                pltpu.make_async_copy(w2_hbm.at[layer, ex], w2f, wsem.at[1]))

    @pl.when((i == 0) & (n_used > 0))
    def _():
        gather(0, 0)
        for cp in weight_copies(e):
            cp.start(priority=1)

    @pl.when(i + 1 < n_used)
    def _():
        gather(i + 1, 1 - slot)

    @pl.when((i < n_used) & ((i == 0) | (e != prev)))
    def _():
        for cp in weight_copies(e):
            cp.wait()
        w13b[...] = w13f[...].astype(bf16)
        w2b[...] = w2f[...].astype(bf16)

        @pl.when(nx_ref[i] >= 0)
        def _():
            for cp in weight_copies(nx_ref[i]):
                cp.start(priority=1)

    @pl.when(i < n_used)
    def _():
        for r in range(MOE_ROWS):
            row_copy(0, slot, r).wait()
        gu = jnp.dot(xbuf[slot].astype(bf16), w13b[...], preferred_element_type=f32)
        g = gu[:, :d_exp]
        act = (g * _sigmoid(g)) * gu[:, d_exp:]
        o_ref[...] = jnp.dot(act.astype(bf16), w2b[...], preferred_element_type=f32)

    @pl.when(i >= n_used)
    def _():
        o_ref[...] = jnp.zeros_like(o_ref)


def _experts(xn, slot_tok, block_expert, n_used, next_expert, w13, w2, layer):
    d = xn.shape[1]
    n_blocks = slot_tok.shape[0] // MOE_ROWS
    d_exp = w2.shape[2]
    gs = pltpu.PrefetchScalarGridSpec(
        num_scalar_prefetch=4,
        grid=(n_blocks,),
        in_specs=[pl.BlockSpec(memory_space=pl.ANY), pl.BlockSpec(memory_space=pl.ANY),
                  pl.BlockSpec(memory_space=pl.ANY)],
        out_specs=pl.BlockSpec((MOE_ROWS, d), lambda i, *_: (i, 0)),
        scratch_shapes=[pltpu.VMEM((2, MOE_ROWS, d), f32), pltpu.SemaphoreType.DMA((2,)),
                        pltpu.VMEM((d, 2 * d_exp), f32), pltpu.VMEM((d_exp, d), f32),
                        pltpu.SemaphoreType.DMA((2,)),
                        pltpu.VMEM((d, 2 * d_exp), bf16), pltpu.VMEM((d_exp, d), bf16)],
    )
    return pl.pallas_call(
        functools.partial(_experts_kernel, d_exp=d_exp, layer=layer),
        grid_spec=gs,
        out_shape=jax.ShapeDtypeStruct((n_blocks * MOE_ROWS, d), f32),
        compiler_params=_params("arbitrary"),
    )(block_expert, n_used, slot_tok, next_expert, xn, w13, w2)


def _combine_kernel(x_ref, ya_ref, yb_ref, wt_ref, gate_ref, *rest, final):
    if final:
        fg_ref, o_ref = rest
    else:
        (o_ref,) = rest
    wt = wt_ref[...]
    y = x_ref[...] + gate_ref[...] * (wt[:, 0:1] * ya_ref[...] + wt[:, 1:2] * yb_ref[...])
    if final:
        y = _rms(y, fg_ref[...])
    o_ref[...] = y


def _combine(x, ya, yb, wts, gate, grp, final_gain=None):
    d = x.shape[1]
    g_arr, g_spec = _mod_operand(gate, grp)
    final = final_gain is not None
    args = [x, ya, yb, wts, g_arr]
    specs = [_row_spec(grp, d), _row_spec(grp, d), _row_spec(grp, d), _row_spec(grp, TOP_K), g_spec]
    if final:
        args.append(final_gain.reshape(1, d))
        specs.append(_full_spec((1, d)))
    return pl.pallas_call(
        functools.partial(_combine_kernel, final=final),
        grid=(grp.n // grp.tm,),
        in_specs=specs,
        out_specs=_row_spec(grp, d),
        out_shape=jax.ShapeDtypeStruct((grp.n, d), f32),
        compiler_params=_params("arbitrary"),
    )(*args)


def _dispatch(expert_ids, n_exp):
    n = expert_ids.shape[0]
    na = n * TOP_K
    flat_e = expert_ids.reshape(na)
    onehot = (flat_e[:, None] == jnp.arange(n_exp, dtype=jnp.int32)[None, :]).astype(jnp.int32)
    incl = jnp.cumsum(onehot, axis=0)
    counts = incl[-1]
    padded = (counts + MOE_ROWS - 1) // MOE_ROWS * MOE_ROWS
    pad_end = jnp.cumsum(padded)
    pad_start = pad_end - padded
    dest = jnp.sum(onehot * (pad_start[None, :] + incl - 1), axis=1)
    n_blocks = (na + MOE_ROWS - 1) // MOE_ROWS + n_exp
    tok = jnp.arange(na, dtype=jnp.int32) // TOP_K
    slot_tok = jnp.zeros((n_blocks * MOE_ROWS,), jnp.int32).at[dest].set(tok)
    block_start = jnp.arange(n_blocks, dtype=jnp.int32) * MOE_ROWS
    block_expert = jnp.minimum(jnp.sum((pad_end[None, :] <= block_start[:, None]).astype(jnp.int32), axis=1),
                               n_exp - 1)
    n_used = pad_end[-1:] // MOE_ROWS
    bidx = jnp.arange(n_blocks, dtype=jnp.int32)
    is_first = (bidx == 0) | (block_expert != jnp.roll(block_expert, 1))
    pos = jnp.where(is_first & (bidx < n_used[0]), bidx, n_blocks)
    nxt = jnp.min(jnp.where(bidx[None, :] > bidx[:, None], pos[None, :], n_blocks), axis=1)
    next_expert = jnp.where(nxt < n_blocks, block_expert[jnp.minimum(nxt, n_blocks - 1)], -1)
    return slot_tok, block_expert, n_used, next_expert, dest.reshape(n, TOP_K)


def _kv_kernel(x_ref, gain_ref, shift_ref, scale_ref, w_ref, lg_ref, cos_ref, sin_ref, *rest,
               r_kv, half, with_up):
    if with_up:
        wup_ref, ckv_ref, kpe_ref, kn_ref, v_ref, kpeb_ref = rest
    else:
        ckv_ref, kpe_ref = rest
    a = _normmod(x_ref[...], gain_ref[...], shift_ref[...], scale_ref[...]).astype(bf16)
    kv = jnp.dot(a, w_ref[...], preferred_element_type=f32)
    c = _rms(kv[:, :r_kv], lg_ref[...])
    ckv_ref[...] = c
    kp = _rope_tile(kv[:, r_kv:], cos_ref[...], sin_ref[...], half)
    kpe_ref[...] = kp
    if with_up:
        up = jnp.dot(c.astype(bf16), wup_ref[...], preferred_element_type=f32)
        hw = up.shape[1] // 2
        kn_ref[...] = up[:, :hw].astype(bf16)
        v_ref[...] = up[:, hw:].astype(bf16)
        kpeb_ref[...] = kp.astype(bf16)


def _kv_latent(x, gain, shift, scale, w_pad, latent_gain, cos, sin, grp, half, w_up=None):
    d = x.shape[1]
    r_kv = latent_gain.shape[0]
    with_up = w_up is not None
    sh, sh_spec = _mod_operand(shift, grp)
    sc, sc_spec = _mod_operand(scale, grp)
    cs, cs_spec = _pos_operand(cos, grp)
    sn, sn_spec = _pos_operand(sin, grp)
    args = [x, gain.reshape(1, d), sh, sc, w_pad, latent_gain.reshape(1, r_kv), cs, sn]
    specs = [_row_spec(grp, d), _full_spec((1, d)), sh_spec, sc_spec, _full_spec(w_pad.shape),
             _full_spec((1, r_kv)), cs_spec, sn_spec]
    out_specs = [_row_spec(grp, r_kv), _row_spec(grp, LANE)]
    out_shape = [jax.ShapeDtypeStruct((grp.n, r_kv), f32), jax.ShapeDtypeStruct((grp.n, LANE), f32)]
    if with_up:
        hw = w_up.shape[1] // 2
        args.append(w_up)
        specs.append(_full_spec(w_up.shape))
        out_specs += [_row_spec(grp, hw), _row_spec(grp, hw), _row_spec(grp, LANE)]
        out_shape += [jax.ShapeDtypeStruct((grp.n, hw), bf16), jax.ShapeDtypeStruct((grp.n, hw), bf16),
                      jax.ShapeDtypeStruct((grp.n, LANE), bf16)]
    return pl.pallas_call(
        functools.partial(_kv_kernel, r_kv=r_kv, half=half, with_up=with_up),
        grid=(grp.n // grp.tm,),
        in_specs=specs, out_specs=out_specs, out_shape=out_shape,
        compiler_params=_params("arbitrary"),
    )(*args)


def _q_kernel(x_ref, gain_ref, shift_ref, scale_ref, wdq_ref, qg_ref, wq_ref, cos_ref, sin_ref, q_ref,
              *, n_heads, half):
    a = _normmod(x_ref[...], gain_ref[...], shift_ref[...], scale_ref[...]).astype(bf16)
    ql = jnp.dot(a, wdq_ref[...], preferred_element_type=f32)
    qn = _rms(ql, qg_ref[...]).astype(bf16)
    cos = cos_ref[...]
    sin = sin_ref[...]
    for h in range(n_heads):
        q = jnp.dot(qn, wq_ref[:, 2 * LANE * h:2 * LANE * (h + 1)], preferred_element_type=f32)
        q_ref[:, 2 * LANE * h:2 * LANE * h + LANE] = q[:, :LANE].astype(bf16)
        q_ref[:, 2 * LANE * h + LANE:2 * LANE * (h + 1)] = _rope_tile(q[:, LANE:], cos, sin, half).astype(bf16)


def _q_proj(x, gain, shift, scale, w_dq, q_gain, wq, cos, sin, grp, n_heads, half):
    d = x.shape[1]
    rq = w_dq.shape[1]
    sh, sh_spec = _mod_operand(shift, grp)
    sc, sc_spec = _mod_operand(scale, grp)
    cs, cs_spec = _pos_operand(cos, grp)
    sn, sn_spec = _pos_operand(sin, grp)
    return pl.pallas_call(
        functools.partial(_q_kernel, n_heads=n_heads, half=half),
        grid=(grp.n // grp.tm,),
        in_specs=[_row_spec(grp, d), _full_spec((1, d)), sh_spec, sc_spec, _full_spec(w_dq.shape),
                  _full_spec((1, rq)), _full_spec(wq.shape), cs_spec, sn_spec],
        out_specs=_row_spec(grp, wq.shape[1]),
        out_shape=jax.ShapeDtypeStruct((grp.n, wq.shape[1]), bf16),
        compiler_params=_params("arbitrary"),
    )(x, gain.reshape(1, d), sh, sc, w_dq, q_gain.reshape(1, rq), wq, cs, sn)


def _attn_kernel(q_ref, kn_ref, v_ref, kpe_ref, o_ref, m_scr, acc_scr, *, hg, tq, c):
    qi = pl.program_id(2)
    m_scr[...] = jnp.full_like(m_scr, -jnp.inf)
    acc_scr[...] = jnp.zeros_like(acc_scr)
    ones = jnp.ones((tq, LANE), bf16)
    n_lt = tq // LANE

    def block(ks, masked):
        kpe = kpe_ref[pl.ds(ks, tq), :]
        for h in range(hg):
            hs = slice(h * LANE, (h + 1) * LANE)
            k = jnp.concatenate([kn_ref[pl.ds(ks, tq), hs], kpe], axis=-1)
            s = lax.dot_general(q_ref[:, 2 * LANE * h:2 * LANE * (h + 1)], k, (((1,), (1,)), ((), ())),
                                preferred_element_type=f32)
            if masked:
                row = lax.broadcasted_iota(jnp.int32, (tq, tq), 0)
                col = lax.broadcasted_iota(jnp.int32, (tq, tq), 1)
                s = jnp.where(col <= row, s, -jnp.inf)
            tiles = [s[:, t * LANE:(t + 1) * LANE] for t in range(n_lt)]
            mx = tiles[0]
            for t in tiles[1:]:
                mx = jnp.maximum(mx, t)
            m = m_scr[h]
            m_new = jnp.maximum(m, jnp.max(mx, axis=-1, keepdims=True))
            p = jnp.concatenate([jnp.exp2((t - m_new) * c) for t in tiles], axis=-1).astype(bf16)
            corr = jnp.exp2((m - m_new) * c)
            v_aug = jnp.concatenate([v_ref[pl.ds(ks, tq), hs], ones], axis=-1)
            pv = jnp.dot(p, v_aug, preferred_element_type=f32)
            acc_scr[h, :, :LANE] = acc_scr[h, :, :LANE] * corr + pv[:, :LANE]
            acc_scr[h, :, LANE:] = acc_scr[h, :, LANE:] * corr + pv[:, LANE:]
            m_scr[h] = m_new

    def body(kb, carry):
        block(pl.multiple_of(kb * tq, tq), False)
        return carry

    lax.fori_loop(0, qi, body, 0)
    block(pl.multiple_of(qi * tq, tq), True)
    for h in range(hg):
        o_ref[:, h * LANE:(h + 1) * LANE] = (acc_scr[h, :, :LANE] * (1.0 / acc_scr[h, :, LANE:])).astype(o_ref.dtype)


def _attn_prompt(q, kn, v, kpe, nb_seq, seq, n_heads, scale, tq, hg):
    nq = seq // tq
    return pl.pallas_call(
        functools.partial(_attn_kernel, hg=hg, tq=tq, c=scale * LOG2E),
        grid=(nb_seq, n_heads // hg, nq),
        in_specs=[pl.BlockSpec((tq, hg * 2 * LANE), lambda b, g, i: (b * nq + i, g)),
                  pl.BlockSpec((seq, hg * LANE), lambda b, g, i: (b, g)),
                  pl.BlockSpec((seq, hg * LANE), lambda b, g, i: (b, g)),
                  pl.BlockSpec((seq, LANE), lambda b, g, i: (b, 0))],
        out_specs=pl.BlockSpec((tq, hg * LANE), lambda b, g, i: (b * nq + i, g)),
        out_shape=jax.ShapeDtypeStruct((nb_seq * seq, n_heads * LANE), bf16),
        scratch_shapes=[pltpu.VMEM((hg, tq, LANE), f32), pltpu.VMEM((hg, tq, 2 * LANE), f32)],
        compiler_params=_params("arbitrary", "arbitrary", "arbitrary"),
    )(q, kn, v, kpe)


def _head_matmul_kernel(a_ref, w_ref, o_ref):
    o_ref[...] = jnp.dot(a_ref[...].astype(bf16), w_ref[...].astype(bf16),
                         preferred_element_type=f32).astype(o_ref.dtype)


def _head_matmul(a, a_spec, w, w_spec, out_shape, out_spec, n_heads):
    return pl.pallas_call(
        _head_matmul_kernel,
        grid=(n_heads,),
        in_specs=[a_spec, w_spec],
        out_specs=out_spec,
        out_shape=out_shape,
        compiler_params=_params("arbitrary"),
    )(a, w)


def _decode_kernel(pt_ref, qlat_ref, qpe_ref, cnew_ref, knew_ref, cache_c, cache_k, o_ref,
                   cbuf, kbuf, sem, m_scr, l_scr, acc_scr, *, bp, ppc, n_chunks, n_pages, steps, c):
    g = pl.program_id(0)
    ch = pl.program_id(1)
    step = g * n_chunks + ch
    slot = step % 2
    total = pl.num_programs(0) * n_chunks
    page = cbuf.shape[3]
    rope = kbuf.shape[2]

    def copies(gg, cc, sl, bi, p):
        phys = pt_ref[(gg * bp + bi) * n_pages + cc * ppc + p]
        return (pltpu.make_async_copy(cache_c.at[phys], cbuf.at[sl, bi, p], sem.at[0, sl]),
                pltpu.make_async_copy(cache_k.at[phys], kbuf.at[sl, bi, :, pl.ds(p * page, page)], sem.at[1, sl]))

    def issue(gg, cc, sl):
        for bi in range(bp):
            for p in range(ppc):
                for cp in copies(gg, cc, sl, bi, p):
                    cp.start()

    @pl.when(step == 0)
    def _():
        issue(0, 0, 0)

    @pl.when(step + 1 < total)
    def _():
        nxt = step + 1
        issue(nxt // n_chunks, nxt % n_chunks, 1 - slot)

    for bi in range(bp):
        for p in range(ppc):
            for cp in copies(g, ch, slot, bi, p):
                cp.wait()

    @pl.when(ch == 0)
    def _():
        m_scr[...] = jnp.full_like(m_scr, -jnp.inf)
        l_scr[...] = jnp.zeros_like(l_scr)
        acc_scr[...] = jnp.zeros_like(acc_scr)

    nt = (((1,), (1,)), ((), ()))

    def update(state, s, vals):
        m, l, acc = state
        m_new = jnp.maximum(m, jnp.max(s, axis=-1, keepdims=True))
        p = jnp.exp2((s - m_new) * c)
        corr = jnp.exp2((m - m_new) * c)
        l = l * corr + jnp.sum(p, axis=-1, keepdims=True)
        acc = acc * corr + jnp.dot(p.astype(bf16), vals, preferred_element_type=f32)
        return m_new, l, acc

    states = [(m_scr[bi], l_scr[bi], acc_scr[bi]) for bi in range(bp)]
    for bi in range(bp):
        ck = cbuf[slot, bi].reshape(ppc * page, cbuf.shape[4]).astype(bf16)
        s = (lax.dot_general(qlat_ref[bi], ck, nt, preferred_element_type=f32)
             + jnp.dot(qpe_ref[bi, :, :rope], kbuf[slot, bi].astype(bf16), preferred_element_type=f32))
        states[bi] = update(states[bi], s, ck)
    for bi in range(bp):
        m_scr[bi], l_scr[bi], acc_scr[bi] = states[bi]

    @pl.when(ch == n_chunks - 1)
    def _():
        for bi in range(bp):
            cn = cnew_ref[bi]
            sn = (lax.dot_general(qlat_ref[bi], cn, nt, preferred_element_type=f32)
                  + lax.dot_general(qpe_ref[bi], knew_ref[bi], nt, preferred_element_type=f32))
            t_row = lax.broadcasted_iota(jnp.int32, sn.shape, 0) % steps
            key = lax.broadcasted_iota(jnp.int32, sn.shape, 1)
            _, l, acc = update(states[bi], jnp.where(key <= t_row, sn, -jnp.inf), cn)
            o_ref[bi] = (acc * (1.0 / l)).astype(o_ref.dtype)


def _decode_attn(page_table, qlat, qpe, cnew, knew, cache_c, cache_k, scale, steps, ppc, bp):
    nb, rows, r_kv = qlat.shape
    n_pages = page_table.shape[1]
    page = cache_c.shape[1]
    rope = cache_k.shape[1]
    n_chunks = n_pages // ppc

    def seq_spec(a, b):
        return pl.BlockSpec((bp, a, b), lambda g, ch, pt: (g, 0, 0))

    gs = pltpu.PrefetchScalarGridSpec(
        num_scalar_prefetch=1,
        grid=(nb // bp, n_chunks),
        in_specs=[seq_spec(rows, r_kv), seq_spec(rows, LANE), seq_spec(LANE, r_kv), seq_spec(LANE, LANE),
                  pl.BlockSpec(memory_space=pl.ANY),
                  pl.BlockSpec(memory_space=pl.ANY)],
        out_specs=seq_spec(rows, r_kv),
        scratch_shapes=[pltpu.VMEM((2, bp, ppc, page, r_kv), f32),
                        pltpu.VMEM((2, bp, rope, ppc * page), f32),
                        pltpu.SemaphoreType.DMA((2, 2)),
                        pltpu.VMEM((bp, rows, 1), f32), pltpu.VMEM((bp, rows, 1), f32),
                        pltpu.VMEM((bp, rows, r_kv), f32)],
    )
    return pl.pallas_call(
        functools.partial(_decode_kernel, bp=bp, ppc=ppc, n_chunks=n_chunks, n_pages=n_pages, steps=steps,
                          c=scale * LOG2E),
        grid_spec=gs,
        out_shape=jax.ShapeDtypeStruct((nb, rows, r_kv), bf16),
        compiler_params=_params("arbitrary", "arbitrary"),
    )(page_table.reshape(-1), qlat, qpe, cnew, knew, cache_c, cache_k)


def _rope_tables(pos, rope):
    half = rope // 2
    inv_freq = ROPE_THETA ** (-jnp.arange(half, dtype=f32) / half)
    ang = pos.astype(f32)[:, None] * inv_freq
    cos, sin = jnp.cos(ang), jnp.sin(ang)
    pad = jnp.zeros((pos.shape[0], LANE - rope), f32)
    return (jnp.concatenate([cos, cos, pad], axis=-1), jnp.concatenate([-sin, sin, pad], axis=-1))


def kernel(x_prompt, x_sample, cache_ckv, cache_kpe, state_conv, state_rglru, page_table, c_prompt, c_sample, mod_w, mod_b, mix_norm, ffn_norm, rg_w_in, rg_conv_w, rg_conv_b, rg_w_gates, rg_b_gates, rg_lambda, rg_w_out, kv_mod_w, kv_mod_b, kv_norm, kv_w_dkv, kv_latent_norm, kv_w_uk, kv_w_uv, mla_w_dq, mla_q_norm, mla_w_uq, mla_w_o, moe_w_group, moe_b_group, moe_w_expert, moe_b_expert, moe_w13, moe_w2, final_norm):
    bp, seq, d = x_prompt.shape
    bs, steps, _ = x_sample.shape
    d_rnn = rg_conv_w.shape[-1]
    cwid = rg_conv_w.shape[1]
    nb_rnn, rnn_blk = rg_w_gates.shape[1], rg_w_gates.shape[2]
    r_kv, n_heads, d_nope = kv_w_uk.shape
    d_v = kv_w_uv.shape[2]
    rope = cache_kpe.shape[-1]
    half = rope // 2
    page = cache_ckv.shape[1]
    past = page_table.shape[1] * page
    n_groups = moe_w_group.shape[-1]
    n_exp = moe_w_expert.shape[-1]
    scale = 1.0 / math.sqrt(d_nope + rope)
    assert d_nope == LANE and d_v == LANE and rope <= LANE and d_rnn % LANE == 0
    assert rg_w_in.shape[0] == 1 and mla_w_dq.shape[0] == 1 and mod_w.shape[0] == 2
    assert steps >= cwid - 1 and seq >= cwid - 1

    n_p, n_s = bp * seq, steps * bs
    gp = _Group(n_p, min(512, seq), seq, False)
    gs_ = _Group(n_s, min(512, n_s), 0, True)
    groups = (gp, gs_)

    xp = x_prompt.reshape(n_p, d)
    xs = jnp.swapaxes(x_sample, 0, 1).reshape(n_s, d)

    n_c = bp + bs
    n_c_pad = -(-n_c // SUBLANE) * SUBLANE
    c_all = jnp.concatenate([c_prompt, c_sample, jnp.zeros((n_c_pad - n_c, d), f32)], axis=0)
    mods = [_bias_matmul(c_all, mod_w, layer, mod_b[layer]) for layer in range(2)]
    kv_mods = _bias_matmul(c_all, kv_mod_w[None], 0, kv_mod_b)

    def mod_vec(m, idx, grp_i):
        rows = slice(0, bp) if grp_i == 0 else slice(bp, bp + bs)
        return m[rows, idx * d:(idx + 1) * d]

    w_in = rg_w_in[0].astype(bf16)
    w_out = rg_w_out[0].astype(bf16)
    starts, win = _gate_windows(d_rnn, rnn_blk)
    wg = _gate_weights(rg_w_gates[0], starts, win)
    bgr = rg_b_gates[0][:, :rnn_blk].reshape(1, d_rnn)
    bgi = rg_b_gates[0][:, rnn_blk:].reshape(1, d_rnn)
    conv_w = rg_conv_w[0]
    conv_b = rg_conv_b[0].reshape(1, d_rnn)
    lam = rg_lambda[0].reshape(1, d_rnn)

    w_dkv = jnp.concatenate([kv_w_dkv, jnp.zeros((d, LANE - rope), f32)], axis=1).astype(bf16)
    w_up = jnp.concatenate([kv_w_uk.reshape(r_kv, n_heads * d_nope), kv_w_uv.reshape(r_kv, n_heads * d_v)],
                           axis=1).astype(bf16)
    w_dq = mla_w_dq[0].astype(bf16)
    wq3 = mla_w_uq[0].reshape(-1, n_heads, d_nope + rope)
    wq = jnp.concatenate([wq3, jnp.zeros((wq3.shape[0], n_heads, LANE - rope), f32)], axis=-1)
    wq = wq.reshape(-1, n_heads * 2 * LANE).astype(bf16)
    w_o = mla_w_o[0].astype(bf16)
    w_uk_t = jnp.transpose(kv_w_uk, (1, 2, 0))
    w_uv_flat = kv_w_uv.reshape(r_kv, n_heads * d_v)

    cos_p, sin_p = _rope_tables(jnp.arange(seq, dtype=jnp.int32), rope)
    cos_s, sin_s = _rope_tables(past + jnp.arange(steps, dtype=jnp.int32), rope)
    tabs = ((cos_p, sin_p), (cos_s, sin_s))

    def moe(xs_in, layer):
        rt_l = []
        n_pad = LANE - n_groups - n_exp
        wr = jnp.concatenate([moe_w_group[layer], moe_w_expert[layer], jnp.zeros((d, n_pad), f32)], axis=1)
        br = jnp.concatenate([moe_b_group[layer], moe_b_expert[layer], jnp.zeros((n_pad,), f32)]).reshape(1, LANE)
        xn_all, row0 = None, 0
        for gi, grp in enumerate(groups):
            xn_all, rt = _router(xs_in[gi], ffn_norm[layer], mod_vec(mods[layer], 3, gi),
                                 mod_vec(mods[layer], 4, gi), wr, br, grp, n_groups, n_exp,
                                 n_p + n_s, row0, xn_all)
            row0 += grp.n
            rt_l.append(rt)
        route = jnp.concatenate(rt_l, axis=0)
        expert_ids = route[:, :TOP_K].astype(jnp.int32)
        weights = route[:, TOP_K:2 * TOP_K]
        slot_tok, block_expert, n_used, next_expert, dest = _dispatch(expert_ids, n_exp)
        yb = _experts(xn_all, slot_tok, block_expert, n_used, next_expert, moe_w13, moe_w2, layer)
        out = []
        lo = 0
        for grp in groups:
            dg = dest[lo:lo + grp.n]
            out.append((yb[dg[:, 0]], yb[dg[:, 1]], weights[lo:lo + grp.n]))
            lo += grp.n
        return out

    x_cur = [xp, xs]
    projs, h_last = [], []
    for gi, grp in enumerate(groups):
        proj = _normmod_matmul(x_cur[gi], mix_norm[0], mod_vec(mods[0], 0, gi), mod_vec(mods[0], 1, gi),
                               w_in, grp, tn_cap=2688)
        projs.append(proj)
        if gi == 0:
            hg, hl = _rglru_prompt(proj, jnp.zeros((bp, 1, d_rnn), f32), jnp.zeros((bp, SUBLANE, d_rnn), f32),
                                   conv_w, conv_b, wg, bgr, bgi, lam, starts, win, bp, seq, min(256, seq))
            hl = hl.reshape(bp, d_rnn)
        else:
            cst = jnp.swapaxes(state_conv[0], 0, 1)
            hg, hl = _rglru_sample(proj.reshape(steps, bs, 2 * d_rnn), state_rglru[0], cst, conv_w, conv_b,
                                   wg, bgr, bgi, lam, starts, win)
            hg = hg.reshape(n_s, d_rnn)
        h_last.append(hl)
        x_cur[gi] = _matmul_res(hg, w_out, x_cur[gi], mod_vec(mods[0], 2, gi), grp)
    moe_out = moe(x_cur, 0)
    for gi, grp in enumerate(groups):
        ya, yc, wts = moe_out[gi]
        x_cur[gi] = _combine(x_cur[gi], ya, yc, wts, mod_vec(mods[0], 5, gi), grp)

    kv_out = []
    for gi, grp in enumerate(groups):
        kv_out.append(_kv_latent(x_cur[gi], kv_norm, kv_mods[:, :d][(slice(0, bp) if gi == 0 else slice(bp, bp + bs))],
                                 kv_mods[:, d:][(slice(0, bp) if gi == 0 else slice(bp, bp + bs))],
                                 w_dkv, kv_latent_norm, tabs[gi][0], tabs[gi][1], grp, half,
                                 w_up=w_up if gi == 0 else None))
    qs = [_q_proj(x_cur[gi], mix_norm[1], mod_vec(mods[1], 0, gi), mod_vec(mods[1], 1, gi), w_dq,
                  mla_q_norm[0], wq, tabs[gi][0], tabs[gi][1], grp, n_heads, half)
          for gi, grp in enumerate(groups)]

    ckv_p, kpe_p, kn_p, v_p, kpeb_p = kv_out[0]
    tq = min(ATTN_ROWS, seq)
    hgrp = 4 if n_heads % 4 == 0 else 1
    attn_p = _attn_prompt(qs[0], kn_p, v_p, kpeb_p, bp, seq, n_heads, scale, tq, hgrp)

    ckv_s, kpe_s = kv_out[1]
    rows = n_heads * steps
    qlat = _head_matmul(
        qs[1], pl.BlockSpec((n_s, LANE), lambda h: (0, 2 * h)),
        w_uk_t, pl.BlockSpec((None, d_nope, r_kv), lambda h: (h, 0, 0)),
        jax.ShapeDtypeStruct((n_heads, n_s, r_kv), bf16),
        pl.BlockSpec((None, n_s, r_kv), lambda h: (h, 0, 0)), n_heads)
    qlat = jnp.transpose(qlat.reshape(n_heads, steps, bs, r_kv), (2, 0, 1, 3)).reshape(bs, rows, r_kv)
    qpe = qs[1].reshape(steps, bs, n_heads, 2, LANE)[:, :, :, 1]
    qpe = jnp.transpose(qpe, (1, 2, 0, 3)).reshape(bs, rows, LANE)

    def new_keys(a):
        a = jnp.swapaxes(a.reshape(steps, bs, a.shape[-1]), 0, 1).astype(bf16)
        return jnp.concatenate([a, jnp.zeros((bs, LANE - steps, a.shape[-1]), bf16)], axis=1)

    ppc = _pick_tile(page_table.shape[1], DECODE_PAGES, 1)
    olat = _decode_attn(page_table, qlat, qpe, new_keys(ckv_s), new_keys(kpe_s), cache_ckv,
                        jnp.swapaxes(cache_kpe, 1, 2), scale, steps, ppc,
                        DECODE_BATCH if bs % DECODE_BATCH == 0 else 1)
    olat = jnp.transpose(olat.reshape(bs, n_heads, steps, r_kv), (1, 2, 0, 3)).reshape(n_heads, n_s, r_kv)
    attn_s = _head_matmul(
        olat, pl.BlockSpec((None, n_s, r_kv), lambda h: (h, 0, 0)),
        w_uv_flat, pl.BlockSpec((r_kv, d_v), lambda h: (0, h)),
        jax.ShapeDtypeStruct((n_s, n_heads * d_v), bf16),
        pl.BlockSpec((n_s, d_v), lambda h: (0, h)), n_heads)

    attn = [attn_p, attn_s]
    for gi, grp in enumerate(groups):
        x_cur[gi] = _matmul_res(attn[gi], w_o, x_cur[gi], mod_vec(mods[1], 2, gi), grp)
    moe_out = moe(x_cur, 1)
    ys = []
    for gi, grp in enumerate(groups):
        ya, yc, wts = moe_out[gi]
        ys.append(_combine(x_cur[gi], ya, yc, wts, mod_vec(mods[1], 5, gi), grp, final_gain=final_norm))

    y_prompt = ys[0].reshape(bp, seq, d)
    y_sample = jnp.swapaxes(ys[1].reshape(steps, bs, d), 0, 1)
    conv_p = projs[0].reshape(bp, seq, 2 * d_rnn)[:, seq - (cwid - 1):, d_rnn:][None]
    conv_s = jnp.swapaxes(projs[1].reshape(steps, bs, 2 * d_rnn)[steps - (cwid - 1):, :, d_rnn:], 0, 1)[None]
    h_p = h_last[0][None]
    h_s = h_last[1][None]
    ckv_prompt = ckv_p.reshape(bp, seq, r_kv)
    kpe_prompt = kpe_p[:, :rope].reshape(bp, seq, rope)
    ckv_sample = jnp.swapaxes(ckv_s.reshape(steps, bs, r_kv), 0, 1)
    kpe_sample = jnp.swapaxes(kpe_s[:, :rope].reshape(steps, bs, rope), 0, 1)
    return (y_prompt, y_sample, conv_p, h_p, ckv_prompt, kpe_prompt, conv_s, h_s, ckv_sample, kpe_sample)
```

```python
import functools
import math
from typing import NamedTuple

import jax
import jax.numpy as jnp
from jax import lax
from jax.experimental import pallas as pl
from jax.experimental.pallas import tpu as pltpu

EPS = 1e-6
LRU_C = 8.0
ROPE_THETA = 10000.0
TOP_K = 2
LANE = 128
SUBLANE = 8
VMEM_LIMIT = 56 * 1024 * 1024
MOE_ROWS = 128
MOE_GROUP = 32
ATTN_ROWS = 512
SAMPLE_ROWS = 64
DECODE_PAGES = 32
DECODE_BATCH = 1
LOG2E = 1.4426950408889634

f32 = jnp.float32
bf16 = jnp.bfloat16


def _params(*sem):
    return pltpu.CompilerParams(dimension_semantics=sem, vmem_limit_bytes=VMEM_LIMIT)


def _pick_tile(n, cap, mult=LANE):
    if n <= cap:
        return n
    best = None
    for t in range(mult, cap + 1, mult):
        if n % t == 0:
            best = t
    assert best is not None, (n, cap, mult)
    return best


class _Group(NamedTuple):
    n: int
    tm: int
    seq: int
    per_token: bool


def _mod_operand(vec, grp):
    d = vec.shape[-1]
    if grp.per_token:
        arr = jnp.tile(vec, (grp.n // vec.shape[0], 1))
        return arr, pl.BlockSpec((grp.tm, d), lambda i, *_: (i, 0))
    per_seq = grp.seq // grp.tm
    return vec[:, None, :], pl.BlockSpec((None, 1, d), lambda i, *_: (i // per_seq, 0, 0))


def _pos_operand(tab, grp):
    if grp.per_token:
        arr = jnp.repeat(tab, grp.n // tab.shape[0], axis=0)
        return arr, pl.BlockSpec((grp.tm, LANE), lambda i, *_: (i, 0))
    per_seq = grp.seq // grp.tm
    return tab, pl.BlockSpec((grp.tm, LANE), lambda i, *_: (i % per_seq, 0))


def _row_spec(grp, width):
    return pl.BlockSpec((grp.tm, width), lambda i, *_: (i, 0))


def _full_spec(shape):
    nd = len(shape)
    return pl.BlockSpec(shape, lambda *_: (0,) * nd)


def _rms(x, gain):
    return x * lax.rsqrt(jnp.mean(x * x, axis=-1, keepdims=True) + EPS) * gain


def _normmod(x, gain, shift, scale):
    return _rms(x, gain) * (1.0 + scale) + shift


def _rope_tile(t, cos, sin, half):
    lane = lax.broadcasted_iota(jnp.int32, t.shape, 1)
    rot = jnp.where(lane < half, pltpu.roll(t, LANE - half, 1), pltpu.roll(t, half, 1))
    return t * cos + rot * sin


def _sigmoid(x):
    return 0.5 * (jnp.tanh(0.5 * x) + 1.0)


def _gelu(x):
    return 0.5 * x * (1.0 + jnp.tanh(math.sqrt(2.0 / math.pi) * (x + 0.044715 * (x * x * x))))


def _softplus(z):
    return jnp.maximum(z, 0.0) + jnp.log1p(jnp.exp(-jnp.abs(z)))


def _bias_matmul_kernel(a_ref, w_ref, b_ref, o_ref):
    o_ref[...] = jnp.dot(a_ref[...].astype(bf16), w_ref[...].astype(bf16),
                         preferred_element_type=f32) + b_ref[...]


def _bias_matmul(a, w3, layer, b):
    m, k = a.shape
    n = w3.shape[-1]
    tn = _pick_tile(n, 1024)
    return pl.pallas_call(
        _bias_matmul_kernel,
        grid=(n // tn,),
        in_specs=[_full_spec((m, k)),
                  pl.BlockSpec((None, k, tn), lambda j: (layer, 0, j)),
                  pl.BlockSpec((1, tn), lambda j: (0, j))],
        out_specs=pl.BlockSpec((m, tn), lambda j: (0, j)),
        out_shape=jax.ShapeDtypeStruct((m, n), f32),
        compiler_params=_params("arbitrary"),
    )(a, w3, b.reshape(1, n))


def _normmod_matmul_kernel(x_ref, gain_ref, shift_ref, scale_ref, w_ref, o_ref, xn_ref):
    @pl.when(pl.program_id(1) == 0)
    def _():
        xn_ref[...] = _normmod(x_ref[...], gain_ref[...], shift_ref[...], scale_ref[...]).astype(bf16)

    o_ref[...] = jnp.dot(xn_ref[...], w_ref[...], preferred_element_type=f32).astype(o_ref.dtype)


def _normmod_matmul(x, gain, shift, scale, w, grp, tn_cap=1024, out_dtype=f32):
    d = x.shape[1]
    n = w.shape[1]
    tn = _pick_tile(n, tn_cap)
    sh, sh_spec = _mod_operand(shift, grp)
    sc, sc_spec = _mod_operand(scale, grp)
    return pl.pallas_call(
        _normmod_matmul_kernel,
        grid=(grp.n // grp.tm, n // tn),
        in_specs=[_row_spec(grp, d), _full_spec((1, d)), sh_spec, sc_spec,
                  pl.BlockSpec((d, tn), lambda i, j: (0, j))],
        out_specs=pl.BlockSpec((grp.tm, tn), lambda i, j: (i, j)),
        out_shape=jax.ShapeDtypeStruct((grp.n, n), out_dtype),
        scratch_shapes=[pltpu.VMEM((grp.tm, d), bf16)],
        compiler_params=_params("arbitrary", "arbitrary"),
    )(x, gain.reshape(1, d), sh, sc, w)


def _matmul_res_kernel(a_ref, w_ref, res_ref, gate_ref, o_ref):
    y = jnp.dot(a_ref[...].astype(bf16), w_ref[...], preferred_element_type=f32)
    o_ref[...] = res_ref[...] + gate_ref[...] * y


def _matmul_res(a, w, res, gate, grp, tn_cap=1024):
    k = a.shape[1]
    n = w.shape[1]
    tn = _pick_tile(n, tn_cap)
    if grp.per_token:
        g_arr = jnp.tile(gate, (grp.n // gate.shape[0], 1))
        g_spec = pl.BlockSpec((grp.tm, tn), lambda j, i: (i, j))
    else:
        per_seq = grp.seq // grp.tm
        g_arr = gate[:, None, :]
        g_spec = pl.BlockSpec((None, 1, tn), lambda j, i: (i // per_seq, 0, j))
    return pl.pallas_call(
        _matmul_res_kernel,
        grid=(n // tn, grp.n // grp.tm),
        in_specs=[pl.BlockSpec((grp.tm, k), lambda j, i: (i, 0)),
                  pl.BlockSpec((k, tn), lambda j, i: (0, j)),
                  pl.BlockSpec((grp.tm, tn), lambda j, i: (i, j)),
                  g_spec],
        out_specs=pl.BlockSpec((grp.tm, tn), lambda j, i: (i, j)),
        out_shape=jax.ShapeDtypeStruct((grp.n, n), f32),
        compiler_params=_params("arbitrary", "arbitrary"),
    )(a, w, res, g_arr)


def _gate_windows(d_rnn, blk):
    nt = d_rnn // LANE
    raw, need = [], 0
    for j in range(nt):
        n0 = (LANE * j) // blk
        n1 = (LANE * j + LANE - 1) // blk
        s = (blk * n0) // LANE * LANE
        raw.append(s)
        need = max(need, blk * (n1 + 1) - s)
    win = min(d_rnn, -(-need // LANE) * LANE)
    return [min(s, d_rnn - win) for s in raw], win


def _gate_weights(w_gates, starts, win):
    nb, blk, _ = w_gates.shape
    wb = w_gates.astype(bf16)
    tiles = []
    for j, s in enumerate(starts):
        halves = []
        for off in (0, blk):
            acc = None
            for n in range((LANE * j) // blk, (LANE * j + LANE - 1) // blk + 1):
                c_lo = max(blk * n, LANE * j)
                c_hi = min(blk * (n + 1), LANE * (j + 1))
                r_off = blk * n - s
                assert 0 <= r_off and r_off + blk <= win
                piece = wb[n, :, off + c_lo - blk * n:off + c_hi - blk * n]
                piece = jnp.pad(piece, ((r_off, win - blk - r_off), (c_lo - LANE * j, LANE * (j + 1) - c_hi)))
                acc = piece if acc is None else acc + piece
            halves.append(acc)
        tiles.append(jnp.concatenate(halves, axis=1))
    return jnp.stack(tiles)


def _lru_inputs(g, b_r, b_i, neg_c_sp, u):
    r = _sigmoid(g[:, :LANE] + b_r)
    i = _sigmoid(g[:, LANE:] + b_i)
    log_a = neg_c_sp * r
    a = jnp.exp(log_a)
    x = jnp.sqrt(-jnp.tanh(log_a) * (a * a + 1.0)) * (i * u)
    return a, x


def _rglru_prompt_kernel(proj_ref, h0_ref, cinit_ref, cw_ref, cb_ref, wg_ref, bgr_ref, bgi_ref, lam_ref,
                         hg_ref, hlast_ref, ubuf, ucf, ucb, a_scr, x_scr, hcar,
                         *, starts, win, tc, d_rnn, cwid):
    t = pl.program_id(1)
    hist = SUBLANE

    @pl.when(t == 0)
    def _():
        ubuf[0:hist, :] = cinit_ref[...]
        hcar[...] = h0_ref[...]

    ubuf[hist:hist + tc, :] = proj_ref[:, d_rnn:]
    uc = cb_ref[...]
    for k in range(cwid):
        off = hist - (cwid - 1) + k
        uc = uc + cw_ref[k:k + 1, :] * ubuf[off:off + tc, :]
    ucf[...] = uc
    ucb[...] = uc.astype(bf16)
    ubuf[0:hist, :] = ubuf[tc:tc + hist, :]

    neg_c_sp = -LRU_C * _softplus(-lam_ref[...])
    seg = tc // SUBLANE
    for j in range(d_rnn // LANE):
        cs = slice(j * LANE, (j + 1) * LANE)
        g = jnp.dot(ucb[:, starts[j]:starts[j] + win], wg_ref[j], preferred_element_type=f32)
        a, x = _lru_inputs(g, bgr_ref[:, cs], bgi_ref[:, cs], neg_c_sp[:, cs], ucf[:, cs])
        a_scr[j] = a
        x_scr[j] = x
        h = jnp.zeros((SUBLANE, LANE), f32)
        p = jnp.ones((SUBLANE, LANE), f32)
        for k in range(seg):
            ak = a_scr[j, pl.ds(k, SUBLANE, stride=seg), :]
            xk = x_scr[j, pl.ds(k, SUBLANE, stride=seg), :]
            h = ak * h + xk
            p = p * ak
            x_scr[j, pl.ds(k, SUBLANE, stride=seg), :] = h
            a_scr[j, pl.ds(k, SUBLANE, stride=seg), :] = p
        c = hcar[:, cs]
        outs = []
        for r in range(SUBLANE):
            rows = slice(r * seg, (r + 1) * seg)
            outs.append(x_scr[j, rows, :] + a_scr[j, rows, :] * c)
            c = p[r:r + 1, :] * c + h[r:r + 1, :]
        hcar[:, cs] = c
        hs = jnp.concatenate(outs, axis=0)
        hg_ref[:, cs] = (hs * _gelu(proj_ref[:, cs])).astype(hg_ref.dtype)
    hlast_ref[...] = hcar[...]


def _rglru_prompt(proj, h0, conv_init, conv_w, conv_b, wg, bgr, bgi, lam, starts, win, nb_seq, seq, tc):
    d_rnn = proj.shape[1] // 2
    cwid = conv_w.shape[0]
    nt = d_rnn // LANE
    per_seq = seq // tc
    kern = functools.partial(_rglru_prompt_kernel, starts=tuple(starts), win=win, tc=tc, d_rnn=d_rnn, cwid=cwid)
    return pl.pallas_call(
        kern,
        grid=(nb_seq, per_seq),
        in_specs=[pl.BlockSpec((tc, 2 * d_rnn), lambda b, t: (b * per_seq + t, 0)),
                  pl.BlockSpec((None, 1, d_rnn), lambda b, t: (b, 0, 0)),
                  pl.BlockSpec((None, SUBLANE, d_rnn), lambda b, t: (b, 0, 0)),
                  _full_spec((cwid, d_rnn)), _full_spec((1, d_rnn)),
                  _full_spec(wg.shape), _full_spec((1, d_rnn)), _full_spec((1, d_rnn)), _full_spec((1, d_rnn))],
        out_specs=[pl.BlockSpec((tc, d_rnn), lambda b, t: (b * per_seq + t, 0)),
                   pl.BlockSpec((None, 1, d_rnn), lambda b, t: (b, 0, 0))],
        out_shape=[jax.ShapeDtypeStruct((nb_seq * seq, d_rnn), bf16),
                   jax.ShapeDtypeStruct((nb_seq, 1, d_rnn), f32)],
        scratch_shapes=[pltpu.VMEM((tc + SUBLANE, d_rnn), f32),
                        pltpu.VMEM((tc, d_rnn), f32),
                        pltpu.VMEM((tc, d_rnn), bf16),
                        pltpu.VMEM((nt, tc, LANE), f32),
                        pltpu.VMEM((nt, tc, LANE), f32),
                        pltpu.VMEM((1, d_rnn), f32)],
        compiler_params=_params("arbitrary", "arbitrary"),
    )(proj, h0, conv_init, conv_w, conv_b, wg, bgr, bgi, lam)


def _rglru_sample_kernel(proj_ref, h0_ref, cst_ref, cw_ref, cb_ref, wg_ref, bgr_ref, bgi_ref, lam_ref,
                         hg_ref, hlast_ref, ucf, ucb, *, starts, win, steps, tb, d_rnn, cwid):
    def up(tp):
        if tp < cwid - 1:
            return cst_ref[tp]
        return proj_ref[tp - cwid + 1, :, d_rnn:]

    for t in range(steps):
        uc = cb_ref[...]
        for k in range(cwid):
            uc = uc + cw_ref[k:k + 1, :] * up(t + k)
        ucf[t * tb:(t + 1) * tb, :] = uc
        ucb[t * tb:(t + 1) * tb, :] = uc.astype(bf16)

    neg_c_sp = -LRU_C * _softplus(-lam_ref[...])
    for j in range(d_rnn // LANE):
        cs = slice(j * LANE, (j + 1) * LANE)
        g = jnp.dot(ucb[:, starts[j]:starts[j] + win], wg_ref[j], preferred_element_type=f32)
        h = h0_ref[:, cs]
        for t in range(steps):
            rows = slice(t * tb, (t + 1) * tb)
            a, x = _lru_inputs(g[rows], bgr_ref[:, cs], bgi_ref[:, cs], neg_c_sp[:, cs], ucf[rows, cs])
            h = a * h + x
            hg_ref[t, :, cs] = (h * _gelu(proj_ref[t, :, cs])).astype(hg_ref.dtype)
        hlast_ref[:, cs] = h


def _rglru_sample(proj, h0, conv_state, conv_w, conv_b, wg, bgr, bgi, lam, starts, win):
    steps, nb, _ = proj.shape
    d_rnn = proj.shape[2] // 2
    cwid = conv_w.shape[0]
    tb = min(SAMPLE_ROWS, nb)
    kern = functools.partial(_rglru_sample_kernel, starts=tuple(starts), win=win, steps=steps, tb=tb,
                             d_rnn=d_rnn, cwid=cwid)
    return pl.pallas_call(
        kern,
        grid=(nb // tb,),
        in_specs=[pl.BlockSpec((steps, tb, 2 * d_rnn), lambda i: (0, i, 0)),
                  pl.BlockSpec((tb, d_rnn), lambda i: (i, 0)),
                  pl.BlockSpec((cwid - 1, tb, d_rnn), lambda i: (0, i, 0)),
                  _full_spec((cwid, d_rnn)), _full_spec((1, d_rnn)),
                  _full_spec(wg.shape), _full_spec((1, d_rnn)), _full_spec((1, d_rnn)), _full_spec((1, d_rnn))],
        out_specs=[pl.BlockSpec((steps, tb, d_rnn), lambda i: (0, i, 0)),
                   pl.BlockSpec((tb, d_rnn), lambda i: (i, 0))],
        out_shape=[jax.ShapeDtypeStruct((steps, nb, d_rnn), bf16),
                   jax.ShapeDtypeStruct((nb, d_rnn), f32)],
        scratch_shapes=[pltpu.VMEM((steps * tb, d_rnn), f32), pltpu.VMEM((steps * tb, d_rnn), bf16)],
        compiler_params=_params("arbitrary"),
    )(proj, h0, conv_state, conv_w, conv_b, wg, bgr, bgi, lam)


def _first_max(vals, ids):
    m = jnp.max(vals, axis=-1, keepdims=True)
    idx = jnp.min(jnp.where(vals == m, ids, jnp.int32(2 ** 30)), axis=-1, keepdims=True)
    return m, idx


def _router_kernel(x_ref, gain_ref, shift_ref, scale_ref, wh_ref, wl_ref, b_ref, *rest, n_groups, n_exp):
    xn_ref, rt_ref = rest[-2:]
    xn = _normmod(x_ref[...], gain_ref[...], shift_ref[...], scale_ref[...])
    for s in range(xn_ref.shape[1]):
        xn_ref[:, s, :] = xn[:, s * LANE:(s + 1) * LANE]
    xh = xn.astype(bf16)
    xl = (xn - xh.astype(f32)).astype(bf16)
    lg = (jnp.dot(xh, wh_ref[...], preferred_element_type=f32)
          + jnp.dot(xl, wh_ref[...], preferred_element_type=f32)
          + jnp.dot(xh, wl_ref[...], preferred_element_type=f32)
          + jnp.dot(xl, wl_ref[...], preferred_element_type=f32)) + b_ref[...]
    per = n_exp // n_groups
    lane = lax.broadcasted_iota(jnp.int32, lg.shape, 1)
    neg = -jnp.inf
    gl = jnp.where(lane < n_groups, lg, neg)
    gmax, g_idx = _first_max(gl, lane)
    g_w = 1.0 / jnp.sum(jnp.exp(gl - gmax), axis=-1, keepdims=True)
    e_id = lane - n_groups
    lo = g_idx * per
    el = jnp.where(e_id >= lo, jnp.where(e_id < lo + per, lg, neg), neg)
    l1, i1 = _first_max(el, e_id)
    el2 = jnp.where(e_id == i1, neg, el)
    l2, i2 = _first_max(el2, e_id)
    e = jnp.exp(l2 - l1)
    w1 = g_w / (1.0 + e)
    w2 = w1 * e
    rt_ref[...] = jnp.where(lane == 0, i1.astype(f32),
                            jnp.where(lane == 1, i2.astype(f32),
                                      jnp.where(lane == 2, w1, jnp.where(lane == 3, w2, 0.0))))


def _router(x, gain, shift, scale, wr, br, grp, n_groups, n_exp, n_total, row0, xn_all=None):
    d = x.shape[1]
    nr = wr.shape[1]
    wh = wr.astype(bf16)
    wl = (wr - wh.astype(f32)).astype(bf16)
    sh, sh_spec = _mod_operand(shift, grp)
    sc, sc_spec = _mod_operand(scale, grp)
    assert row0 % grp.tm == 0 and n_total % grp.tm == 0
    t0 = row0 // grp.tm
    n_tiles = grp.n // grp.tm
    n_steps = n_tiles if xn_all is not None else n_total // grp.tm

    def clamp(spec):
        return pl.BlockSpec(spec.block_shape, lambda i: spec.index_map(jnp.minimum(i, n_tiles - 1)))

    args = [x, gain.reshape(1, d), sh, sc, wh, wl, br]
    specs = [clamp(_row_spec(grp, d)), _full_spec((1, d)), clamp(sh_spec), clamp(sc_spec),
             _full_spec((d, nr)), _full_spec((d, nr)), _full_spec((1, nr))]
    aliases = {}
    if xn_all is not None:
        aliases = {len(args): 0}
        args.append(xn_all)
        specs.append(pl.BlockSpec(memory_space=pl.ANY))
    return pl.pallas_call(
        functools.partial(_router_kernel, n_groups=n_groups, n_exp=n_exp),
        grid=(n_steps,),
        in_specs=specs,
        out_specs=[pl.BlockSpec((grp.tm, d // LANE, LANE), lambda i: (i + t0, 0, 0)),
                   clamp(_row_spec(grp, nr))],
        out_shape=[jax.ShapeDtypeStruct((n_total, d // LANE, LANE), f32),
                   jax.ShapeDtypeStruct((grp.n, nr), f32)],
        input_output_aliases=aliases,
        compiler_params=_params("arbitrary"),
    )(*args)


def _experts_kernel(be_ref, nu_ref, st_ref, nx_ref, bc_ref, xn_hbm, w13_hbm, w2_hbm, o_ref,
                    xbuf, sem, w13f, w2f, wsem, w13b, w2b, *, d_exp, layer):
    i = pl.program_id(0)
    n_used = nu_ref[0]
    slot = i % 2
    e = be_ref[i]
    prev = be_ref[jnp.maximum(i - 1, 0)]
    n_slab = xbuf.shape[2]

    def row_copy(tok, sl, r):
        return pltpu.make_async_copy(xn_hbm.at[tok], xbuf.at[sl, r], sem.at[sl])

    def row_groups(blk, fn):
        for g0 in range(0, MOE_ROWS, MOE_GROUP):
            @pl.when(g0 < bc_ref[blk])
            def _(g0=g0):
                for r in range(g0, g0 + MOE_GROUP):
                    fn(r)

    def gather(blk, sl):
        row_groups(blk, lambda r: row_copy(st_ref[blk * MOE_ROWS + r], sl, r).start())

    def weight_copies(ex):
        return (pltpu.make_async_copy(w13_hbm.at[layer, ex], w13f, wsem.at[0]),
                pltpu.make_async_copy(w2_hbm.at[layer, ex], w2f, wsem.at[1]))

    @pl.when((i == 0) & (n_used > 0))
    def _():
        xbuf[...] = jnp.zeros_like(xbuf)
        gather(0, 0)
        for cp in weight_copies(e):
            cp.start(priority=1)

    @pl.when(i + 1 < n_used)
    def _():
        gather(i + 1, 1 - slot)

    @pl.when((i < n_used) & ((i == 0) | (e != prev)))
    def _():
        for cp in weight_copies(e):
            cp.wait()
        w13b[...] = w13f[...].astype(bf16)
        w2b[...] = w2f[...].astype(bf16)

        @pl.when(nx_ref[i] >= 0)
        def _():
            for cp in weight_copies(nx_ref[i]):
                cp.start(priority=1)

    @pl.when(i < n_used)
    def _():
        row_groups(i, lambda r: row_copy(0, slot, r).wait())
        x = jnp.concatenate([xbuf[slot, :, s, :] for s in range(n_slab)], axis=-1).astype(bf16)
        gu = jnp.dot(x, w13b[...], preferred_element_type=f32)
        g = gu[:, :d_exp]
        act = (g * _sigmoid(g)) * gu[:, d_exp:]
        o_ref[...] = jnp.dot(act.astype(bf16), w2b[...], preferred_element_type=f32)

    @pl.when(i >= n_used)
    def _():
        o_ref[...] = jnp.zeros_like(o_ref)


def _experts(xn, slot_tok, block_expert, n_used, next_expert, block_rows, w13, w2, layer):
    n_slab = xn.shape[1]
    d = n_slab * LANE
    n_blocks = slot_tok.shape[0] // MOE_ROWS
    d_exp = w2.shape[2]
    gs = pltpu.PrefetchScalarGridSpec(
        num_scalar_prefetch=5,
        grid=(n_blocks,),
        in_specs=[pl.BlockSpec(memory_space=pl.ANY), pl.BlockSpec(memory_space=pl.ANY),
                  pl.BlockSpec(memory_space=pl.ANY)],
        out_specs=pl.BlockSpec((MOE_ROWS, d), lambda i, *_: (i, 0)),
        scratch_shapes=[pltpu.VMEM((2, MOE_ROWS, n_slab, LANE), f32), pltpu.SemaphoreType.DMA((2,)),
                        pltpu.VMEM((d, 2 * d_exp), f32), pltpu.VMEM((d_exp, d), f32),
                        pltpu.SemaphoreType.DMA((2,)),
                        pltpu.VMEM((d, 2 * d_exp), bf16), pltpu.VMEM((d_exp, d), bf16)],
    )
    return pl.pallas_call(
        functools.partial(_experts_kernel, d_exp=d_exp, layer=layer),
        grid_spec=gs,
        out_shape=jax.ShapeDtypeStruct((n_blocks * MOE_ROWS, d), f32),
        compiler_params=_params("arbitrary"),
    )(block_expert, n_used, slot_tok, next_expert, block_rows, xn, w13, w2)


def _combine_kernel(x_ref, ya_ref, yb_ref, wt_ref, gate_ref, *rest, final):
    if final:
        fg_ref, o_ref = rest
    else:
        (o_ref,) = rest
    wt = wt_ref[...]
    y = x_ref[...] + gate_ref[...] * (wt[:, 0:1] * ya_ref[...] + wt[:, 1:2] * yb_ref[...])
    if final:
        y = _rms(y, fg_ref[...])
    o_ref[...] = y


def _combine(x, ya, yb, wts, gate, grp, final_gain=None):
    d = x.shape[1]
    g_arr, g_spec = _mod_operand(gate, grp)
    final = final_gain is not None
    args = [x, ya, yb, wts, g_arr]
    specs = [_row_spec(grp, d), _row_spec(grp, d), _row_spec(grp, d), _row_spec(grp, TOP_K), g_spec]
    if final:
        args.append(final_gain.reshape(1, d))
        specs.append(_full_spec((1, d)))
    return pl.pallas_call(
        functools.partial(_combine_kernel, final=final),
        grid=(grp.n // grp.tm,),
        in_specs=specs,
        out_specs=_row_spec(grp, d),
        out_shape=jax.ShapeDtypeStruct((grp.n, d), f32),
        compiler_params=_params("arbitrary"),
    )(*args)


def _dispatch(expert_ids, n_exp):
    n = expert_ids.shape[0]
    na = n * TOP_K
    flat_e = expert_ids.reshape(na)
    onehot = (flat_e[:, None] == jnp.arange(n_exp, dtype=jnp.int32)[None, :]).astype(jnp.int32)
    incl = jnp.cumsum(onehot, axis=0)
    counts = incl[-1]
    padded = (counts + MOE_ROWS - 1) // MOE_ROWS * MOE_ROWS
    pad_end = jnp.cumsum(padded)
    pad_start = pad_end - padded
    dest = jnp.sum(onehot * (pad_start[None, :] + incl - 1), axis=1)
    n_blocks = (na + MOE_ROWS - 1) // MOE_ROWS + n_exp
    tok = jnp.arange(na, dtype=jnp.int32) // TOP_K
    slot_tok = jnp.zeros((n_blocks * MOE_ROWS,), jnp.int32).at[dest].set(tok)
    block_start = jnp.arange(n_blocks, dtype=jnp.int32) * MOE_ROWS
    block_expert = jnp.minimum(jnp.sum((pad_end[None, :] <= block_start[:, None]).astype(jnp.int32), axis=1),
                               n_exp - 1)
    n_used = pad_end[-1:] // MOE_ROWS
    bidx = jnp.arange(n_blocks, dtype=jnp.int32)
    is_first = (bidx == 0) | (block_expert != jnp.roll(block_expert, 1))
    pos = jnp.where(is_first & (bidx < n_used[0]), bidx, n_blocks)
    nxt = jnp.min(jnp.where(bidx[None, :] > bidx[:, None], pos[None, :], n_blocks), axis=1)
    next_expert = jnp.where(nxt < n_blocks, block_expert[jnp.minimum(nxt, n_blocks - 1)], -1)
    sel = (block_expert[:, None] == jnp.arange(n_exp, dtype=jnp.int32)[None, :]).astype(jnp.int32)
    real_end = jnp.sum(sel * (pad_start + counts)[None, :], axis=1)
    block_rows = jnp.clip(real_end - block_start, 0, MOE_ROWS)
    return slot_tok, block_expert, n_used, next_expert, block_rows, dest.reshape(n, TOP_K)


def _kv_kernel(x_ref, gain_ref, shift_ref, scale_ref, w_ref, lg_ref, cos_ref, sin_ref, *rest,
               r_kv, half, with_up):
    if with_up:
        wup_ref, ckv_ref, kpe_ref, kn_ref, v_ref, kpeb_ref = rest
    else:
        ckv_ref, kpe_ref = rest
    a = _normmod(x_ref[...], gain_ref[...], shift_ref[...], scale_ref[...]).astype(bf16)
    kv = jnp.dot(a, w_ref[...], preferred_element_type=f32)
    c = _rms(kv[:, :r_kv], lg_ref[...])
    ckv_ref[...] = c
    kp = _rope_tile(kv[:, r_kv:], cos_ref[...], sin_ref[...], half)
    kpe_ref[...] = kp
    if with_up:
        up = jnp.dot(c.astype(bf16), wup_ref[...], preferred_element_type=f32)
        hw = up.shape[1] // 2
        kn_ref[...] = up[:, :hw].astype(bf16)
        v_ref[...] = up[:, hw:].astype(bf16)
        kpeb_ref[...] = kp.astype(bf16)


def _kv_latent(x, gain, shift, scale, w_pad, latent_gain, cos, sin, grp, half, w_up=None):
    d = x.shape[1]
    r_kv = latent_gain.shape[0]
    with_up = w_up is not None
    sh, sh_spec = _mod_operand(shift, grp)
    sc, sc_spec = _mod_operand(scale, grp)
    cs, cs_spec = _pos_operand(cos, grp)
    sn, sn_spec = _pos_operand(sin, grp)
    args = [x, gain.reshape(1, d), sh, sc, w_pad, latent_gain.reshape(1, r_kv), cs, sn]
    specs = [_row_spec(grp, d), _full_spec((1, d)), sh_spec, sc_spec, _full_spec(w_pad.shape),
             _full_spec((1, r_kv)), cs_spec, sn_spec]
    out_specs = [_row_spec(grp, r_kv), _row_spec(grp, LANE)]
    out_shape = [jax.ShapeDtypeStruct((grp.n, r_kv), f32), jax.ShapeDtypeStruct((grp.n, LANE), f32)]
    if with_up:
        hw = w_up.shape[1] // 2
        args.append(w_up)
        specs.append(_full_spec(w_up.shape))
        out_specs += [_row_spec(grp, hw), _row_spec(grp, hw), _row_spec(grp, LANE)]
        out_shape += [jax.ShapeDtypeStruct((grp.n, hw), bf16), jax.ShapeDtypeStruct((grp.n, hw), bf16),
                      jax.ShapeDtypeStruct((grp.n, LANE), bf16)]
    return pl.pallas_call(
        functools.partial(_kv_kernel, r_kv=r_kv, half=half, with_up=with_up),
        grid=(grp.n // grp.tm,),
        in_specs=specs, out_specs=out_specs, out_shape=out_shape,
        compiler_params=_params("arbitrary"),
    )(*args)


def _q_kernel(x_ref, gain_ref, shift_ref, scale_ref, wdq_ref, qg_ref, wq_ref, cos_ref, sin_ref, q_ref,
              *, n_heads, half):
    a = _normmod(x_ref[...], gain_ref[...], shift_ref[...], scale_ref[...]).astype(bf16)
    ql = jnp.dot(a, wdq_ref[...], preferred_element_type=f32)
    qn = _rms(ql, qg_ref[...]).astype(bf16)
    cos = cos_ref[...]
    sin = sin_ref[...]
    for h in range(n_heads):
        q = jnp.dot(qn, wq_ref[:, 2 * LANE * h:2 * LANE * (h + 1)], preferred_element_type=f32)
        q_ref[:, 2 * LANE * h:2 * LANE * h + LANE] = q[:, :LANE].astype(bf16)
        q_ref[:, 2 * LANE * h + LANE:2 * LANE * (h + 1)] = _rope_tile(q[:, LANE:], cos, sin, half).astype(bf16)


def _q_proj(x, gain, shift, scale, w_dq, q_gain, wq, cos, sin, grp, n_heads, half):
    d = x.shape[1]
    rq = w_dq.shape[1]
    sh, sh_spec = _mod_operand(shift, grp)
    sc, sc_spec = _mod_operand(scale, grp)
    cs, cs_spec = _pos_operand(cos, grp)
    sn, sn_spec = _pos_operand(sin, grp)
    return pl.pallas_call(
        functools.partial(_q_kernel, n_heads=n_heads, half=half),
        grid=(grp.n // grp.tm,),
        in_specs=[_row_spec(grp, d), _full_spec((1, d)), sh_spec, sc_spec, _full_spec(w_dq.shape),
                  _full_spec((1, rq)), _full_spec(wq.shape), cs_spec, sn_spec],
        out_specs=_row_spec(grp, wq.shape[1]),
        out_shape=jax.ShapeDtypeStruct((grp.n, wq.shape[1]), bf16),
        compiler_params=_params("arbitrary"),
    )(x, gain.reshape(1, d), sh, sc, w_dq, q_gain.reshape(1, rq), wq, cs, sn)


def _attn_kernel(q_ref, kn_ref, v_ref, kpe_ref, o_ref, m_scr, acc_scr, *, hg, tq, c):
    qi = pl.program_id(2)
    m_scr[...] = jnp.full_like(m_scr, -jnp.inf)
    acc_scr[...] = jnp.zeros_like(acc_scr)
    ones = jnp.ones((tq, LANE), bf16)
    n_lt = tq // LANE

    def block(ks, masked):
        kpe = kpe_ref[pl.ds(ks, tq), :]
        for h in range(hg):
            hs = slice(h * LANE, (h + 1) * LANE)
            k = jnp.concatenate([kn_ref[pl.ds(ks, tq), hs], kpe], axis=-1)
            s = lax.dot_general(q_ref[:, 2 * LANE * h:2 * LANE * (h + 1)], k, (((1,), (1,)), ((), ())),
                                preferred_element_type=f32)
            if masked:
                row = lax.broadcasted_iota(jnp.int32, (tq, tq), 0)
                col = lax.broadcasted_iota(jnp.int32, (tq, tq), 1)
                s = jnp.where(col <= row, s, -jnp.inf)
            tiles = [s[:, t * LANE:(t + 1) * LANE] for t in range(n_lt)]
            mx = tiles[0]
            for t in tiles[1:]:
                mx = jnp.maximum(mx, t)
            m = m_scr[h]
            m_new = jnp.maximum(m, jnp.max(mx, axis=-1, keepdims=True))
            p = jnp.concatenate([jnp.exp2((t - m_new) * c) for t in tiles], axis=-1).astype(bf16)
            corr = jnp.exp2((m - m_new) * c)
            v_aug = jnp.concatenate([v_ref[pl.ds(ks, tq), hs], ones], axis=-1)
            pv = jnp.dot(p, v_aug, preferred_element_type=f32)
            acc_scr[h, :, :LANE] = acc_scr[h, :, :LANE] * corr + pv[:, :LANE]
            acc_scr[h, :, LANE:] = acc_scr[h, :, LANE:] * corr + pv[:, LANE:]
            m_scr[h] = m_new

    def body(kb, carry):
        block(pl.multiple_of(kb * tq, tq), False)
        return carry

    lax.fori_loop(0, qi, body, 0)
    block(pl.multiple_of(qi * tq, tq), True)
    for h in range(hg):
        o_ref[:, h * LANE:(h + 1) * LANE] = (acc_scr[h, :, :LANE] * (1.0 / acc_scr[h, :, LANE:])).astype(o_ref.dtype)


def _attn_prompt(q, kn, v, kpe, nb_seq, seq, n_heads, scale, tq, hg):
    nq = seq // tq
    return pl.pallas_call(
        functools.partial(_attn_kernel, hg=hg, tq=tq, c=scale * LOG2E),
        grid=(nb_seq, n_heads // hg, nq),
        in_specs=[pl.BlockSpec((tq, hg * 2 * LANE), lambda b, g, i: (b * nq + i, g)),
                  pl.BlockSpec((seq, hg * LANE), lambda b, g, i: (b, g)),
                  pl.BlockSpec((seq, hg * LANE), lambda b, g, i: (b, g)),
                  pl.BlockSpec((seq, LANE), lambda b, g, i: (b, 0))],
        out_specs=pl.BlockSpec((tq, hg * LANE), lambda b, g, i: (b * nq + i, g)),
        out_shape=jax.ShapeDtypeStruct((nb_seq * seq, n_heads * LANE), bf16),
        scratch_shapes=[pltpu.VMEM((hg, tq, LANE), f32), pltpu.VMEM((hg, tq, 2 * LANE), f32)],
        compiler_params=_params("arbitrary", "arbitrary", "arbitrary"),
    )(q, kn, v, kpe)


def _head_matmul_kernel(a_ref, w_ref, o_ref):
    o_ref[...] = jnp.dot(a_ref[...].astype(bf16), w_ref[...].astype(bf16),
                         preferred_element_type=f32).astype(o_ref.dtype)


def _head_matmul(a, a_spec, w, w_spec, out_shape, out_spec, n_heads):
    return pl.pallas_call(
        _head_matmul_kernel,
        grid=(n_heads,),
        in_specs=[a_spec, w_spec],
        out_specs=out_spec,
        out_shape=out_shape,
        compiler_params=_params("arbitrary"),
    )(a, w)


def _decode_kernel(pt_ref, qlat_ref, qpe_ref, cnew_ref, knew_ref, cache_c, cache_k, o_ref,
                   cbuf, kbuf, sem, m_scr, l_scr, acc_scr, *, bp, ppc, n_chunks, n_pages, steps, c):
    g = pl.program_id(0)
    ch = pl.program_id(1)
    step = g * n_chunks + ch
    slot = step % 2
    total = pl.num_programs(0) * n_chunks
    page = cbuf.shape[3]
    rope = kbuf.shape[2]

    def copies(gg, cc, sl, bi, p):
        phys = pt_ref[(gg * bp + bi) * n_pages + cc * ppc + p]
        return (pltpu.make_async_copy(cache_c.at[phys], cbuf.at[sl, bi, p], sem.at[0, sl]),
                pltpu.make_async_copy(cache_k.at[phys], kbuf.at[sl, bi, :, pl.ds(p * page, page)], sem.at[1, sl]))

    def issue(gg, cc, sl):
        for bi in range(bp):
            for p in range(ppc):
                for cp in copies(gg, cc, sl, bi, p):
                    cp.start()

    @pl.when(step == 0)
    def _():
        issue(0, 0, 0)

    @pl.when(step + 1 < total)
    def _():
        nxt = step + 1
        issue(nxt // n_chunks, nxt % n_chunks, 1 - slot)

    for bi in range(bp):
        for p in range(ppc):
            for cp in copies(g, ch, slot, bi, p):
                cp.wait()

    @pl.when(ch == 0)
    def _():
        m_scr[...] = jnp.full_like(m_scr, -jnp.inf)
        l_scr[...] = jnp.zeros_like(l_scr)
        acc_scr[...] = jnp.zeros_like(acc_scr)

    nt = (((1,), (1,)), ((), ()))

    def update(state, s, vals):
        m, l, acc = state
        m_new = jnp.maximum(m, jnp.max(s, axis=-1, keepdims=True))
        p = jnp.exp2((s - m_new) * c)
        corr = jnp.exp2((m - m_new) * c)
        l = l * corr + jnp.sum(p, axis=-1, keepdims=True)
        acc = acc * corr + jnp.dot(p.astype(bf16), vals, preferred_element_type=f32)
        return m_new, l, acc

    states = [(m_scr[bi], l_scr[bi], acc_scr[bi]) for bi in range(bp)]
    for bi in range(bp):
        ck = cbuf[slot, bi].reshape(ppc * page, cbuf.shape[4]).astype(bf16)
        s = (lax.dot_general(qlat_ref[bi], ck, nt, preferred_element_type=f32)
             + jnp.dot(qpe_ref[bi, :, :rope], kbuf[slot, bi].astype(bf16), preferred_element_type=f32))
        states[bi] = update(states[bi], s, ck)
    for bi in range(bp):
        m_scr[bi], l_scr[bi], acc_scr[bi] = states[bi]

    @pl.when(ch == n_chunks - 1)
    def _():
        for bi in range(bp):
            cn = cnew_ref[bi]
            sn = (lax.dot_general(qlat_ref[bi], cn, nt, preferred_element_type=f32)
                  + lax.dot_general(qpe_ref[bi], knew_ref[bi], nt, preferred_element_type=f32))
            t_row = lax.broadcasted_iota(jnp.int32, sn.shape, 0) % steps
            key = lax.broadcasted_iota(jnp.int32, sn.shape, 1)
            _, l, acc = update(states[bi], jnp.where(key <= t_row, sn, -jnp.inf), cn)
            o_ref[bi] = (acc * (1.0 / l)).astype(o_ref.dtype)


def _decode_attn(page_table, qlat, qpe, cnew, knew, cache_c, cache_k, scale, steps, ppc, bp):
    nb, rows, r_kv = qlat.shape
    n_pages = page_table.shape[1]
    page = cache_c.shape[1]
    rope = cache_k.shape[1]
    n_chunks = n_pages // ppc

    def seq_spec(a, b):
        return pl.BlockSpec((bp, a, b), lambda g, ch, pt: (g, 0, 0))

    gs = pltpu.PrefetchScalarGridSpec(
        num_scalar_prefetch=1,
        grid=(nb // bp, n_chunks),
        in_specs=[seq_spec(rows, r_kv), seq_spec(rows, LANE), seq_spec(LANE, r_kv), seq_spec(LANE, LANE),
                  pl.BlockSpec(memory_space=pl.ANY),
                  pl.BlockSpec(memory_space=pl.ANY)],
        out_specs=seq_spec(rows, r_kv),
        scratch_shapes=[pltpu.VMEM((2, bp, ppc, page, r_kv), f32),
                        pltpu.VMEM((2, bp, rope, ppc * page), f32),
                        pltpu.SemaphoreType.DMA((2, 2)),
                        pltpu.VMEM((bp, rows, 1), f32), pltpu.VMEM((bp, rows, 1), f32),
                        pltpu.VMEM((bp, rows, r_kv), f32)],
    )
    return pl.pallas_call(
        functools.partial(_decode_kernel, bp=bp, ppc=ppc, n_chunks=n_chunks, n_pages=n_pages, steps=steps,
                          c=scale * LOG2E),
        grid_spec=gs,
        out_shape=jax.ShapeDtypeStruct((nb, rows, r_kv), bf16),
        compiler_params=_params("arbitrary", "arbitrary"),
    )(page_table.reshape(-1), qlat, qpe, cnew, knew, cache_c, cache_k)


def _rope_tables(pos, rope):
    half = rope // 2
    inv_freq = ROPE_THETA ** (-jnp.arange(half, dtype=f32) / half)
    ang = pos.astype(f32)[:, None] * inv_freq
    cos, sin = jnp.cos(ang), jnp.sin(ang)
    pad = jnp.zeros((pos.shape[0], LANE - rope), f32)
    return (jnp.concatenate([cos, cos, pad], axis=-1), jnp.concatenate([-sin, sin, pad], axis=-1))


def kernel(x_prompt, x_sample, cache_ckv, cache_kpe, state_conv, state_rglru, page_table, c_prompt, c_sample, mod_w, mod_b, mix_norm, ffn_norm, rg_w_in, rg_conv_w, rg_conv_b, rg_w_gates, rg_b_gates, rg_lambda, rg_w_out, kv_mod_w, kv_mod_b, kv_norm, kv_w_dkv, kv_latent_norm, kv_w_uk, kv_w_uv, mla_w_dq, mla_q_norm, mla_w_uq, mla_w_o, moe_w_group, moe_b_group, moe_w_expert, moe_b_expert, moe_w13, moe_w2, final_norm):
    bp, seq, d = x_prompt.shape
    bs, steps, _ = x_sample.shape
    d_rnn = rg_conv_w.shape[-1]
    cwid = rg_conv_w.shape[1]
    nb_rnn, rnn_blk = rg_w_gates.shape[1], rg_w_gates.shape[2]
    r_kv, n_heads, d_nope = kv_w_uk.shape
    d_v = kv_w_uv.shape[2]
    rope = cache_kpe.shape[-1]
    half = rope // 2
    page = cache_ckv.shape[1]
    past = page_table.shape[1] * page
    n_groups = moe_w_group.shape[-1]
    n_exp = moe_w_expert.shape[-1]
    scale = 1.0 / math.sqrt(d_nope + rope)
    assert d_nope == LANE and d_v == LANE and rope <= LANE and d_rnn % LANE == 0
    assert rg_w_in.shape[0] == 1 and mla_w_dq.shape[0] == 1 and mod_w.shape[0] == 2
    assert steps >= cwid - 1 and seq >= cwid - 1

    n_p, n_s = bp * seq, steps * bs
    gp = _Group(n_p, min(512, seq), seq, False)
    gs_ = _Group(n_s, min(512, n_s), 0, True)
    groups = (gp, gs_)

    xp = x_prompt.reshape(n_p, d)
    xs = jnp.swapaxes(x_sample, 0, 1).reshape(n_s, d)

    n_c = bp + bs
    n_c_pad = -(-n_c // SUBLANE) * SUBLANE
    c_all = jnp.concatenate([c_prompt, c_sample, jnp.zeros((n_c_pad - n_c, d), f32)], axis=0)
    mods = [_bias_matmul(c_all, mod_w, layer, mod_b[layer]) for layer in range(2)]
    kv_mods = _bias_matmul(c_all, kv_mod_w[None], 0, kv_mod_b)

    def mod_vec(m, idx, grp_i):
        rows = slice(0, bp) if grp_i == 0 else slice(bp, bp + bs)
        return m[rows, idx * d:(idx + 1) * d]

    w_in = rg_w_in[0].astype(bf16)
    w_out = rg_w_out[0].astype(bf16)
    starts, win = _gate_windows(d_rnn, rnn_blk)
    wg = _gate_weights(rg_w_gates[0], starts, win)
    bgr = rg_b_gates[0][:, :rnn_blk].reshape(1, d_rnn)
    bgi = rg_b_gates[0][:, rnn_blk:].reshape(1, d_rnn)
    conv_w = rg_conv_w[0]
    conv_b = rg_conv_b[0].reshape(1, d_rnn)
    lam = rg_lambda[0].reshape(1, d_rnn)

    w_dkv = jnp.concatenate([kv_w_dkv, jnp.zeros((d, LANE - rope), f32)], axis=1).astype(bf16)
    w_up = jnp.concatenate([kv_w_uk.reshape(r_kv, n_heads * d_nope), kv_w_uv.reshape(r_kv, n_heads * d_v)],
                           axis=1).astype(bf16)
    w_dq = mla_w_dq[0].astype(bf16)
    wq3 = mla_w_uq[0].reshape(-1, n_heads, d_nope + rope)
    wq = jnp.concatenate([wq3, jnp.zeros((wq3.shape[0], n_heads, LANE - rope), f32)], axis=-1)
    wq = wq.reshape(-1, n_heads * 2 * LANE).astype(bf16)
    w_o = mla_w_o[0].astype(bf16)
    w_uk_t = jnp.transpose(kv_w_uk, (1, 2, 0))
    w_uv_flat = kv_w_uv.reshape(r_kv, n_heads * d_v)

    cos_p, sin_p = _rope_tables(jnp.arange(seq, dtype=jnp.int32), rope)
    cos_s, sin_s = _rope_tables(past + jnp.arange(steps, dtype=jnp.int32), rope)
    tabs = ((cos_p, sin_p), (cos_s, sin_s))

    def moe(xs_in, layer):
        rt_l = []
        n_pad = LANE - n_groups - n_exp
        wr = jnp.concatenate([moe_w_group[layer], moe_w_expert[layer], jnp.zeros((d, n_pad), f32)], axis=1)
        br = jnp.concatenate([moe_b_group[layer], moe_b_expert[layer], jnp.zeros((n_pad,), f32)]).reshape(1, LANE)
        xn_all, row0 = None, 0
        for gi, grp in enumerate(groups):
            xn_all, rt = _router(xs_in[gi], ffn_norm[layer], mod_vec(mods[layer], 3, gi),
                                 mod_vec(mods[layer], 4, gi), wr, br, grp, n_groups, n_exp,
                                 n_p + n_s, row0, xn_all)
            row0 += grp.n
            rt_l.append(rt)
        route = jnp.concatenate(rt_l, axis=0)
        expert_ids = route[:, :TOP_K].astype(jnp.int32)
        weights = route[:, TOP_K:2 * TOP_K]
        slot_tok, block_expert, n_used, next_expert, block_rows, dest = _dispatch(expert_ids, n_exp)
        yb = _experts(xn_all, slot_tok, block_expert, n_used, next_expert, block_rows, moe_w13, moe_w2, layer)
        out = []
        lo = 0
        for grp in groups:
            dg = dest[lo:lo + grp.n]
            out.append((yb[dg[:, 0]], yb[dg[:, 1]], weights[lo:lo + grp.n]))
            lo += grp.n
        return out

    x_cur = [xp, xs]
    projs, h_last = [], []
    for gi, grp in enumerate(groups):
        proj = _normmod_matmul(x_cur[gi], mix_norm[0], mod_vec(mods[0], 0, gi), mod_vec(mods[0], 1, gi),
                               w_in, grp, tn_cap=2688)
        projs.append(proj)
        if gi == 0:
            hg, hl = _rglru_prompt(proj, jnp.zeros((bp, 1, d_rnn), f32), jnp.zeros((bp, SUBLANE, d_rnn), f32),
                                   conv_w, conv_b, wg, bgr, bgi, lam, starts, win, bp, seq, min(256, seq))
            hl = hl.reshape(bp, d_rnn)
        else:
            cst = jnp.swapaxes(state_conv[0], 0, 1)
            hg, hl = _rglru_sample(proj.reshape(steps, bs, 2 * d_rnn), state_rglru[0], cst, conv_w, conv_b,
                                   wg, bgr, bgi, lam, starts, win)
            hg = hg.reshape(n_s, d_rnn)
        h_last.append(hl)
        x_cur[gi] = _matmul_res(hg, w_out, x_cur[gi], mod_vec(mods[0], 2, gi), grp)
    moe_out = moe(x_cur, 0)
    for gi, grp in enumerate(groups):
        ya, yc, wts = moe_out[gi]
        x_cur[gi] = _combine(x_cur[gi], ya, yc, wts, mod_vec(mods[0], 5, gi), grp)

    kv_out = []
    for gi, grp in enumerate(groups):
        kv_out.append(_kv_latent(x_cur[gi], kv_norm, kv_mods[:, :d][(slice(0, bp) if gi == 0 else slice(bp, bp + bs))],
                                 kv_mods[:, d:][(slice(0, bp) if gi == 0 else slice(bp, bp + bs))],
                                 w_dkv, kv_latent_norm, tabs[gi][0], tabs[gi][1], grp, half,
                                 w_up=w_up if gi == 0 else None))
    qs = [_q_proj(x_cur[gi], mix_norm[1], mod_vec(mods[1], 0, gi), mod_vec(mods[1], 1, gi), w_dq,
                  mla_q_norm[0], wq, tabs[gi][0], tabs[gi][1], grp, n_heads, half)
          for gi, grp in enumerate(groups)]

    ckv_p, kpe_p, kn_p, v_p, kpeb_p = kv_out[0]
    tq = min(ATTN_ROWS, seq)
    hgrp = 4 if n_heads % 4 == 0 else 1
    attn_p = _attn_prompt(qs[0], kn_p, v_p, kpeb_p, bp, seq, n_heads, scale, tq, hgrp)

    ckv_s, kpe_s = kv_out[1]
    rows = n_heads * steps
    qlat = _head_matmul(
        qs[1], pl.BlockSpec((n_s, LANE), lambda h: (0, 2 * h)),
        w_uk_t, pl.BlockSpec((None, d_nope, r_kv), lambda h: (h, 0, 0)),
        jax.ShapeDtypeStruct((n_heads, n_s, r_kv), bf16),
        pl.BlockSpec((None, n_s, r_kv), lambda h: (h, 0, 0)), n_heads)
    qlat = jnp.transpose(qlat.reshape(n_heads, steps, bs, r_kv), (2, 0, 1, 3)).reshape(bs, rows, r_kv)
    qpe = qs[1].reshape(steps, bs, n_heads, 2, LANE)[:, :, :, 1]
    qpe = jnp.transpose(qpe, (1, 2, 0, 3)).reshape(bs, rows, LANE)

    def new_keys(a):
        a = jnp.swapaxes(a.reshape(steps, bs, a.shape[-1]), 0, 1).astype(bf16)
        return jnp.concatenate([a, jnp.zeros((bs, LANE - steps, a.shape[-1]), bf16)], axis=1)

    ppc = _pick_tile(page_table.shape[1], DECODE_PAGES, 1)
    olat = _decode_attn(page_table, qlat, qpe, new_keys(ckv_s), new_keys(kpe_s), cache_ckv,
                        jnp.swapaxes(cache_kpe, 1, 2), scale, steps, ppc,
                        DECODE_BATCH if bs % DECODE_BATCH == 0 else 1)
    olat = jnp.transpose(olat.reshape(bs, n_heads, steps, r_kv), (1, 2, 0, 3)).reshape(n_heads, n_s, r_kv)
    attn_s = _head_matmul(
        olat, pl.BlockSpec((None, n_s, r_kv), lambda h: (h, 0, 0)),
        w_uv_flat, pl.BlockSpec((r_kv, d_v), lambda h: (0, h)),
        jax.ShapeDtypeStruct((n_s, n_heads * d_v), bf16),
        pl.BlockSpec((n_s, d_v), lambda h: (0, h)), n_heads)

    attn = [attn_p, attn_s]
    for gi, grp in enumerate(groups):
        x_cur[gi] = _matmul_res(attn[gi], w_o, x_cur[gi], mod_vec(mods[1], 2, gi), grp)
    moe_out = moe(x_cur, 1)
    ys = []
    for gi, grp in enumerate(groups):
        ya, yc, wts = moe_out[gi]
        ys.append(_combine(x_cur[gi], ya, yc, wts, mod_vec(mods[1], 5, gi), grp, final_gain=final_norm))

    y_prompt = ys[0].reshape(bp, seq, d)
    y_sample = jnp.swapaxes(ys[1].reshape(steps, bs, d), 0, 1)
    conv_p = projs[0].reshape(bp, seq, 2 * d_rnn)[:, seq - (cwid - 1):, d_rnn:][None]
    conv_s = jnp.swapaxes(projs[1].reshape(steps, bs, 2 * d_rnn)[steps - (cwid - 1):, :, d_rnn:], 0, 1)[None]
    h_p = h_last[0][None]
    h_s = h_last[1][None]
    ckv_prompt = ckv_p.reshape(bp, seq, r_kv)
    kpe_prompt = kpe_p[:, :rope].reshape(bp, seq, rope)
    ckv_sample = jnp.swapaxes(ckv_s.reshape(steps, bs, r_kv), 0, 1)
    kpe_sample = jnp.swapaxes(kpe_s[:, :rope].reshape(steps, bs, rope), 0, 1)
    return (y_prompt, y_sample, conv_p, h_p, ckv_prompt, kpe_prompt, conv_s, h_s, ckv_sample, kpe_sample)
```

```python
import functools
import math
from typing import NamedTuple

import jax
import jax.numpy as jnp
from jax import lax
from jax.experimental import pallas as pl
from jax.experimental.pallas import tpu as pltpu

EPS = 1e-6
LRU_C = 8.0
ROPE_THETA = 10000.0
TOP_K = 2
LANE = 128
SUBLANE = 8
VMEM_LIMIT = 56 * 1024 * 1024
MOE_ROWS = 128
MOE_GROUP = 32
ATTN_ROWS = 512
RNN_ROWS = 128
ROUTER_ROWS = 256
SAMPLE_ROWS = 64
DECODE_PAGES = 32
DECODE_BATCH = 1
LOG2E = 1.4426950408889634

f32 = jnp.float32
bf16 = jnp.bfloat16


def _params(*sem):
    return pltpu.CompilerParams(dimension_semantics=sem, vmem_limit_bytes=VMEM_LIMIT)


def _pick_tile(n, cap, mult=LANE):
    if n <= cap:
        return n
    best = None
    for t in range(mult, cap + 1, mult):
        if n % t == 0:
            best = t
    assert best is not None, (n, cap, mult)
    return best


class _Group(NamedTuple):
    n: int
    tm: int
    seq: int
    per_token: bool


def _mod_operand(vec, grp):
    d = vec.shape[-1]
    if grp.per_token:
        arr = jnp.tile(vec, (grp.n // vec.shape[0], 1))
        return arr, pl.BlockSpec((grp.tm, d), lambda i, *_: (i, 0))
    per_seq = grp.seq // grp.tm
    return vec[:, None, :], pl.BlockSpec((None, 1, d), lambda i, *_: (i // per_seq, 0, 0))


def _pos_operand(tab, grp):
    if grp.per_token:
        arr = jnp.repeat(tab, grp.n // tab.shape[0], axis=0)
        return arr, pl.BlockSpec((grp.tm, LANE), lambda i, *_: (i, 0))
    per_seq = grp.seq // grp.tm
    return tab, pl.BlockSpec((grp.tm, LANE), lambda i, *_: (i % per_seq, 0))


def _row_spec(grp, width):
    return pl.BlockSpec((grp.tm, width), lambda i, *_: (i, 0))


def _full_spec(shape):
    nd = len(shape)
    return pl.BlockSpec(shape, lambda *_: (0,) * nd)


def _rms(x, gain):
    return x * lax.rsqrt(jnp.mean(x * x, axis=-1, keepdims=True) + EPS) * gain


def _normmod(x, gain, shift, scale):
    return _rms(x, gain) * (1.0 + scale) + shift


def _rope_tile(t, cos, sin, half):
    lane = lax.broadcasted_iota(jnp.int32, t.shape, 1)
    rot = jnp.where(lane < half, pltpu.roll(t, LANE - half, 1), pltpu.roll(t, half, 1))
    return t * cos + rot * sin


def _sigmoid(x):
    return 0.5 * (jnp.tanh(0.5 * x) + 1.0)


def _gelu(x):
    return 0.5 * x * (1.0 + jnp.tanh(math.sqrt(2.0 / math.pi) * (x + 0.044715 * (x * x * x))))


def _softplus(z):
    return jnp.maximum(z, 0.0) + jnp.log1p(jnp.exp(-jnp.abs(z)))


def _bias_matmul_kernel(a_ref, w_ref, b_ref, o_ref):
    o_ref[...] = jnp.dot(a_ref[...].astype(bf16), w_ref[...].astype(bf16),
                         preferred_element_type=f32) + b_ref[...]


def _bias_matmul(a, w3, layer, b):
    m, k = a.shape
    n = w3.shape[-1]
    tn = _pick_tile(n, 1024)
    return pl.pallas_call(
        _bias_matmul_kernel,
        grid=(n // tn,),
        in_specs=[_full_spec((m, k)),
                  pl.BlockSpec((None, k, tn), lambda j: (layer, 0, j)),
                  pl.BlockSpec((1, tn), lambda j: (0, j))],
        out_specs=pl.BlockSpec((m, tn), lambda j: (0, j)),
        out_shape=jax.ShapeDtypeStruct((m, n), f32),
        compiler_params=_params("arbitrary"),
    )(a, w3, b.reshape(1, n))


def _normmod_matmul_kernel(x_ref, gain_ref, shift_ref, scale_ref, w_ref, o_ref, xn_ref):
    @pl.when(pl.program_id(1) == 0)
    def _():
        xn_ref[...] = _normmod(x_ref[...], gain_ref[...], shift_ref[...], scale_ref[...]).astype(bf16)

    o_ref[...] = jnp.dot(xn_ref[...], w_ref[...], preferred_element_type=f32).astype(o_ref.dtype)


def _normmod_matmul(x, gain, shift, scale, w, grp, tn_cap=1024, out_dtype=f32):
    d = x.shape[1]
    n = w.shape[1]
    tn = _pick_tile(n, tn_cap)
    sh, sh_spec = _mod_operand(shift, grp)
    sc, sc_spec = _mod_operand(scale, grp)
    return pl.pallas_call(
        _normmod_matmul_kernel,
        grid=(grp.n // grp.tm, n // tn),
        in_specs=[_row_spec(grp, d), _full_spec((1, d)), sh_spec, sc_spec,
                  pl.BlockSpec((d, tn), lambda i, j: (0, j))],
        out_specs=pl.BlockSpec((grp.tm, tn), lambda i, j: (i, j)),
        out_shape=jax.ShapeDtypeStruct((grp.n, n), out_dtype),
        scratch_shapes=[pltpu.VMEM((grp.tm, d), bf16)],
        compiler_params=_params("arbitrary", "arbitrary"),
    )(x, gain.reshape(1, d), sh, sc, w)


def _matmul_res_kernel(a_ref, w_ref, res_ref, gate_ref, o_ref):
    y = jnp.dot(a_ref[...].astype(bf16), w_ref[...], preferred_element_type=f32)
    o_ref[...] = res_ref[...] + gate_ref[...] * y


def _matmul_res(a, w, res, gate, grp, tn_cap=1024):
    k = a.shape[1]
    n = w.shape[1]
    tn = _pick_tile(n, tn_cap)
    if grp.per_token:
        g_arr = jnp.tile(gate, (grp.n // gate.shape[0], 1))
        g_spec = pl.BlockSpec((grp.tm, tn), lambda j, i: (i, j))
    else:
        per_seq = grp.seq // grp.tm
        g_arr = gate[:, None, :]
        g_spec = pl.BlockSpec((None, 1, tn), lambda j, i: (i // per_seq, 0, j))
    return pl.pallas_call(
        _matmul_res_kernel,
        grid=(n // tn, grp.n // grp.tm),
        in_specs=[pl.BlockSpec((grp.tm, k), lambda j, i: (i, 0)),
                  pl.BlockSpec((k, tn), lambda j, i: (0, j)),
                  pl.BlockSpec((grp.tm, tn), lambda j, i: (i, j)),
                  g_spec],
        out_specs=pl.BlockSpec((grp.tm, tn), lambda j, i: (i, j)),
        out_shape=jax.ShapeDtypeStruct((grp.n, n), f32),
        compiler_params=_params("arbitrary", "arbitrary"),
    )(a, w, res, g_arr)


def _gate_windows(d_rnn, blk):
    nt = d_rnn // LANE
    raw, need = [], 0
    for j in range(nt):
        n0 = (LANE * j) // blk
        n1 = (LANE * j + LANE - 1) // blk
        s = (blk * n0) // LANE * LANE
        raw.append(s)
        need = max(need, blk * (n1 + 1) - s)
    win = min(d_rnn, -(-need // LANE) * LANE)
    return [min(s, d_rnn - win) for s in raw], win


def _gate_weights(w_gates, starts, win):
    nb, blk, _ = w_gates.shape
    wb = w_gates.astype(bf16)
    tiles = []
    for j, s in enumerate(starts):
        halves = []
        for off in (0, blk):
            acc = None
            for n in range((LANE * j) // blk, (LANE * j + LANE - 1) // blk + 1):
                c_lo = max(blk * n, LANE * j)
                c_hi = min(blk * (n + 1), LANE * (j + 1))
                r_off = blk * n - s
                assert 0 <= r_off and r_off + blk <= win
                piece = wb[n, :, off + c_lo - blk * n:off + c_hi - blk * n]
                piece = jnp.pad(piece, ((r_off, win - blk - r_off), (c_lo - LANE * j, LANE * (j + 1) - c_hi)))
                acc = piece if acc is None else acc + piece
            halves.append(acc)
        tiles.append(jnp.concatenate(halves, axis=1))
    return jnp.stack(tiles)


def _lru_inputs(g, b_r, b_i, neg_c_sp, u):
    r = _sigmoid(g[:, :LANE] + b_r)
    i = _sigmoid(g[:, LANE:] + b_i)
    log_a = neg_c_sp * r
    a = jnp.exp(log_a)
    x = jnp.sqrt(-jnp.tanh(log_a) * (a * a + 1.0)) * (i * u)
    return a, x


def _rglru_prompt_kernel(proj_ref, h0_ref, cinit_ref, cw_ref, cb_ref, wg_ref, bgr_ref, bgi_ref, lam_ref,
                         hg_ref, hlast_ref, ubuf, ucf, ucb, a_scr, x_scr, hcar,
                         *, starts, win, tc, d_rnn, cwid):
    t = pl.program_id(1)
    hist = SUBLANE

    @pl.when(t == 0)
    def _():
        ubuf[0:hist, :] = cinit_ref[...]
        hcar[...] = h0_ref[...]

    ubuf[hist:hist + tc, :] = proj_ref[:, d_rnn:]
    uc = cb_ref[...]
    for k in range(cwid):
        off = hist - (cwid - 1) + k
        uc = uc + cw_ref[k:k + 1, :] * ubuf[off:off + tc, :]
    ucf[...] = uc
    ucb[...] = uc.astype(bf16)
    ubuf[0:hist, :] = ubuf[tc:tc + hist, :]

    neg_c_sp = -LRU_C * _softplus(-lam_ref[...])
    seg = tc // SUBLANE
    for j in range(d_rnn // LANE):
        cs = slice(j * LANE, (j + 1) * LANE)
        g = jnp.dot(ucb[:, starts[j]:starts[j] + win], wg_ref[j], preferred_element_type=f32)
        a, x = _lru_inputs(g, bgr_ref[:, cs], bgi_ref[:, cs], neg_c_sp[:, cs], ucf[:, cs])
        a_scr[j] = a
        x_scr[j] = x
        h = jnp.zeros((SUBLANE, LANE), f32)
        p = jnp.ones((SUBLANE, LANE), f32)
        for k in range(seg):
            ak = a_scr[j, pl.ds(k, SUBLANE, stride=seg), :]
            xk = x_scr[j, pl.ds(k, SUBLANE, stride=seg), :]
            h = ak * h + xk
            p = p * ak
            x_scr[j, pl.ds(k, SUBLANE, stride=seg), :] = h
            a_scr[j, pl.ds(k, SUBLANE, stride=seg), :] = p
        c = hcar[:, cs]
        outs = []
        for r in range(SUBLANE):
            rows = slice(r * seg, (r + 1) * seg)
            outs.append(x_scr[j, rows, :] + a_scr[j, rows, :] * c)
            c = p[r:r + 1, :] * c + h[r:r + 1, :]
        hcar[:, cs] = c
        hs = jnp.concatenate(outs, axis=0)
        hg_ref[:, cs] = (hs * _gelu(proj_ref[:, cs])).astype(hg_ref.dtype)
    hlast_ref[...] = hcar[...]


def _rglru_prompt(proj, h0, conv_init, conv_w, conv_b, wg, bgr, bgi, lam, starts, win, nb_seq, seq, tc):
    d_rnn = proj.shape[1] // 2
    cwid = conv_w.shape[0]
    nt = d_rnn // LANE
    per_seq = seq // tc
    kern = functools.partial(_rglru_prompt_kernel, starts=tuple(starts), win=win, tc=tc, d_rnn=d_rnn, cwid=cwid)
    return pl.pallas_call(
        kern,
        grid=(nb_seq, per_seq),
        in_specs=[pl.BlockSpec((tc, 2 * d_rnn), lambda b, t: (b * per_seq + t, 0)),
                  pl.BlockSpec((None, 1, d_rnn), lambda b, t: (b, 0, 0)),
                  pl.BlockSpec((None, SUBLANE, d_rnn), lambda b, t: (b, 0, 0)),
                  _full_spec((cwid, d_rnn)), _full_spec((1, d_rnn)),
                  _full_spec(wg.shape), _full_spec((1, d_rnn)), _full_spec((1, d_rnn)), _full_spec((1, d_rnn))],
        out_specs=[pl.BlockSpec((tc, d_rnn), lambda b, t: (b * per_seq + t, 0)),
                   pl.BlockSpec((None, 1, d_rnn), lambda b, t: (b, 0, 0))],
        out_shape=[jax.ShapeDtypeStruct((nb_seq * seq, d_rnn), bf16),
                   jax.ShapeDtypeStruct((nb_seq, 1, d_rnn), f32)],
        scratch_shapes=[pltpu.VMEM((tc + SUBLANE, d_rnn), f32),
                        pltpu.VMEM((tc, d_rnn), f32),
                        pltpu.VMEM((tc, d_rnn), bf16),
                        pltpu.VMEM((nt, tc, LANE), f32),
                        pltpu.VMEM((nt, tc, LANE), f32),
                        pltpu.VMEM((1, d_rnn), f32)],
        compiler_params=_params("arbitrary", "arbitrary"),
    )(proj, h0, conv_init, conv_w, conv_b, wg, bgr, bgi, lam)


def _rglru_sample_kernel(proj_ref, h0_ref, cst_ref, cw_ref, cb_ref, wg_ref, bgr_ref, bgi_ref, lam_ref,
                         hg_ref, hlast_ref, ucf, ucb, *, starts, win, steps, tb, d_rnn, cwid):
    def up(tp):
        if tp < cwid - 1:
            return cst_ref[tp]
        return proj_ref[tp - cwid + 1, :, d_rnn:]

    for t in range(steps):
        uc = cb_ref[...]
        for k in range(cwid):
            uc = uc + cw_ref[k:k + 1, :] * up(t + k)
        ucf[t * tb:(t + 1) * tb, :] = uc
        ucb[t * tb:(t + 1) * tb, :] = uc.astype(bf16)

    neg_c_sp = -LRU_C * _softplus(-lam_ref[...])
    for j in range(d_rnn // LANE):
        cs = slice(j * LANE, (j + 1) * LANE)
        g = jnp.dot(ucb[:, starts[j]:starts[j] + win], wg_ref[j], preferred_element_type=f32)
        h = h0_ref[:, cs]
        for t in range(steps):
            rows = slice(t * tb, (t + 1) * tb)
            a, x = _lru_inputs(g[rows], bgr_ref[:, cs], bgi_ref[:, cs], neg_c_sp[:, cs], ucf[rows, cs])
            h = a * h + x
            hg_ref[t, :, cs] = (h * _gelu(proj_ref[t, :, cs])).astype(hg_ref.dtype)
        hlast_ref[:, cs] = h


def _rglru_sample(proj, h0, conv_state, conv_w, conv_b, wg, bgr, bgi, lam, starts, win):
    steps, nb, _ = proj.shape
    d_rnn = proj.shape[2] // 2
    cwid = conv_w.shape[0]
    tb = min(SAMPLE_ROWS, nb)
    kern = functools.partial(_rglru_sample_kernel, starts=tuple(starts), win=win, steps=steps, tb=tb,
                             d_rnn=d_rnn, cwid=cwid)
    return pl.pallas_call(
        kern,
        grid=(nb // tb,),
        in_specs=[pl.BlockSpec((steps, tb, 2 * d_rnn), lambda i: (0, i, 0)),
                  pl.BlockSpec((tb, d_rnn), lambda i: (i, 0)),
                  pl.BlockSpec((cwid - 1, tb, d_rnn), lambda i: (0, i, 0)),
                  _full_spec((cwid, d_rnn)), _full_spec((1, d_rnn)),
                  _full_spec(wg.shape), _full_spec((1, d_rnn)), _full_spec((1, d_rnn)), _full_spec((1, d_rnn))],
        out_specs=[pl.BlockSpec((steps, tb, d_rnn), lambda i: (0, i, 0)),
                   pl.BlockSpec((tb, d_rnn), lambda i: (i, 0))],
        out_shape=[jax.ShapeDtypeStruct((steps, nb, d_rnn), bf16),
                   jax.ShapeDtypeStruct((nb, d_rnn), f32)],
        scratch_shapes=[pltpu.VMEM((steps * tb, d_rnn), f32), pltpu.VMEM((steps * tb, d_rnn), bf16)],
        compiler_params=_params("arbitrary"),
    )(proj, h0, conv_state, conv_w, conv_b, wg, bgr, bgi, lam)


def _first_max(vals, ids):
    m = jnp.max(vals, axis=-1, keepdims=True)
    idx = jnp.min(jnp.where(vals == m, ids, jnp.int32(2 ** 30)), axis=-1, keepdims=True)
    return m, idx


def _router_kernel(x_ref, gain_ref, shift_ref, scale_ref, w_ref, b_ref, *rest, n_groups, n_exp):
    xn_ref, rt_ref = rest[-2:]
    xn = _normmod(x_ref[...], gain_ref[...], shift_ref[...], scale_ref[...])
    for s in range(xn_ref.shape[1]):
        xn_ref[:, s, :] = xn[:, s * LANE:(s + 1) * LANE]
    xh = xn.astype(bf16)
    xl = (xn - xh.astype(f32)).astype(bf16)
    ph = jnp.dot(xh, w_ref[...], preferred_element_type=f32)
    pl_ = jnp.dot(xl, w_ref[...], preferred_element_type=f32)
    lg = (ph[:, :LANE] + pl_[:, :LANE] + ph[:, LANE:] + pl_[:, LANE:]) + b_ref[...]
    per = n_exp // n_groups
    lane = lax.broadcasted_iota(jnp.int32, lg.shape, 1)
    neg = -jnp.inf
    gl = jnp.where(lane < n_groups, lg, neg)
    gmax, g_idx = _first_max(gl, lane)
    g_w = 1.0 / jnp.sum(jnp.exp(gl - gmax), axis=-1, keepdims=True)
    e_id = lane - n_groups
    lo = g_idx * per
    el = jnp.where(e_id >= lo, jnp.where(e_id < lo + per, lg, neg), neg)
    l1, i1 = _first_max(el, e_id)
    el2 = jnp.where(e_id == i1, neg, el)
    l2, i2 = _first_max(el2, e_id)
    e = jnp.exp(l2 - l1)
    w1 = g_w / (1.0 + e)
    w2 = w1 * e
    rt_ref[...] = jnp.where(lane == 0, i1.astype(f32),
                            jnp.where(lane == 1, i2.astype(f32),
                                      jnp.where(lane == 2, w1, jnp.where(lane == 3, w2, 0.0))))


def _router(x, gain, shift, scale, wr, br, grp, n_groups, n_exp, n_total, row0, xn_all=None):
    d = x.shape[1]
    nr = wr.shape[1]
    wh = wr.astype(bf16)
    wl = (wr - wh.astype(f32)).astype(bf16)
    sh, sh_spec = _mod_operand(shift, grp)
    sc, sc_spec = _mod_operand(scale, grp)
    assert row0 % grp.tm == 0 and n_total % grp.tm == 0
    t0 = row0 // grp.tm
    n_tiles = grp.n // grp.tm
    n_steps = n_tiles if xn_all is not None else n_total // grp.tm

    def clamp(spec):
        return pl.BlockSpec(spec.block_shape, lambda i: spec.index_map(jnp.minimum(i, n_tiles - 1)))

    args = [x, gain.reshape(1, d), sh, sc, jnp.concatenate([wh, wl], axis=1), br]
    specs = [clamp(_row_spec(grp, d)), _full_spec((1, d)), clamp(sh_spec), clamp(sc_spec),
             _full_spec((d, 2 * nr)), _full_spec((1, nr))]
    aliases = {}
    if xn_all is not None:
        aliases = {len(args): 0}
        args.append(xn_all)
        specs.append(pl.BlockSpec(memory_space=pl.ANY))
    return pl.pallas_call(
        functools.partial(_router_kernel, n_groups=n_groups, n_exp=n_exp),
        grid=(n_steps,),
        in_specs=specs,
        out_specs=[pl.BlockSpec((grp.tm, d // LANE, LANE), lambda i: (i + t0, 0, 0)),
                   clamp(_row_spec(grp, nr))],
        out_shape=[jax.ShapeDtypeStruct((n_total, d // LANE, LANE), f32),
                   jax.ShapeDtypeStruct((grp.n, nr), f32)],
        input_output_aliases=aliases,
        compiler_params=_params("arbitrary"),
    )(*args)


def _experts_kernel(be_ref, nu_ref, st_ref, nx_ref, bc_ref, xn_hbm, w13_hbm, w2_hbm, o_ref,
                    xbuf, sem, w13f, w2f, wsem, w13b, w2b, *, d_exp, layer):
    i = pl.program_id(0)
    n_used = nu_ref[0]
    slot = i % 2
    e = be_ref[i]
    prev = be_ref[jnp.maximum(i - 1, 0)]
    n_slab = xbuf.shape[2]

    def row_copy(tok, sl, r):
        return pltpu.make_async_copy(xn_hbm.at[tok], xbuf.at[sl, r], sem.at[sl])

    def row_groups(blk, fn):
        for g0 in range(0, MOE_ROWS, MOE_GROUP):
            @pl.when(g0 < bc_ref[blk])
            def _(g0=g0):
                for r in range(g0, g0 + MOE_GROUP):
                    fn(r)

    def gather(blk, sl):
        row_groups(blk, lambda r: row_copy(st_ref[blk * MOE_ROWS + r], sl, r).start())

    def weight_copies(ex):
        return (pltpu.make_async_copy(w13_hbm.at[layer, ex], w13f, wsem.at[0]),
                pltpu.make_async_copy(w2_hbm.at[layer, ex], w2f, wsem.at[1]))

    @pl.when((i == 0) & (n_used > 0))
    def _():
        xbuf[...] = jnp.zeros_like(xbuf)
        gather(0, 0)
        for cp in weight_copies(e):
            cp.start(priority=1)

    @pl.when(i + 1 < n_used)
    def _():
        gather(i + 1, 1 - slot)

    @pl.when((i < n_used) & ((i == 0) | (e != prev)))
    def _():
        for cp in weight_copies(e):
            cp.wait()
        w13b[...] = w13f[...].astype(bf16)
        w2b[...] = w2f[...].astype(bf16)

        @pl.when(nx_ref[i] >= 0)
        def _():
            for cp in weight_copies(nx_ref[i]):
                cp.start(priority=1)

    @pl.when(i < n_used)
    def _():
        row_groups(i, lambda r: row_copy(0, slot, r).wait())
        x = jnp.concatenate([xbuf[slot, :, s, :] for s in range(n_slab)], axis=-1).astype(bf16)
        gu = jnp.dot(x, w13b[...], preferred_element_type=f32)
        g = gu[:, :d_exp]
        act = (g * _sigmoid(g)) * gu[:, d_exp:]
        o_ref[...] = jnp.dot(act.astype(bf16), w2b[...], preferred_element_type=f32)

    @pl.when(i >= n_used)
    def _():
        o_ref[...] = jnp.zeros_like(o_ref)


def _experts(xn, slot_tok, block_expert, n_used, next_expert, block_rows, w13, w2, layer):
    n_slab = xn.shape[1]
    d = n_slab * LANE
    n_blocks = slot_tok.shape[0] // MOE_ROWS
    d_exp = w2.shape[2]
    gs = pltpu.PrefetchScalarGridSpec(
        num_scalar_prefetch=5,
        grid=(n_blocks,),
        in_specs=[pl.BlockSpec(memory_space=pl.ANY), pl.BlockSpec(memory_space=pl.ANY),
                  pl.BlockSpec(memory_space=pl.ANY)],
        out_specs=pl.BlockSpec((MOE_ROWS, d), lambda i, *_: (i, 0)),
        scratch_shapes=[pltpu.VMEM((2, MOE_ROWS, n_slab, LANE), f32), pltpu.SemaphoreType.DMA((2,)),
                        pltpu.VMEM((d, 2 * d_exp), f32), pltpu.VMEM((d_exp, d), f32),
                        pltpu.SemaphoreType.DMA((2,)),
                        pltpu.VMEM((d, 2 * d_exp), bf16), pltpu.VMEM((d_exp, d), bf16)],
    )
    return pl.pallas_call(
        functools.partial(_experts_kernel, d_exp=d_exp, layer=layer),
        grid_spec=gs,
        out_shape=jax.ShapeDtypeStruct((n_blocks * MOE_ROWS, d), f32),
        compiler_params=_params("arbitrary"),
    )(block_expert, n_used, slot_tok, next_expert, block_rows, xn, w13, w2)


def _combine_kernel(x_ref, ya_ref, yb_ref, wt_ref, gate_ref, *rest, final):
    if final:
        fg_ref, o_ref = rest
    else:
        (o_ref,) = rest
    wt = wt_ref[...]
    y = x_ref[...] + gate_ref[...] * (wt[:, 0:1] * ya_ref[...] + wt[:, 1:2] * yb_ref[...])
    if final:
        y = _rms(y, fg_ref[...])
    o_ref[...] = y


def _combine(x, ya, yb, wts, gate, grp, final_gain=None):
    d = x.shape[1]
    g_arr, g_spec = _mod_operand(gate, grp)
    final = final_gain is not None
    args = [x, ya, yb, wts, g_arr]
    specs = [_row_spec(grp, d), _row_spec(grp, d), _row_spec(grp, d), _row_spec(grp, TOP_K), g_spec]
    if final:
        args.append(final_gain.reshape(1, d))
        specs.append(_full_spec((1, d)))
    return pl.pallas_call(
        functools.partial(_combine_kernel, final=final),
        grid=(grp.n // grp.tm,),
        in_specs=specs,
        out_specs=_row_spec(grp, d),
        out_shape=jax.ShapeDtypeStruct((grp.n, d), f32),
        compiler_params=_params("arbitrary"),
    )(*args)


def _dispatch(expert_ids, n_exp):
    n = expert_ids.shape[0]
    na = n * TOP_K
    flat_e = expert_ids.reshape(na)
    onehot = (flat_e[:, None] == jnp.arange(n_exp, dtype=jnp.int32)[None, :]).astype(jnp.int32)
    incl = jnp.cumsum(onehot, axis=0)
    counts = incl[-1]
    padded = (counts + MOE_ROWS - 1) // MOE_ROWS * MOE_ROWS
    pad_end = jnp.cumsum(padded)
    pad_start = pad_end - padded
    dest = jnp.sum(onehot * (pad_start[None, :] + incl - 1), axis=1)
    n_blocks = (na + MOE_ROWS - 1) // MOE_ROWS + n_exp
    block_start = jnp.arange(n_blocks, dtype=jnp.int32) * MOE_ROWS
    block_expert = jnp.minimum(jnp.sum((pad_end[None, :] <= block_start[:, None]).astype(jnp.int32), axis=1),
                               n_exp - 1)
    n_used = pad_end[-1:] // MOE_ROWS
    bidx = jnp.arange(n_blocks, dtype=jnp.int32)
    is_first = (bidx == 0) | (block_expert != jnp.roll(block_expert, 1))
    pos = jnp.where(is_first & (bidx < n_used[0]), bidx, n_blocks)
    nxt = jnp.min(jnp.where(bidx[None, :] > bidx[:, None], pos[None, :], n_blocks), axis=1)
    next_expert = jnp.where(nxt < n_blocks, block_expert[jnp.minimum(nxt, n_blocks - 1)], -1)
    sel = (block_expert[:, None] == jnp.arange(n_exp, dtype=jnp.int32)[None, :]).astype(jnp.int32)
    real_end = jnp.sum(sel * (pad_start + counts)[None, :], axis=1)
    block_rows = jnp.clip(real_end - block_start, 0, MOE_ROWS)
    tok = jnp.arange(na, dtype=jnp.int32) // TOP_K
    slot_tok = jnp.zeros((n_blocks * MOE_ROWS,), jnp.int32).at[dest].set(tok)
    return slot_tok, block_expert, n_used, next_expert, block_rows, dest.reshape(n, TOP_K)


def _kv_kernel(x_ref, gain_ref, shift_ref, scale_ref, w_ref, lg_ref, cos_ref, sin_ref, *rest,
               r_kv, half, with_up):
    if with_up:
        wup_ref, ckv_ref, kpe_ref, kn_ref, v_ref, kpeb_ref = rest
    else:
        ckv_ref, kpe_ref = rest
    a = _normmod(x_ref[...], gain_ref[...], shift_ref[...], scale_ref[...]).astype(bf16)
    kv = jnp.dot(a, w_ref[...], preferred_element_type=f32)
    c = _rms(kv[:, :r_kv], lg_ref[...])
    ckv_ref[...] = c
    kp = _rope_tile(kv[:, r_kv:], cos_ref[...], sin_ref[...], half)
    kpe_ref[...] = kp
    if with_up:
        up = jnp.dot(c.astype(bf16), wup_ref[...], preferred_element_type=f32)
        hw = up.shape[1] // 2
        kn_ref[...] = up[:, :hw].astype(bf16)
        v_ref[...] = up[:, hw:].astype(bf16)
        kpeb_ref[...] = kp.astype(bf16)


def _kv_latent(x, gain, shift, scale, w_pad, latent_gain, cos, sin, grp, half, w_up=None):
    d = x.shape[1]
    r_kv = latent_gain.shape[0]
    with_up = w_up is not None
    sh, sh_spec = _mod_operand(shift, grp)
    sc, sc_spec = _mod_operand(scale, grp)
    cs, cs_spec = _pos_operand(cos, grp)
    sn, sn_spec = _pos_operand(sin, grp)
    args = [x, gain.reshape(1, d), sh, sc, w_pad, latent_gain.reshape(1, r_kv), cs, sn]
    specs = [_row_spec(grp, d), _full_spec((1, d)), sh_spec, sc_spec, _full_spec(w_pad.shape),
             _full_spec((1, r_kv)), cs_spec, sn_spec]
    out_specs = [_row_spec(grp, r_kv), _row_spec(grp, LANE)]
    out_shape = [jax.ShapeDtypeStruct((grp.n, r_kv), f32), jax.ShapeDtypeStruct((grp.n, LANE), f32)]
    if with_up:
        hw = w_up.shape[1] // 2
        args.append(w_up)
        specs.append(_full_spec(w_up.shape))
        out_specs += [_row_spec(grp, hw), _row_spec(grp, hw), _row_spec(grp, LANE)]
        out_shape += [jax.ShapeDtypeStruct((grp.n, hw), bf16), jax.ShapeDtypeStruct((grp.n, hw), bf16),
                      jax.ShapeDtypeStruct((grp.n, LANE), bf16)]
    return pl.pallas_call(
        functools.partial(_kv_kernel, r_kv=r_kv, half=half, with_up=with_up),
        grid=(grp.n // grp.tm,),
        in_specs=specs, out_specs=out_specs, out_shape=out_shape,
        compiler_params=_params("arbitrary"),
    )(*args)


def _q_kernel(x_ref, gain_ref, shift_ref, scale_ref, wdq_ref, qg_ref, wq_ref, cos_ref, sin_ref, q_ref,
              *, n_heads, half):
    a = _normmod(x_ref[...], gain_ref[...], shift_ref[...], scale_ref[...]).astype(bf16)
    ql = jnp.dot(a, wdq_ref[...], preferred_element_type=f32)
    qn = _rms(ql, qg_ref[...]).astype(bf16)
    cos = cos_ref[...]
    sin = sin_ref[...]
    for h in range(n_heads):
        q = jnp.dot(qn, wq_ref[:, 2 * LANE * h:2 * LANE * (h + 1)], preferred_element_type=f32)
        q_ref[:, 2 * LANE * h:2 * LANE * h + LANE] = q[:, :LANE].astype(bf16)
        q_ref[:, 2 * LANE * h + LANE:2 * LANE * (h + 1)] = _rope_tile(q[:, LANE:], cos, sin, half).astype(bf16)


def _q_proj(x, gain, shift, scale, w_dq, q_gain, wq, cos, sin, grp, n_heads, half):
    d = x.shape[1]
    rq = w_dq.shape[1]
    sh, sh_spec = _mod_operand(shift, grp)
    sc, sc_spec = _mod_operand(scale, grp)
    cs, cs_spec = _pos_operand(cos, grp)
    sn, sn_spec = _pos_operand(sin, grp)
    return pl.pallas_call(
        functools.partial(_q_kernel, n_heads=n_heads, half=half),
        grid=(grp.n // grp.tm,),
        in_specs=[_row_spec(grp, d), _full_spec((1, d)), sh_spec, sc_spec, _full_spec(w_dq.shape),
                  _full_spec((1, rq)), _full_spec(wq.shape), cs_spec, sn_spec],
        out_specs=_row_spec(grp, wq.shape[1]),
        out_shape=jax.ShapeDtypeStruct((grp.n, wq.shape[1]), bf16),
        compiler_params=_params("arbitrary"),
    )(x, gain.reshape(1, d), sh, sc, w_dq, q_gain.reshape(1, rq), wq, cs, sn)


def _attn_kernel(q_ref, kn_ref, v_ref, kpe_ref, o_ref, m_scr, acc_scr, *, hg, tq, c):
    qi = pl.program_id(2)
    m_scr[...] = jnp.full_like(m_scr, -jnp.inf)
    acc_scr[...] = jnp.zeros_like(acc_scr)
    ones = jnp.ones((tq, LANE), bf16)
    n_lt = tq // LANE

    def block(ks, masked):
        kpe = kpe_ref[pl.ds(ks, tq), :]
        for h in range(hg):
            hs = slice(h * LANE, (h + 1) * LANE)
            k = jnp.concatenate([kn_ref[pl.ds(ks, tq), hs], kpe], axis=-1)
            s = lax.dot_general(q_ref[:, 2 * LANE * h:2 * LANE * (h + 1)], k, (((1,), (1,)), ((), ())),
                                preferred_element_type=f32)
            if masked:
                row = lax.broadcasted_iota(jnp.int32, (tq, tq), 0)
                col = lax.broadcasted_iota(jnp.int32, (tq, tq), 1)
                s = jnp.where(col <= row, s, -jnp.inf)
            tiles = [s[:, t * LANE:(t + 1) * LANE] for t in range(n_lt)]
            mx = tiles[0]
            for t in tiles[1:]:
                mx = jnp.maximum(mx, t)
            m = m_scr[h]
            m_new = jnp.maximum(m, jnp.max(mx, axis=-1, keepdims=True))
            p = jnp.concatenate([jnp.exp2((t - m_new) * c) for t in tiles], axis=-1).astype(bf16)
            corr = jnp.exp2((m - m_new) * c)
            v_aug = jnp.concatenate([v_ref[pl.ds(ks, tq), hs], ones], axis=-1)
            pv = jnp.dot(p, v_aug, preferred_element_type=f32)
            acc_scr[h, :, :LANE] = acc_scr[h, :, :LANE] * corr + pv[:, :LANE]
            acc_scr[h, :, LANE:] = acc_scr[h, :, LANE:] * corr + pv[:, LANE:]
            m_scr[h] = m_new

    def body(kb, carry):
        block(pl.multiple_of(kb * tq, tq), False)
        return carry

    lax.fori_loop(0, qi, body, 0)
    block(pl.multiple_of(qi * tq, tq), True)
    for h in range(hg):
        o_ref[:, h * LANE:(h + 1) * LANE] = (acc_scr[h, :, :LANE] * (1.0 / acc_scr[h, :, LANE:])).astype(o_ref.dtype)


def _attn_prompt(q, kn, v, kpe, nb_seq, seq, n_heads, scale, tq, hg):
    nq = seq // tq
    return pl.pallas_call(
        functools.partial(_attn_kernel, hg=hg, tq=tq, c=scale * LOG2E),
        grid=(nb_seq, n_heads // hg, nq),
        in_specs=[pl.BlockSpec((tq, hg * 2 * LANE), lambda b, g, i: (b * nq + i, g)),
                  pl.BlockSpec((seq, hg * LANE), lambda b, g, i: (b, g)),
                  pl.BlockSpec((seq, hg * LANE), lambda b, g, i: (b, g)),
                  pl.BlockSpec((seq, LANE), lambda b, g, i: (b, 0))],
        out_specs=pl.BlockSpec((tq, hg * LANE), lambda b, g, i: (b * nq + i, g)),
        out_shape=jax.ShapeDtypeStruct((nb_seq * seq, n_heads * LANE), bf16),
        scratch_shapes=[pltpu.VMEM((hg, tq, LANE), f32), pltpu.VMEM((hg, tq, 2 * LANE), f32)],
        compiler_params=_params("arbitrary", "arbitrary", "arbitrary"),
    )(q, kn, v, kpe)


def _head_matmul_kernel(a_ref, w_ref, o_ref):
    o_ref[...] = jnp.dot(a_ref[...].astype(bf16), w_ref[...].astype(bf16),
                         preferred_element_type=f32).astype(o_ref.dtype)


def _head_matmul(a, a_spec, w, w_spec, out_shape, out_spec, n_heads):
    return pl.pallas_call(
        _head_matmul_kernel,
        grid=(n_heads,),
        in_specs=[a_spec, w_spec],
        out_specs=out_spec,
        out_shape=out_shape,
        compiler_params=_params("arbitrary"),
    )(a, w)


def _decode_kernel(pt_ref, qlat_ref, qpe_ref, cnew_ref, knew_ref, cache_c, cache_k, o_ref,
                   cbuf, kbuf, sem, m_scr, l_scr, acc_scr, *, bp, ppc, n_chunks, n_pages, steps, c):
    g = pl.program_id(0)
    ch = pl.program_id(1)
    step = g * n_chunks + ch
    slot = step % 2
    total = pl.num_programs(0) * n_chunks
    page = cbuf.shape[3]
    rope = kbuf.shape[2]

    def copies(gg, cc, sl, bi, p):
        phys = pt_ref[(gg * bp + bi) * n_pages + cc * ppc + p]
        return (pltpu.make_async_copy(cache_c.at[phys], cbuf.at[sl, bi, p], sem.at[0, sl]),
                pltpu.make_async_copy(cache_k.at[phys], kbuf.at[sl, bi, :, pl.ds(p * page, page)], sem.at[1, sl]))

    def issue(gg, cc, sl):
        for bi in range(bp):
            for p in range(ppc):
                for cp in copies(gg, cc, sl, bi, p):
                    cp.start()

    @pl.when(step == 0)
    def _():
        issue(0, 0, 0)

    @pl.when(step + 1 < total)
    def _():
        nxt = step + 1
        issue(nxt // n_chunks, nxt % n_chunks, 1 - slot)

    for bi in range(bp):
        for p in range(ppc):
            for cp in copies(g, ch, slot, bi, p):
                cp.wait()

    @pl.when(ch == 0)
    def _():
        m_scr[...] = jnp.full_like(m_scr, -jnp.inf)
        l_scr[...] = jnp.zeros_like(l_scr)
        acc_scr[...] = jnp.zeros_like(acc_scr)

    nt = (((1,), (1,)), ((), ()))

    def update(state, s, vals):
        m, l, acc = state
        m_new = jnp.maximum(m, jnp.max(s, axis=-1, keepdims=True))
        p = jnp.exp2((s - m_new) * c)
        corr = jnp.exp2((m - m_new) * c)
        l = l * corr + jnp.sum(p, axis=-1, keepdims=True)
        acc = acc * corr + jnp.dot(p.astype(bf16), vals, preferred_element_type=f32)
        return m_new, l, acc

    states = [(m_scr[bi], l_scr[bi], acc_scr[bi]) for bi in range(bp)]
    for bi in range(bp):
        ck = cbuf[slot, bi].reshape(ppc * page, cbuf.shape[4]).astype(bf16)
        s = (lax.dot_general(qlat_ref[bi], ck, nt, preferred_element_type=f32)
             + jnp.dot(qpe_ref[bi, :, :rope], kbuf[slot, bi].astype(bf16), preferred_element_type=f32))
        states[bi] = update(states[bi], s, ck)
    for bi in range(bp):
        m_scr[bi], l_scr[bi], acc_scr[bi] = states[bi]

    @pl.when(ch == n_chunks - 1)
    def _():
        for bi in range(bp):
            cn = cnew_ref[bi]
            sn = (lax.dot_general(qlat_ref[bi], cn, nt, preferred_element_type=f32)
                  + lax.dot_general(qpe_ref[bi], knew_ref[bi], nt, preferred_element_type=f32))
            t_row = lax.broadcasted_iota(jnp.int32, sn.shape, 0) % steps
            key = lax.broadcasted_iota(jnp.int32, sn.shape, 1)
            _, l, acc = update(states[bi], jnp.where(key <= t_row, sn, -jnp.inf), cn)
            o_ref[bi] = (acc * (1.0 / l)).astype(o_ref.dtype)


def _decode_attn(page_table, qlat, qpe, cnew, knew, cache_c, cache_k, scale, steps, ppc, bp):
    nb, rows, r_kv = qlat.shape
    n_pages = page_table.shape[1]
    page = cache_c.shape[1]
    rope = cache_k.shape[1]
    n_chunks = n_pages // ppc

    def seq_spec(a, b):
        return pl.BlockSpec((bp, a, b), lambda g, ch, pt: (g, 0, 0))

    gs = pltpu.PrefetchScalarGridSpec(
        num_scalar_prefetch=1,
        grid=(nb // bp, n_chunks),
        in_specs=[seq_spec(rows, r_kv), seq_spec(rows, LANE), seq_spec(LANE, r_kv), seq_spec(LANE, LANE),
                  pl.BlockSpec(memory_space=pl.ANY),
                  pl.BlockSpec(memory_space=pl.ANY)],
        out_specs=seq_spec(rows, r_kv),
        scratch_shapes=[pltpu.VMEM((2, bp, ppc, page, r_kv), f32),
                        pltpu.VMEM((2, bp, rope, ppc * page), f32),
                        pltpu.SemaphoreType.DMA((2, 2)),
                        pltpu.VMEM((bp, rows, 1), f32), pltpu.VMEM((bp, rows, 1), f32),
                        pltpu.VMEM((bp, rows, r_kv), f32)],
    )
    return pl.pallas_call(
        functools.partial(_decode_kernel, bp=bp, ppc=ppc, n_chunks=n_chunks, n_pages=n_pages, steps=steps,
                          c=scale * LOG2E),
        grid_spec=gs,
        out_shape=jax.ShapeDtypeStruct((nb, rows, r_kv), bf16),
        compiler_params=_params("arbitrary", "arbitrary"),
    )(page_table.reshape(-1), qlat, qpe, cnew, knew, cache_c, cache_k)


def _rope_tables(pos, rope):
    half = rope // 2
    inv_freq = ROPE_THETA ** (-jnp.arange(half, dtype=f32) / half)
    ang = pos.astype(f32)[:, None] * inv_freq
    cos, sin = jnp.cos(ang), jnp.sin(ang)
    pad = jnp.zeros((pos.shape[0], LANE - rope), f32)
    return (jnp.concatenate([cos, cos, pad], axis=-1), jnp.concatenate([-sin, sin, pad], axis=-1))


def kernel(x_prompt, x_sample, cache_ckv, cache_kpe, state_conv, state_rglru, page_table, c_prompt, c_sample, mod_w, mod_b, mix_norm, ffn_norm, rg_w_in, rg_conv_w, rg_conv_b, rg_w_gates, rg_b_gates, rg_lambda, rg_w_out, kv_mod_w, kv_mod_b, kv_norm, kv_w_dkv, kv_latent_norm, kv_w_uk, kv_w_uv, mla_w_dq, mla_q_norm, mla_w_uq, mla_w_o, moe_w_group, moe_b_group, moe_w_expert, moe_b_expert, moe_w13, moe_w2, final_norm):
    bp, seq, d = x_prompt.shape
    bs, steps, _ = x_sample.shape
    d_rnn = rg_conv_w.shape[-1]
    cwid = rg_conv_w.shape[1]
    nb_rnn, rnn_blk = rg_w_gates.shape[1], rg_w_gates.shape[2]
    r_kv, n_heads, d_nope = kv_w_uk.shape
    d_v = kv_w_uv.shape[2]
    rope = cache_kpe.shape[-1]
    half = rope // 2
    page = cache_ckv.shape[1]
    past = page_table.shape[1] * page
    n_groups = moe_w_group.shape[-1]
    n_exp = moe_w_expert.shape[-1]
    scale = 1.0 / math.sqrt(d_nope + rope)
    assert d_nope == LANE and d_v == LANE and rope <= LANE and d_rnn % LANE == 0
    assert rg_w_in.shape[0] == 1 and mla_w_dq.shape[0] == 1 and mod_w.shape[0] == 2
    assert steps >= cwid - 1 and seq >= cwid - 1

    n_p, n_s = bp * seq, steps * bs
    gp = _Group(n_p, min(512, seq), seq, False)
    gs_ = _Group(n_s, min(512, n_s), 0, True)
    groups = (gp, gs_)

    xp = x_prompt.reshape(n_p, d)
    xs = jnp.swapaxes(x_sample, 0, 1).reshape(n_s, d)

    n_c = bp + bs
    n_c_pad = -(-n_c // SUBLANE) * SUBLANE
    c_all = jnp.concatenate([c_prompt, c_sample, jnp.zeros((n_c_pad - n_c, d), f32)], axis=0)
    mods = [_bias_matmul(c_all, mod_w, layer, mod_b[layer]) for layer in range(2)]
    kv_mods = _bias_matmul(c_all, kv_mod_w[None], 0, kv_mod_b)

    def mod_vec(m, idx, grp_i):
        rows = slice(0, bp) if grp_i == 0 else slice(bp, bp + bs)
        return m[rows, idx * d:(idx + 1) * d]

    w_in = rg_w_in[0].astype(bf16)
    w_out = rg_w_out[0].astype(bf16)
    starts, win = _gate_windows(d_rnn, rnn_blk)
    wg = _gate_weights(rg_w_gates[0], starts, win)
    bgr = rg_b_gates[0][:, :rnn_blk].reshape(1, d_rnn)
    bgi = rg_b_gates[0][:, rnn_blk:].reshape(1, d_rnn)
    conv_w = rg_conv_w[0]
    conv_b = rg_conv_b[0].reshape(1, d_rnn)
    lam = rg_lambda[0].reshape(1, d_rnn)

    w_dkv = jnp.concatenate([kv_w_dkv, jnp.zeros((d, LANE - rope), f32)], axis=1).astype(bf16)
    w_up = jnp.concatenate([kv_w_uk.reshape(r_kv, n_heads * d_nope), kv_w_uv.reshape(r_kv, n_heads * d_v)],
                           axis=1).astype(bf16)
    w_dq = mla_w_dq[0].astype(bf16)
    wq3 = mla_w_uq[0].reshape(-1, n_heads, d_nope + rope)
    wq = jnp.concatenate([wq3, jnp.zeros((wq3.shape[0], n_heads, LANE - rope), f32)], axis=-1)
    wq = wq.reshape(-1, n_heads * 2 * LANE).astype(bf16)
    w_o = mla_w_o[0].astype(bf16)
    w_uk_t = jnp.transpose(kv_w_uk, (1, 2, 0))
    w_uv_flat = kv_w_uv.reshape(r_kv, n_heads * d_v)

    cos_p, sin_p = _rope_tables(jnp.arange(seq, dtype=jnp.int32), rope)
    cos_s, sin_s = _rope_tables(past + jnp.arange(steps, dtype=jnp.int32), rope)
    tabs = ((cos_p, sin_p), (cos_s, sin_s))

    def moe(xs_in, layer):
        rt_l = []
        n_pad = LANE - n_groups - n_exp
        wr = jnp.concatenate([moe_w_group[layer], moe_w_expert[layer], jnp.zeros((d, n_pad), f32)], axis=1)
        br = jnp.concatenate([moe_b_group[layer], moe_b_expert[layer], jnp.zeros((n_pad,), f32)]).reshape(1, LANE)
        xn_all, row0 = None, 0
        for gi, grp in enumerate(groups):
            xn_all, rt = _router(xs_in[gi], ffn_norm[layer], mod_vec(mods[layer], 3, gi),
                                 mod_vec(mods[layer], 4, gi), wr, br, grp._replace(tm=min(ROUTER_ROWS, grp.tm)),
                                 n_groups, n_exp, n_p + n_s, row0, xn_all)
            row0 += grp.n
            rt_l.append(rt)
        route = jnp.concatenate(rt_l, axis=0)
        expert_ids = route[:, :TOP_K].astype(jnp.int32)
        weights = route[:, TOP_K:2 * TOP_K]
        slot_tok, block_expert, n_used, next_expert, block_rows, dest = _dispatch(expert_ids, n_exp)
        yb = _experts(xn_all, slot_tok, block_expert, n_used, next_expert, block_rows, moe_w13, moe_w2, layer)
        out = []
        lo = 0
        for grp in groups:
            dg = dest[lo:lo + grp.n]
            out.append((yb[dg[:, 0]], yb[dg[:, 1]], weights[lo:lo + grp.n]))
            lo += grp.n
        return out

    x_cur = [xp, xs]
    projs, h_last = [], []
    for gi, grp in enumerate(groups):
        proj = _normmod_matmul(x_cur[gi], mix_norm[0], mod_vec(mods[0], 0, gi), mod_vec(mods[0], 1, gi),
                               w_in, grp, tn_cap=2688)
        projs.append(proj)
        if gi == 0:
            hg, hl = _rglru_prompt(proj, jnp.zeros((bp, 1, d_rnn), f32), jnp.zeros((bp, SUBLANE, d_rnn), f32),
                                   conv_w, conv_b, wg, bgr, bgi, lam, starts, win, bp, seq, min(RNN_ROWS, seq))
            hl = hl.reshape(bp, d_rnn)
        else:
            cst = jnp.swapaxes(state_conv[0], 0, 1)
            hg, hl = _rglru_sample(proj.reshape(steps, bs, 2 * d_rnn), state_rglru[0], cst, conv_w, conv_b,
                                   wg, bgr, bgi, lam, starts, win)
            hg = hg.reshape(n_s, d_rnn)
        h_last.append(hl)
        x_cur[gi] = _matmul_res(hg, w_out, x_cur[gi], mod_vec(mods[0], 2, gi), grp)
    moe_out = moe(x_cur, 0)
    for gi, grp in enumerate(groups):
        ya, yc, wts = moe_out[gi]
        x_cur[gi] = _combine(x_cur[gi], ya, yc, wts, mod_vec(mods[0], 5, gi), grp)

    kv_out = []
    for gi, grp in enumerate(groups):
        kv_out.append(_kv_latent(x_cur[gi], kv_norm, kv_mods[:, :d][(slice(0, bp) if gi == 0 else slice(bp, bp + bs))],
                                 kv_mods[:, d:][(slice(0, bp) if gi == 0 else slice(bp, bp + bs))],
                                 w_dkv, kv_latent_norm, tabs[gi][0], tabs[gi][1], grp, half,
                                 w_up=w_up if gi == 0 else None))
    qs = [_q_proj(x_cur[gi], mix_norm[1], mod_vec(mods[1], 0, gi), mod_vec(mods[1], 1, gi), w_dq,
                  mla_q_norm[0], wq, tabs[gi][0], tabs[gi][1], grp, n_heads, half)
          for gi, grp in enumerate(groups)]

    ckv_p, kpe_p, kn_p, v_p, kpeb_p = kv_out[0]
    tq = min(ATTN_ROWS, seq)
    hgrp = 4 if n_heads % 4 == 0 else 1
    attn_p = _attn_prompt(qs[0], kn_p, v_p, kpeb_p, bp, seq, n_heads, scale, tq, hgrp)

    ckv_s, kpe_s = kv_out[1]
    rows = n_heads * steps
    qlat = _head_matmul(
        qs[1], pl.BlockSpec((n_s, LANE), lambda h: (0, 2 * h)),
        w_uk_t, pl.BlockSpec((None, d_nope, r_kv), lambda h: (h, 0, 0)),
        jax.ShapeDtypeStruct((n_heads, n_s, r_kv), bf16),
        pl.BlockSpec((None, n_s, r_kv), lambda h: (h, 0, 0)), n_heads)
    qlat = jnp.transpose(qlat.reshape(n_heads, steps, bs, r_kv), (2, 0, 1, 3)).reshape(bs, rows, r_kv)
    qpe = qs[1].reshape(steps, bs, n_heads, 2, LANE)[:, :, :, 1]
    qpe = jnp.transpose(qpe, (1, 2, 0, 3)).reshape(bs, rows, LANE)

    def new_keys(a):
        a = jnp.swapaxes(a.reshape(steps, bs, a.shape[-1]), 0, 1).astype(bf16)
        return jnp.concatenate([a, jnp.zeros((bs, LANE - steps, a.shape[-1]), bf16)], axis=1)

    ppc = _pick_tile(page_table.shape[1], DECODE_PAGES, 1)
    olat = _decode_attn(page_table, qlat, qpe, new_keys(ckv_s), new_keys(kpe_s), cache_ckv,
                        jnp.swapaxes(cache_kpe, 1, 2), scale, steps, ppc,
                        DECODE_BATCH if bs % DECODE_BATCH == 0 else 1)
    olat = jnp.transpose(olat.reshape(bs, n_heads, steps, r_kv), (1, 2, 0, 3)).reshape(n_heads, n_s, r_kv)
    attn_s = _head_matmul(
        olat, pl.BlockSpec((None, n_s, r_kv), lambda h: (h, 0, 0)),
        w_uv_flat, pl.BlockSpec((r_kv, d_v), lambda h: (0, h)),
        jax.ShapeDtypeStruct((n_s, n_heads * d_v), bf16),
        pl.BlockSpec((n_s, d_v), lambda h: (0, h)), n_heads)

    attn = [attn_p, attn_s]
    for gi, grp in enumerate(groups):
        x_cur[gi] = _matmul_res(attn[gi], w_o, x_cur[gi], mod_vec(mods[1], 2, gi), grp)
    moe_out = moe(x_cur, 1)
    ys = []
    for gi, grp in enumerate(groups):
        ya, yc, wts = moe_out[gi]
        ys.append(_combine(x_cur[gi], ya, yc, wts, mod_vec(mods[1], 5, gi), grp, final_gain=final_norm))

    y_prompt = ys[0].reshape(bp, seq, d)
    y_sample = jnp.swapaxes(ys[1].reshape(steps, bs, d), 0, 1)
    conv_p = projs[0].reshape(bp, seq, 2 * d_rnn)[:, seq - (cwid - 1):, d_rnn:][None]
    conv_s = jnp.swapaxes(projs[1].reshape(steps, bs, 2 * d_rnn)[steps - (cwid - 1):, :, d_rnn:], 0, 1)[None]
    h_p = h_last[0][None]
    h_s = h_last[1][None]
    ckv_prompt = ckv_p.reshape(bp, seq, r_kv)
    kpe_prompt = kpe_p[:, :rope].reshape(bp, seq, rope)
    ckv_sample = jnp.swapaxes(ckv_s.reshape(steps, bs, r_kv), 0, 1)
    kpe_sample = jnp.swapaxes(kpe_s[:, :rope].reshape(steps, bs, rope), 0, 1)
    return (y_prompt, y_sample, conv_p, h_p, ckv_prompt, kpe_prompt, conv_s, h_s, ckv_sample, kpe_sample)
```

```python
import functools
import math
from typing import NamedTuple

import jax
import jax.numpy as jnp
from jax import lax
from jax.experimental import pallas as pl
from jax.experimental.pallas import tpu as pltpu

EPS = 1e-6
LRU_C = 8.0
ROPE_THETA = 10000.0
TOP_K = 2
LANE = 128
SUBLANE = 8
VMEM_LIMIT = 56 * 1024 * 1024
MOE_ROWS = 128
MOE_GROUP = 32
ATTN_ROWS = 512
RNN_ROWS = 128
ROUTER_ROWS = 256
SAMPLE_ROWS = 64
DECODE_PAGES = 32
DECODE_BATCH = 1
LOG2E = 1.4426950408889634

f32 = jnp.float32
bf16 = jnp.bfloat16


def _params(*sem):
    return pltpu.CompilerParams(dimension_semantics=sem, vmem_limit_bytes=VMEM_LIMIT)


def _pick_tile(n, cap, mult=LANE):
    if n <= cap:
        return n
    best = None
    for t in range(mult, cap + 1, mult):
        if n % t == 0:
            best = t
    assert best is not None, (n, cap, mult)
    return best


class _Group(NamedTuple):
    n: int
    tm: int
    seq: int
    per_token: bool


class _Mod(NamedTuple):
    src: jax.Array
    idx: int
    d: int


def _mod_operand(mod, grp):
    idx = mod.idx
    if grp.per_token:
        return mod.src, pl.BlockSpec((grp.tm, mod.d), lambda i, *_: (i, idx))
    per_seq = grp.seq // grp.tm
    return mod.src, pl.BlockSpec((None, 1, mod.d), lambda i, *_: (i // per_seq, 0, idx))


def _pos_operand(tab, grp):
    if grp.per_token:
        arr = jnp.repeat(tab, grp.n // tab.shape[0], axis=0)
        return arr, pl.BlockSpec((grp.tm, LANE), lambda i, *_: (i, 0))
    per_seq = grp.seq // grp.tm
    return tab, pl.BlockSpec((grp.tm, LANE), lambda i, *_: (i % per_seq, 0))


def _row_spec(grp, width):
    return pl.BlockSpec((grp.tm, width), lambda i, *_: (i, 0))


def _full_spec(shape):
    nd = len(shape)
    return pl.BlockSpec(shape, lambda *_: (0,) * nd)


def _rms(x, gain):
    return x * lax.rsqrt(jnp.mean(x * x, axis=-1, keepdims=True) + EPS) * gain


def _normmod(x, gain, shift, scale):
    return _rms(x, gain) * (1.0 + scale) + shift


def _rope_tile(t, cos, sin, half):
    lane = lax.broadcasted_iota(jnp.int32, t.shape, 1)
    rot = jnp.where(lane < half, pltpu.roll(t, LANE - half, 1), pltpu.roll(t, half, 1))
    return t * cos + rot * sin


def _sigmoid(x):
    return 0.5 * (jnp.tanh(0.5 * x) + 1.0)


def _gelu(x):
    return 0.5 * x * (1.0 + jnp.tanh(math.sqrt(2.0 / math.pi) * (x + 0.044715 * (x * x * x))))


def _softplus(z):
    return jnp.maximum(z, 0.0) + jnp.log1p(jnp.exp(-jnp.abs(z)))


def _bias_matmul_kernel(a_ref, w_ref, b_ref, o_ref):
    o_ref[...] = jnp.dot(a_ref[...].astype(bf16), w_ref[...].astype(bf16),
                         preferred_element_type=f32) + b_ref[...]


def _bias_matmul(a, w3, layer, b):
    m, k = a.shape
    n = w3.shape[-1]
    tn = _pick_tile(n, 1024)
    return pl.pallas_call(
        _bias_matmul_kernel,
        grid=(n // tn,),
        in_specs=[_full_spec((m, k)),
                  pl.BlockSpec((None, k, tn), lambda j: (layer, 0, j)),
                  pl.BlockSpec((1, tn), lambda j: (0, j))],
        out_specs=pl.BlockSpec((m, tn), lambda j: (0, j)),
        out_shape=jax.ShapeDtypeStruct((m, n), f32),
        compiler_params=_params("arbitrary"),
    )(a, w3, b.reshape(1, n))


def _normmod_matmul_kernel(x_ref, gain_ref, shift_ref, scale_ref, w_ref, o_ref, xn_ref):
    @pl.when(pl.program_id(1) == 0)
    def _():
        xn_ref[...] = _normmod(x_ref[...], gain_ref[...], shift_ref[...], scale_ref[...]).astype(bf16)

    o_ref[...] = jnp.dot(xn_ref[...], w_ref[...], preferred_element_type=f32).astype(o_ref.dtype)


def _normmod_matmul(x, gain, shift, scale, w, grp, tn_cap=1024, out_dtype=f32):
    d = x.shape[1]
    n = w.shape[1]
    tn = _pick_tile(n, tn_cap)
    sh, sh_spec = _mod_operand(shift, grp)
    sc, sc_spec = _mod_operand(scale, grp)
    return pl.pallas_call(
        _normmod_matmul_kernel,
        grid=(grp.n // grp.tm, n // tn),
        in_specs=[_row_spec(grp, d), _full_spec((1, d)), sh_spec, sc_spec,
                  pl.BlockSpec((d, tn), lambda i, j: (0, j))],
        out_specs=pl.BlockSpec((grp.tm, tn), lambda i, j: (i, j)),
        out_shape=jax.ShapeDtypeStruct((grp.n, n), out_dtype),
        scratch_shapes=[pltpu.VMEM((grp.tm, d), bf16)],
        compiler_params=_params("arbitrary", "arbitrary"),
    )(x, gain.reshape(1, d), sh, sc, w)


def _matmul_res_kernel(a_ref, w_ref, res_ref, gate_ref, o_ref):
    y = jnp.dot(a_ref[...].astype(bf16), w_ref[...], preferred_element_type=f32)
    o_ref[...] = res_ref[...] + gate_ref[...] * y


def _matmul_res(a, w, res, gate, grp, tn_cap=1024):
    k = a.shape[1]
    n = w.shape[1]
    tn = _pick_tile(n, tn_cap)
    g_arr = gate.src
    j0 = gate.idx * (n // tn)
    if grp.per_token:
        g_spec = pl.BlockSpec((grp.tm, tn), lambda j, i: (i, j0 + j))
    else:
        per_seq = grp.seq // grp.tm
        g_spec = pl.BlockSpec((None, 1, tn), lambda j, i: (i // per_seq, 0, j0 + j))
    return pl.pallas_call(
        _matmul_res_kernel,
        grid=(n // tn, grp.n // grp.tm),
        in_specs=[pl.BlockSpec((grp.tm, k), lambda j, i: (i, 0)),
                  pl.BlockSpec((k, tn), lambda j, i: (0, j)),
                  pl.BlockSpec((grp.tm, tn), lambda j, i: (i, j)),
                  g_spec],
        out_specs=pl.BlockSpec((grp.tm, tn), lambda j, i: (i, j)),
        out_shape=jax.ShapeDtypeStruct((grp.n, n), f32),
        compiler_params=_params("arbitrary", "arbitrary"),
    )(a, w, res, g_arr)


def _gate_windows(d_rnn, blk):
    nt = d_rnn // LANE
    raw, need = [], 0
    for j in range(nt):
        n0 = (LANE * j) // blk
        n1 = (LANE * j + LANE - 1) // blk
        s = (blk * n0) // LANE * LANE
        raw.append(s)
        need = max(need, blk * (n1 + 1) - s)
    win = min(d_rnn, -(-need // LANE) * LANE)
    return [min(s, d_rnn - win) for s in raw], win


def _gate_weights(w_gates, starts, win):
    nb, blk, _ = w_gates.shape
    wb = w_gates.astype(bf16)
    tiles = []
    for j, s in enumerate(starts):
        halves = []
        for off in (0, blk):
            acc = None
            for n in range((LANE * j) // blk, (LANE * j + LANE - 1) // blk + 1):
                c_lo = max(blk * n, LANE * j)
                c_hi = min(blk * (n + 1), LANE * (j + 1))
                r_off = blk * n - s
                assert 0 <= r_off and r_off + blk <= win
                piece = wb[n, :, off + c_lo - blk * n:off + c_hi - blk * n]
                piece = jnp.pad(piece, ((r_off, win - blk - r_off), (c_lo - LANE * j, LANE * (j + 1) - c_hi)))
                acc = piece if acc is None else acc + piece
            halves.append(acc)
        tiles.append(jnp.concatenate(halves, axis=1))
    return jnp.stack(tiles)


def _lru_inputs(g, b_r, b_i, neg_c_sp, u):
    r = _sigmoid(g[:, :LANE] + b_r)
    i = _sigmoid(g[:, LANE:] + b_i)
    log_a = neg_c_sp * r
    a = jnp.exp(log_a)
    x = jnp.sqrt(-jnp.tanh(log_a) * (a * a + 1.0)) * (i * u)
    return a, x


def _rglru_prompt_kernel(proj_ref, h0_ref, cinit_ref, cw_ref, cb_ref, wg_ref, bgr_ref, bgi_ref, lam_ref,
                         hg_ref, hlast_ref, ubuf, ucf, ucb, a_scr, x_scr, hcar,
                         *, starts, win, tc, d_rnn, cwid):
    t = pl.program_id(1)
    hist = SUBLANE

    @pl.when(t == 0)
    def _():
        ubuf[0:hist, :] = cinit_ref[...]
        hcar[...] = h0_ref[...]

    ubuf[hist:hist + tc, :] = proj_ref[:, d_rnn:]
    uc = cb_ref[...]
    for k in range(cwid):
        off = hist - (cwid - 1) + k
        uc = uc + cw_ref[k:k + 1, :] * ubuf[off:off + tc, :]
    ucf[...] = uc
    ucb[...] = uc.astype(bf16)
    ubuf[0:hist, :] = ubuf[tc:tc + hist, :]

    neg_c_sp = -LRU_C * _softplus(-lam_ref[...])
    seg = tc // SUBLANE
    for j in range(d_rnn // LANE):
        cs = slice(j * LANE, (j + 1) * LANE)
        g = jnp.dot(ucb[:, starts[j]:starts[j] + win], wg_ref[j], preferred_element_type=f32)
        a, x = _lru_inputs(g, bgr_ref[:, cs], bgi_ref[:, cs], neg_c_sp[:, cs], ucf[:, cs])
        a_scr[j] = a
        x_scr[j] = x
        h = jnp.zeros((SUBLANE, LANE), f32)
        p = jnp.ones((SUBLANE, LANE), f32)
        for k in range(seg):
            ak = a_scr[j, pl.ds(k, SUBLANE, stride=seg), :]
            xk = x_scr[j, pl.ds(k, SUBLANE, stride=seg), :]
            h = ak * h + xk
            p = p * ak
            x_scr[j, pl.ds(k, SUBLANE, stride=seg), :] = h
            a_scr[j, pl.ds(k, SUBLANE, stride=seg), :] = p
        c = hcar[:, cs]
        outs = []
        for r in range(SUBLANE):
            rows = slice(r * seg, (r + 1) * seg)
            outs.append(x_scr[j, rows, :] + a_scr[j, rows, :] * c)
            c = p[r:r + 1, :] * c + h[r:r + 1, :]
        hcar[:, cs] = c
        hs = jnp.concatenate(outs, axis=0)
        hg_ref[:, cs] = (hs * _gelu(proj_ref[:, cs])).astype(hg_ref.dtype)
    hlast_ref[...] = hcar[...]


def _rglru_prompt(proj, h0, conv_init, conv_w, conv_b, wg, bgr, bgi, lam, starts, win, nb_seq, seq, tc):
    d_rnn = proj.shape[1] // 2
    cwid = conv_w.shape[0]
    nt = d_rnn // LANE
    per_seq = seq // tc
    kern = functools.partial(_rglru_prompt_kernel, starts=tuple(starts), win=win, tc=tc, d_rnn=d_rnn, cwid=cwid)
    return pl.pallas_call(
        kern,
        grid=(nb_seq, per_seq),
        in_specs=[pl.BlockSpec((tc, 2 * d_rnn), lambda b, t: (b * per_seq + t, 0)),
                  pl.BlockSpec((None, 1, d_rnn), lambda b, t: (b, 0, 0)),
                  pl.BlockSpec((None, SUBLANE, d_rnn), lambda b, t: (b, 0, 0)),
                  _full_spec((cwid, d_rnn)), _full_spec((1, d_rnn)),
                  _full_spec(wg.shape), _full_spec((1, d_rnn)), _full_spec((1, d_rnn)), _full_spec((1, d_rnn))],
        out_specs=[pl.BlockSpec((tc, d_rnn), lambda b, t: (b * per_seq + t, 0)),
                   pl.BlockSpec((None, 1, d_rnn), lambda b, t: (b, 0, 0))],
        out_shape=[jax.ShapeDtypeStruct((nb_seq * seq, d_rnn), bf16),
                   jax.ShapeDtypeStruct((nb_seq, 1, d_rnn), f32)],
        scratch_shapes=[pltpu.VMEM((tc + SUBLANE, d_rnn), f32),
                        pltpu.VMEM((tc, d_rnn), f32),
                        pltpu.VMEM((tc, d_rnn), bf16),
                        pltpu.VMEM((nt, tc, LANE), f32),
                        pltpu.VMEM((nt, tc, LANE), f32),
                        pltpu.VMEM((1, d_rnn), f32)],
        compiler_params=_params("arbitrary", "arbitrary"),
    )(proj, h0, conv_init, conv_w, conv_b, wg, bgr, bgi, lam)


def _rglru_sample_kernel(proj_ref, h0_ref, cst_ref, cw_ref, cb_ref, wg_ref, bgr_ref, bgi_ref, lam_ref,
                         hg_ref, hlast_ref, ucf, ucb, *, starts, win, steps, tb, d_rnn, cwid):
    def up(tp):
        if tp < cwid - 1:
            return cst_ref[tp]
        return proj_ref[tp - cwid + 1, :, d_rnn:]

    for t in range(steps):
        uc = cb_ref[...]
        for k in range(cwid):
            uc = uc + cw_ref[k:k + 1, :] * up(t + k)
        ucf[t * tb:(t + 1) * tb, :] = uc
        ucb[t * tb:(t + 1) * tb, :] = uc.astype(bf16)

    neg_c_sp = -LRU_C * _softplus(-lam_ref[...])
    for j in range(d_rnn // LANE):
        cs = slice(j * LANE, (j + 1) * LANE)
        g = jnp.dot(ucb[:, starts[j]:starts[j] + win], wg_ref[j], preferred_element_type=f32)
        h = h0_ref[:, cs]
        for t in range(steps):
            rows = slice(t * tb, (t + 1) * tb)
            a, x = _lru_inputs(g[rows], bgr_ref[:, cs], bgi_ref[:, cs], neg_c_sp[:, cs], ucf[rows, cs])
            h = a * h + x
            hg_ref[t, :, cs] = (h * _gelu(proj_ref[t, :, cs])).astype(hg_ref.dtype)
        hlast_ref[:, cs] = h


def _rglru_sample(proj, h0, conv_state, conv_w, conv_b, wg, bgr, bgi, lam, starts, win):
    steps, nb, _ = proj.shape
    d_rnn = proj.shape[2] // 2
    cwid = conv_w.shape[0]
    tb = min(SAMPLE_ROWS, nb)
    kern = functools.partial(_rglru_sample_kernel, starts=tuple(starts), win=win, steps=steps, tb=tb,
                             d_rnn=d_rnn, cwid=cwid)
    return pl.pallas_call(
        kern,
        grid=(nb // tb,),
        in_specs=[pl.BlockSpec((steps, tb, 2 * d_rnn), lambda i: (0, i, 0)),
                  pl.BlockSpec((tb, d_rnn), lambda i: (i, 0)),
                  pl.BlockSpec((cwid - 1, tb, d_rnn), lambda i: (0, i, 0)),
                  _full_spec((cwid, d_rnn)), _full_spec((1, d_rnn)),
                  _full_spec(wg.shape), _full_spec((1, d_rnn)), _full_spec((1, d_rnn)), _full_spec((1, d_rnn))],
        out_specs=[pl.BlockSpec((steps, tb, d_rnn), lambda i: (0, i, 0)),
                   pl.BlockSpec((tb, d_rnn), lambda i: (i, 0))],
        out_shape=[jax.ShapeDtypeStruct((steps, nb, d_rnn), bf16),
                   jax.ShapeDtypeStruct((nb, d_rnn), f32)],
        scratch_shapes=[pltpu.VMEM((steps * tb, d_rnn), f32), pltpu.VMEM((steps * tb, d_rnn), bf16)],
        compiler_params=_params("arbitrary"),
    )(proj, h0, conv_state, conv_w, conv_b, wg, bgr, bgi, lam)


def _first_max(vals, ids):
    m = jnp.max(vals, axis=-1, keepdims=True)
    idx = jnp.min(jnp.where(vals == m, ids, jnp.int32(2 ** 30)), axis=-1, keepdims=True)
    return m, idx


def _router_kernel(x_ref, gain_ref, shift_ref, scale_ref, w_ref, b_ref, *rest, n_groups, n_exp):
    xn_ref, rt_ref = rest[-2:]
    xn = _normmod(x_ref[...], gain_ref[...], shift_ref[...], scale_ref[...])
    for s in range(xn_ref.shape[1]):
        xn_ref[:, s, :] = xn[:, s * LANE:(s + 1) * LANE]
    xh = xn.astype(bf16)
    xl = (xn - xh.astype(f32)).astype(bf16)
    ph = jnp.dot(xh, w_ref[...], preferred_element_type=f32)
    pl_ = jnp.dot(xl, w_ref[...], preferred_element_type=f32)
    lg = (ph[:, :LANE] + pl_[:, :LANE] + ph[:, LANE:] + pl_[:, LANE:]) + b_ref[...]
    per = n_exp // n_groups
    lane = lax.broadcasted_iota(jnp.int32, lg.shape, 1)
    neg = -jnp.inf
    gl = jnp.where(lane < n_groups, lg, neg)
    gmax, g_idx = _first_max(gl, lane)
    g_w = 1.0 / jnp.sum(jnp.exp(gl - gmax), axis=-1, keepdims=True)
    e_id = lane - n_groups
    lo = g_idx * per
    el = jnp.where(e_id >= lo, jnp.where(e_id < lo + per, lg, neg), neg)
    l1, i1 = _first_max(el, e_id)
    el2 = jnp.where(e_id == i1, neg, el)
    l2, i2 = _first_max(el2, e_id)
    e = jnp.exp(l2 - l1)
    w1 = g_w / (1.0 + e)
    w2 = w1 * e
    rt_ref[...] = jnp.where(lane == 0, i1.astype(f32),
                            jnp.where(lane == 1, i2.astype(f32),
                                      jnp.where(lane == 2, w1, jnp.where(lane == 3, w2, 0.0))))


def _router(x, gain, shift, scale, wr, br, grp, n_groups, n_exp, n_total, row0, xn_all=None):
    d = x.shape[1]
    nr = wr.shape[1]
    wh = wr.astype(bf16)
    wl = (wr - wh.astype(f32)).astype(bf16)
    sh, sh_spec = _mod_operand(shift, grp)
    sc, sc_spec = _mod_operand(scale, grp)
    assert row0 % grp.tm == 0 and n_total % grp.tm == 0
    t0 = row0 // grp.tm
    n_tiles = grp.n // grp.tm
    n_steps = n_tiles if xn_all is not None else n_total // grp.tm

    def clamp(spec):
        return pl.BlockSpec(spec.block_shape, lambda i: spec.index_map(jnp.minimum(i, n_tiles - 1)))

    args = [x, gain.reshape(1, d), sh, sc, jnp.concatenate([wh, wl], axis=1), br]
    specs = [clamp(_row_spec(grp, d)), _full_spec((1, d)), clamp(sh_spec), clamp(sc_spec),
             _full_spec((d, 2 * nr)), _full_spec((1, nr))]
    aliases = {}
    if xn_all is not None:
        aliases = {len(args): 0}
        args.append(xn_all)
        specs.append(pl.BlockSpec(memory_space=pl.ANY))
    return pl.pallas_call(
        functools.partial(_router_kernel, n_groups=n_groups, n_exp=n_exp),
        grid=(n_steps,),
        in_specs=specs,
        out_specs=[pl.BlockSpec((grp.tm, d // LANE, LANE), lambda i: (i + t0, 0, 0)),
                   clamp(_row_spec(grp, nr))],
        out_shape=[jax.ShapeDtypeStruct((n_total, d // LANE, LANE), f32),
                   jax.ShapeDtypeStruct((grp.n, nr), f32)],
        input_output_aliases=aliases,
        compiler_params=_params("arbitrary"),
    )(*args)


def _experts_kernel(be_ref, nu_ref, st_ref, nx_ref, bc_ref, xn_hbm, w13_hbm, w2_hbm, o_ref,
                    xbuf, sem, w13f, w2f, wsem, w13b, w2b, *, d_exp, layer):
    i = pl.program_id(0)
    n_used = nu_ref[0]
    slot = i % 2
    e = be_ref[i]
    prev = be_ref[jnp.maximum(i - 1, 0)]
    n_slab = xbuf.shape[2]

    def row_copy(tok, sl, r):
        return pltpu.make_async_copy(xn_hbm.at[tok], xbuf.at[sl, r], sem.at[sl])

    def row_groups(blk, fn):
        for g0 in range(0, MOE_ROWS, MOE_GROUP):
            @pl.when(g0 < bc_ref[blk])
            def _(g0=g0):
                for r in range(g0, g0 + MOE_GROUP):
                    fn(r)

    def gather(blk, sl):
        row_groups(blk, lambda r: row_copy(st_ref[blk * MOE_ROWS + r], sl, r).start())

    def weight_copies(ex):
        return (pltpu.make_async_copy(w13_hbm.at[layer, ex], w13f, wsem.at[0]),
                pltpu.make_async_copy(w2_hbm.at[layer, ex], w2f, wsem.at[1]))

    @pl.when((i == 0) & (n_used > 0))
    def _():
        xbuf[...] = jnp.zeros_like(xbuf)
        gather(0, 0)
        for cp in weight_copies(e):
            cp.start(priority=1)

    @pl.when(i + 1 < n_used)
    def _():
        gather(i + 1, 1 - slot)

    @pl.when((i < n_used) & ((i == 0) | (e != prev)))
    def _():
        for cp in weight_copies(e):
            cp.wait()
        w13b[...] = w13f[...].astype(bf16)
        w2b[...] = w2f[...].astype(bf16)

        @pl.when(nx_ref[i] >= 0)
        def _():
            for cp in weight_copies(nx_ref[i]):
                cp.start(priority=1)

    @pl.when(i < n_used)
    def _():
        row_groups(i, lambda r: row_copy(0, slot, r).wait())
        x = jnp.concatenate([xbuf[slot, :, s, :] for s in range(n_slab)], axis=-1).astype(bf16)
        gu = jnp.dot(x, w13b[...], preferred_element_type=f32)
        g = gu[:, :d_exp]
        act = (g * _sigmoid(g)) * gu[:, d_exp:]
        o_ref[...] = jnp.dot(act.astype(bf16), w2b[...], preferred_element_type=f32)

    @pl.when(i >= n_used)
    def _():
        o_ref[...] = jnp.zeros_like(o_ref)


def _experts(xn, slot_tok, block_expert, n_used, next_expert, block_rows, w13, w2, layer):
    n_slab = xn.shape[1]
    d = n_slab * LANE
    n_blocks = slot_tok.shape[0] // MOE_ROWS
    d_exp = w2.shape[2]
    gs = pltpu.PrefetchScalarGridSpec(
        num_scalar_prefetch=5,
        grid=(n_blocks,),
        in_specs=[pl.BlockSpec(memory_space=pl.ANY), pl.BlockSpec(memory_space=pl.ANY),
                  pl.BlockSpec(memory_space=pl.ANY)],
        out_specs=pl.BlockSpec((MOE_ROWS, d), lambda i, *_: (i, 0)),
        scratch_shapes=[pltpu.VMEM((2, MOE_ROWS, n_slab, LANE), f32), pltpu.SemaphoreType.DMA((2,)),
                        pltpu.VMEM((d, 2 * d_exp), f32), pltpu.VMEM((d_exp, d), f32),
                        pltpu.SemaphoreType.DMA((2,)),
                        pltpu.VMEM((d, 2 * d_exp), bf16), pltpu.VMEM((d_exp, d), bf16)],
    )
    return pl.pallas_call(
        functools.partial(_experts_kernel, d_exp=d_exp, layer=layer),
        grid_spec=gs,
        out_shape=jax.ShapeDtypeStruct((n_blocks * MOE_ROWS, d), f32),
        compiler_params=_params("arbitrary"),
    )(block_expert, n_used, slot_tok, next_expert, block_rows, xn, w13, w2)


def _combine_kernel(x_ref, ya_ref, yb_ref, wt_ref, gate_ref, *rest, final):
    if final:
        fg_ref, o_ref = rest
    else:
        (o_ref,) = rest
    wt = wt_ref[...]
    y = x_ref[...] + gate_ref[...] * (wt[:, 0:1] * ya_ref[...] + wt[:, 1:2] * yb_ref[...])
    if final:
        y = _rms(y, fg_ref[...])
    o_ref[...] = y


def _combine(x, ya, yb, wts, gate, grp, final_gain=None):
    d = x.shape[1]
    g_arr, g_spec = _mod_operand(gate, grp)
    final = final_gain is not None
    args = [x, ya, yb, wts, g_arr]
    specs = [_row_spec(grp, d), _row_spec(grp, d), _row_spec(grp, d), _row_spec(grp, TOP_K), g_spec]
    if final:
        args.append(final_gain.reshape(1, d))
        specs.append(_full_spec((1, d)))
    return pl.pallas_call(
        functools.partial(_combine_kernel, final=final),
        grid=(grp.n // grp.tm,),
        in_specs=specs,
        out_specs=_row_spec(grp, d),
        out_shape=jax.ShapeDtypeStruct((grp.n, d), f32),
        compiler_params=_params("arbitrary"),
    )(*args)


def _dispatch(expert_ids, n_exp):
    n = expert_ids.shape[0]
    na = n * TOP_K
    flat_e = expert_ids.reshape(na)
    onehot = (flat_e[:, None] == jnp.arange(n_exp, dtype=jnp.int32)[None, :]).astype(jnp.int32)
    incl = jnp.cumsum(onehot, axis=0)
    counts = incl[-1]
    padded = (counts + MOE_ROWS - 1) // MOE_ROWS * MOE_ROWS
    pad_end = jnp.cumsum(padded)
    pad_start = pad_end - padded
    dest = jnp.sum(onehot * (pad_start[None, :] + incl - 1), axis=1)
    n_blocks = (na + MOE_ROWS - 1) // MOE_ROWS + n_exp
    block_start = jnp.arange(n_blocks, dtype=jnp.int32) * MOE_ROWS
    block_expert = jnp.minimum(jnp.sum((pad_end[None, :] <= block_start[:, None]).astype(jnp.int32), axis=1),
                               n_exp - 1)
    n_used = pad_end[-1:] // MOE_ROWS
    bidx = jnp.arange(n_blocks, dtype=jnp.int32)
    is_first = (bidx == 0) | (block_expert != jnp.roll(block_expert, 1))
    pos = jnp.where(is_first & (bidx < n_used[0]), bidx, n_blocks)
    nxt = jnp.min(jnp.where(bidx[None, :] > bidx[:, None], pos[None, :], n_blocks), axis=1)
    next_expert = jnp.where(nxt < n_blocks, block_expert[jnp.minimum(nxt, n_blocks - 1)], -1)
    sel = (block_expert[:, None] == jnp.arange(n_exp, dtype=jnp.int32)[None, :]).astype(jnp.int32)
    real_end = jnp.sum(sel * (pad_start + counts)[None, :], axis=1)
    block_rows = jnp.clip(real_end - block_start, 0, MOE_ROWS)
    tok = jnp.arange(na, dtype=jnp.int32) // TOP_K
    slot_tok = jnp.zeros((n_blocks * MOE_ROWS,), jnp.int32).at[dest].set(tok)
    return slot_tok, block_expert, n_used, next_expert, block_rows, dest.reshape(n, TOP_K)


def _kv_kernel(x_ref, gain_ref, shift_ref, scale_ref, w_ref, lg_ref, cos_ref, sin_ref, *rest,
               r_kv, half, with_up):
    if with_up:
        wup_ref, ckv_ref, kpe_ref, kn_ref, v_ref, kpeb_ref = rest
    else:
        ckv_ref, kpe_ref = rest
    a = _normmod(x_ref[...], gain_ref[...], shift_ref[...], scale_ref[...]).astype(bf16)
    kv = jnp.dot(a, w_ref[...], preferred_element_type=f32)
    c = _rms(kv[:, :r_kv], lg_ref[...])
    ckv_ref[...] = c
    kp = _rope_tile(kv[:, r_kv:], cos_ref[...], sin_ref[...], half)
    kpe_ref[...] = kp
    if with_up:
        up = jnp.dot(c.astype(bf16), wup_ref[...], preferred_element_type=f32)
        hw = up.shape[1] // 2
        kn_ref[...] = up[:, :hw].astype(bf16)
        v_ref[...] = up[:, hw:].astype(bf16)
        kpeb_ref[...] = kp.astype(bf16)


def _kv_latent(x, gain, shift, scale, w_pad, latent_gain, cos, sin, grp, half, w_up=None):
    d = x.shape[1]
    r_kv = latent_gain.shape[0]
    with_up = w_up is not None
    sh, sh_spec = _mod_operand(shift, grp)
    sc, sc_spec = _mod_operand(scale, grp)
    cs, cs_spec = _pos_operand(cos, grp)
    sn, sn_spec = _pos_operand(sin, grp)
    args = [x, gain.reshape(1, d), sh, sc, w_pad, latent_gain.reshape(1, r_kv), cs, sn]
    specs = [_row_spec(grp, d), _full_spec((1, d)), sh_spec, sc_spec, _full_spec(w_pad.shape),
             _full_spec((1, r_kv)), cs_spec, sn_spec]
    out_specs = [_row_spec(grp, r_kv), _row_spec(grp, LANE)]
    out_shape = [jax.ShapeDtypeStruct((grp.n, r_kv), f32), jax.ShapeDtypeStruct((grp.n, LANE), f32)]
    if with_up:
        hw = w_up.shape[1] // 2
        args.append(w_up)
        specs.append(_full_spec(w_up.shape))
        out_specs += [_row_spec(grp, hw), _row_spec(grp, hw), _row_spec(grp, LANE)]
        out_shape += [jax.ShapeDtypeStruct((grp.n, hw), bf16), jax.ShapeDtypeStruct((grp.n, hw), bf16),
                      jax.ShapeDtypeStruct((grp.n, LANE), bf16)]
    return pl.pallas_call(
        functools.partial(_kv_kernel, r_kv=r_kv, half=half, with_up=with_up),
        grid=(grp.n // grp.tm,),
        in_specs=specs, out_specs=out_specs, out_shape=out_shape,
        compiler_params=_params("arbitrary"),
    )(*args)


def _q_kernel(x_ref, gain_ref, shift_ref, scale_ref, wdq_ref, qg_ref, wq_ref, cos_ref, sin_ref, q_ref,
              *, n_heads, half):
    a = _normmod(x_ref[...], gain_ref[...], shift_ref[...], scale_ref[...]).astype(bf16)
    ql = jnp.dot(a, wdq_ref[...], preferred_element_type=f32)
    qn = _rms(ql, qg_ref[...]).astype(bf16)
    cos = cos_ref[...]
    sin = sin_ref[...]
    for h in range(n_heads):
        q = jnp.dot(qn, wq_ref[:, 2 * LANE * h:2 * LANE * (h + 1)], preferred_element_type=f32)
        q_ref[:, 2 * LANE * h:2 * LANE * h + LANE] = q[:, :LANE].astype(bf16)
        q_ref[:, 2 * LANE * h + LANE:2 * LANE * (h + 1)] = _rope_tile(q[:, LANE:], cos, sin, half).astype(bf16)


def _q_proj(x, gain, shift, scale, w_dq, q_gain, wq, cos, sin, grp, n_heads, half):
    d = x.shape[1]
    rq = w_dq.shape[1]
    sh, sh_spec = _mod_operand(shift, grp)
    sc, sc_spec = _mod_operand(scale, grp)
    cs, cs_spec = _pos_operand(cos, grp)
    sn, sn_spec = _pos_operand(sin, grp)
    return pl.pallas_call(
        functools.partial(_q_kernel, n_heads=n_heads, half=half),
        grid=(grp.n // grp.tm,),
        in_specs=[_row_spec(grp, d), _full_spec((1, d)), sh_spec, sc_spec, _full_spec(w_dq.shape),
                  _full_spec((1, rq)), _full_spec(wq.shape), cs_spec, sn_spec],
        out_specs=_row_spec(grp, wq.shape[1]),
        out_shape=jax.ShapeDtypeStruct((grp.n, wq.shape[1]), bf16),
        compiler_params=_params("arbitrary"),
    )(x, gain.reshape(1, d), sh, sc, w_dq, q_gain.reshape(1, rq), wq, cs, sn)


def _attn_kernel(q_ref, kn_ref, v_ref, kpe_ref, o_ref, m_scr, acc_scr, *, hg, tq, c):
    qi = pl.program_id(2)
    m_scr[...] = jnp.full_like(m_scr, -jnp.inf)
    acc_scr[...] = jnp.zeros_like(acc_scr)
    ones = jnp.ones((tq, LANE), bf16)
    n_lt = tq // LANE

    def block(ks, masked):
        kpe = kpe_ref[pl.ds(ks, tq), :]
        for h in range(hg):
            hs = slice(h * LANE, (h + 1) * LANE)
            k = jnp.concatenate([kn_ref[pl.ds(ks, tq), hs], kpe], axis=-1)
            s = lax.dot_general(q_ref[:, 2 * LANE * h:2 * LANE * (h + 1)], k, (((1,), (1,)), ((), ())),
                                preferred_element_type=f32)
            if masked:
                row = lax.broadcasted_iota(jnp.int32, (tq, tq), 0)
                col = lax.broadcasted_iota(jnp.int32, (tq, tq), 1)
                s = jnp.where(col <= row, s, -jnp.inf)
            tiles = [s[:, t * LANE:(t + 1) * LANE] for t in range(n_lt)]
            mx = tiles[0]
            for t in tiles[1:]:
                mx = jnp.maximum(mx, t)
            m = m_scr[h]
            m_new = jnp.maximum(m, jnp.max(mx, axis=-1, keepdims=True))
            p = jnp.concatenate([jnp.exp2((t - m_new) * c) for t in tiles], axis=-1).astype(bf16)
            corr = jnp.exp2((m - m_new) * c)
            v_aug = jnp.concatenate([v_ref[pl.ds(ks, tq), hs], ones], axis=-1)
            pv = jnp.dot(p, v_aug, preferred_element_type=f32)
            acc_scr[h, :, :LANE] = acc_scr[h, :, :LANE] * corr + pv[:, :LANE]
            acc_scr[h, :, LANE:] = acc_scr[h, :, LANE:] * corr + pv[:, LANE:]
            m_scr[h] = m_new

    def body(kb, carry):
        block(pl.multiple_of(kb * tq, tq), False)
        return carry

    lax.fori_loop(0, qi, body, 0)
    block(pl.multiple_of(qi * tq, tq), True)
    for h in range(hg):
        o_ref[:, h * LANE:(h + 1) * LANE] = (acc_scr[h, :, :LANE] * (1.0 / acc_scr[h, :, LANE:])).astype(o_ref.dtype)


def _attn_prompt(q, kn, v, kpe, nb_seq, seq, n_heads, scale, tq, hg):
    nq = seq // tq
    return pl.pallas_call(
        functools.partial(_attn_kernel, hg=hg, tq=tq, c=scale * LOG2E),
        grid=(nb_seq, n_heads // hg, nq),
        in_specs=[pl.BlockSpec((tq, hg * 2 * LANE), lambda b, g, i: (b * nq + i, g)),
                  pl.BlockSpec((seq, hg * LANE), lambda b, g, i: (b, g)),
                  pl.BlockSpec((seq, hg * LANE), lambda b, g, i: (b, g)),
                  pl.BlockSpec((seq, LANE), lambda b, g, i: (b, 0))],
        out_specs=pl.BlockSpec((tq, hg * LANE), lambda b, g, i: (b * nq + i, g)),
        out_shape=jax.ShapeDtypeStruct((nb_seq * seq, n_heads * LANE), bf16),
        scratch_shapes=[pltpu.VMEM((hg, tq, LANE), f32), pltpu.VMEM((hg, tq, 2 * LANE), f32)],
        compiler_params=_params("arbitrary", "arbitrary", "arbitrary"),
    )(q, kn, v, kpe)


def _head_matmul_kernel(a_ref, w_ref, o_ref):
    o_ref[...] = jnp.dot(a_ref[...].astype(bf16), w_ref[...].astype(bf16),
                         preferred_element_type=f32).astype(o_ref.dtype)


def _head_matmul(a, a_spec, w, w_spec, out_shape, out_spec, n_heads):
    return pl.pallas_call(
        _head_matmul_kernel,
        grid=(n_heads,),
        in_specs=[a_spec, w_spec],
        out_specs=out_spec,
        out_shape=out_shape,
        compiler_params=_params("arbitrary"),
    )(a, w)


def _decode_kernel(pt_ref, qlat_ref, qpe_ref, cnew_ref, knew_ref, cache_c, cache_k, o_ref,
                   cbuf, kbuf, sem, m_scr, l_scr, acc_scr, *, bp, ppc, n_chunks, n_pages, steps, c):
    g = pl.program_id(0)
    ch = pl.program_id(1)
    step = g * n_chunks + ch
    slot = step % 2
    total = pl.num_programs(0) * n_chunks
    page = cbuf.shape[3]
    rope = kbuf.shape[2]

    def copies(gg, cc, sl, bi, p):
        phys = pt_ref[(gg * bp + bi) * n_pages + cc * ppc + p]
        return (pltpu.make_async_copy(cache_c.at[phys], cbuf.at[sl, bi, p], sem.at[0, sl]),
                pltpu.make_async_copy(cache_k.at[phys], kbuf.at[sl, bi, :, pl.ds(p * page, page)], sem.at[1, sl]))

    def issue(gg, cc, sl):
        for bi in range(bp):
            for p in range(ppc):
                for cp in copies(gg, cc, sl, bi, p):
                    cp.start()

    @pl.when(step == 0)
    def _():
        issue(0, 0, 0)

    @pl.when(step + 1 < total)
    def _():
        nxt = step + 1
        issue(nxt // n_chunks, nxt % n_chunks, 1 - slot)

    for bi in range(bp):
        for p in range(ppc):
            for cp in copies(g, ch, slot, bi, p):
                cp.wait()

    @pl.when(ch == 0)
    def _():
        m_scr[...] = jnp.full_like(m_scr, -jnp.inf)
        l_scr[...] = jnp.zeros_like(l_scr)
        acc_scr[...] = jnp.zeros_like(acc_scr)

    nt = (((1,), (1,)), ((), ()))

    def update(state, s, vals):
        m, l, acc = state
        m_new = jnp.maximum(m, jnp.max(s, axis=-1, keepdims=True))
        p = jnp.exp2((s - m_new) * c)
        corr = jnp.exp2((m - m_new) * c)
        l = l * corr + jnp.sum(p, axis=-1, keepdims=True)
        acc = acc * corr + jnp.dot(p.astype(bf16), vals, preferred_element_type=f32)
        return m_new, l, acc

    states = [(m_scr[bi], l_scr[bi], acc_scr[bi]) for bi in range(bp)]
    for bi in range(bp):
        ck = cbuf[slot, bi].reshape(ppc * page, cbuf.shape[4]).astype(bf16)
        s = (lax.dot_general(qlat_ref[bi], ck, nt, preferred_element_type=f32)
             + jnp.dot(qpe_ref[bi, :, :rope], kbuf[slot, bi].astype(bf16), preferred_element_type=f32))
        states[bi] = update(states[bi], s, ck)
    for bi in range(bp):
        m_scr[bi], l_scr[bi], acc_scr[bi] = states[bi]

    @pl.when(ch == n_chunks - 1)
    def _():
        for bi in range(bp):
            cn = cnew_ref[bi]
            sn = (lax.dot_general(qlat_ref[bi], cn, nt, preferred_element_type=f32)
                  + lax.dot_general(qpe_ref[bi], knew_ref[bi], nt, preferred_element_type=f32))
            t_row = lax.broadcasted_iota(jnp.int32, sn.shape, 0) % steps
            key = lax.broadcasted_iota(jnp.int32, sn.shape, 1)
            _, l, acc = update(states[bi], jnp.where(key <= t_row, sn, -jnp.inf), cn)
            o_ref[bi] = (acc * (1.0 / l)).astype(o_ref.dtype)


def _decode_attn(page_table, qlat, qpe, cnew, knew, cache_c, cache_k, scale, steps, ppc, bp):
    nb, rows, r_kv = qlat.shape
    n_pages = page_table.shape[1]
    page = cache_c.shape[1]
    rope = cache_k.shape[1]
    n_chunks = n_pages // ppc

    def seq_spec(a, b):
        return pl.BlockSpec((bp, a, b), lambda g, ch, pt: (g, 0, 0))

    gs = pltpu.PrefetchScalarGridSpec(
        num_scalar_prefetch=1,
        grid=(nb // bp, n_chunks),
        in_specs=[seq_spec(rows, r_kv), seq_spec(rows, LANE), seq_spec(LANE, r_kv), seq_spec(LANE, LANE),
                  pl.BlockSpec(memory_space=pl.ANY),
                  pl.BlockSpec(memory_space=pl.ANY)],
        out_specs=seq_spec(rows, r_kv),
        scratch_shapes=[pltpu.VMEM((2, bp, ppc, page, r_kv), f32),
                        pltpu.VMEM((2, bp, rope, ppc * page), f32),
                        pltpu.SemaphoreType.DMA((2, 2)),
                        pltpu.VMEM((bp, rows, 1), f32), pltpu.VMEM((bp, rows, 1), f32),
                        pltpu.VMEM((bp, rows, r_kv), f32)],
    )
    return pl.pallas_call(
        functools.partial(_decode_kernel, bp=bp, ppc=ppc, n_chunks=n_chunks, n_pages=n_pages, steps=steps,
                          c=scale * LOG2E),
        grid_spec=gs,
        out_shape=jax.ShapeDtypeStruct((nb, rows, r_kv), bf16),
        compiler_params=_params("arbitrary", "arbitrary"),
    )(page_table.reshape(-1), qlat, qpe, cnew, knew, cache_c, cache_k)


def _rope_tables(pos, rope):
    half = rope // 2
    inv_freq = ROPE_THETA ** (-jnp.arange(half, dtype=f32) / half)
    ang = pos.astype(f32)[:, None] * inv_freq
    cos, sin = jnp.cos(ang), jnp.sin(ang)
    pad = jnp.zeros((pos.shape[0], LANE - rope), f32)
    return (jnp.concatenate([cos, cos, pad], axis=-1), jnp.concatenate([-sin, sin, pad], axis=-1))


def kernel(x_prompt, x_sample, cache_ckv, cache_kpe, state_conv, state_rglru, page_table, c_prompt, c_sample, mod_w, mod_b, mix_norm, ffn_norm, rg_w_in, rg_conv_w, rg_conv_b, rg_w_gates, rg_b_gates, rg_lambda, rg_w_out, kv_mod_w, kv_mod_b, kv_norm, kv_w_dkv, kv_latent_norm, kv_w_uk, kv_w_uv, mla_w_dq, mla_q_norm, mla_w_uq, mla_w_o, moe_w_group, moe_b_group, moe_w_expert, moe_b_expert, moe_w13, moe_w2, final_norm):
    bp, seq, d = x_prompt.shape
    bs, steps, _ = x_sample.shape
    d_rnn = rg_conv_w.shape[-1]
    cwid = rg_conv_w.shape[1]
    nb_rnn, rnn_blk = rg_w_gates.shape[1], rg_w_gates.shape[2]
    r_kv, n_heads, d_nope = kv_w_uk.shape
    d_v = kv_w_uv.shape[2]
    rope = cache_kpe.shape[-1]
    half = rope // 2
    page = cache_ckv.shape[1]
    past = page_table.shape[1] * page
    n_groups = moe_w_group.shape[-1]
    n_exp = moe_w_expert.shape[-1]
    scale = 1.0 / math.sqrt(d_nope + rope)
    assert d_nope == LANE and d_v == LANE and rope <= LANE and d_rnn % LANE == 0
    assert rg_w_in.shape[0] == 1 and mla_w_dq.shape[0] == 1 and mod_w.shape[0] == 2
    assert steps >= cwid - 1 and seq >= cwid - 1

    n_p, n_s = bp * seq, steps * bs
    gp = _Group(n_p, min(512, seq), seq, False)
    gs_ = _Group(n_s, min(512, n_s), 0, True)
    groups = (gp, gs_)

    xp = x_prompt.reshape(n_p, d)
    xs = jnp.swapaxes(x_sample, 0, 1).reshape(n_s, d)

    n_c = bp + bs
    n_c_pad = -(-n_c // SUBLANE) * SUBLANE
    c_all = jnp.concatenate([c_prompt, c_sample, jnp.zeros((n_c_pad - n_c, d), f32)], axis=0)
    def by_group(m):
        return (m[:bp, None, :], jnp.tile(m[bp:bp + bs], (steps, 1)))

    mods = [by_group(_bias_matmul(c_all, mod_w, layer, mod_b[layer])) for layer in range(2)]
    kv_mods = by_group(_bias_matmul(c_all, kv_mod_w[None], 0, kv_mod_b))

    def mod_vec(m, idx, grp_i):
        return _Mod(m[grp_i], idx, d)

    w_in = rg_w_in[0].astype(bf16)
    w_out = rg_w_out[0].astype(bf16)
    starts, win = _gate_windows(d_rnn, rnn_blk)
    wg = _gate_weights(rg_w_gates[0], starts, win)
    bgr = rg_b_gates[0][:, :rnn_blk].reshape(1, d_rnn)
    bgi = rg_b_gates[0][:, rnn_blk:].reshape(1, d_rnn)
    conv_w = rg_conv_w[0]
    conv_b = rg_conv_b[0].reshape(1, d_rnn)
    lam = rg_lambda[0].reshape(1, d_rnn)

    w_dkv = jnp.concatenate([kv_w_dkv, jnp.zeros((d, LANE - rope), f32)], axis=1).astype(bf16)
    w_up = jnp.concatenate([kv_w_uk.reshape(r_kv, n_heads * d_nope), kv_w_uv.reshape(r_kv, n_heads * d_v)],
                           axis=1).astype(bf16)
    w_dq = mla_w_dq[0].astype(bf16)
    wq3 = mla_w_uq[0].reshape(-1, n_heads, d_nope + rope)
    wq = jnp.concatenate([wq3, jnp.zeros((wq3.shape[0], n_heads, LANE - rope), f32)], axis=-1)
    wq = wq.reshape(-1, n_heads * 2 * LANE).astype(bf16)
    w_o = mla_w_o[0].astype(bf16)
    w_uk_t = jnp.transpose(kv_w_uk, (1, 2, 0))
    w_uv_flat = kv_w_uv.reshape(r_kv, n_heads * d_v)

    cos_p, sin_p = _rope_tables(jnp.arange(seq, dtype=jnp.int32), rope)
    cos_s, sin_s = _rope_tables(past + jnp.arange(steps, dtype=jnp.int32), rope)
    tabs = ((cos_p, sin_p), (cos_s, sin_s))

    def moe(xs_in, layer):
        rt_l = []
        n_pad = LANE - n_groups - n_exp
        wr = jnp.concatenate([moe_w_group[layer], moe_w_expert[layer], jnp.zeros((d, n_pad), f32)], axis=1)
        br = jnp.concatenate([moe_b_group[layer], moe_b_expert[layer], jnp.zeros((n_pad,), f32)]).reshape(1, LANE)
        xn_all, row0 = None, 0
        for gi, grp in enumerate(groups):
            xn_all, rt = _router(xs_in[gi], ffn_norm[layer], mod_vec(mods[layer], 3, gi),
                                 mod_vec(mods[layer], 4, gi), wr, br, grp._replace(tm=min(ROUTER_ROWS, grp.tm)),
                                 n_groups, n_exp, n_p + n_s, row0, xn_all)
            row0 += grp.n
            rt_l.append(rt)
        route = jnp.concatenate(rt_l, axis=0)
        expert_ids = route[:, :TOP_K].astype(jnp.int32)
        weights = route[:, TOP_K:2 * TOP_K]
        slot_tok, block_expert, n_used, next_expert, block_rows, dest = _dispatch(expert_ids, n_exp)
        yb = _experts(xn_all, slot_tok, block_expert, n_used, next_expert, block_rows, moe_w13, moe_w2, layer)
        out = []
        lo = 0
        for grp in groups:
            dg = dest[lo:lo + grp.n]
            out.append((yb[dg[:, 0]], yb[dg[:, 1]], weights[lo:lo + grp.n]))
            lo += grp.n
        return out

    x_cur = [xp, xs]
    projs, h_last = [], []
    for gi, grp in enumerate(groups):
        proj = _normmod_matmul(x_cur[gi], mix_norm[0], mod_vec(mods[0], 0, gi), mod_vec(mods[0], 1, gi),
                               w_in, grp, tn_cap=2688)
        projs.append(proj)
        if gi == 0:
            hg, hl = _rglru_prompt(proj, jnp.zeros((bp, 1, d_rnn), f32), jnp.zeros((bp, SUBLANE, d_rnn), f32),
                                   conv_w, conv_b, wg, bgr, bgi, lam, starts, win, bp, seq, min(RNN_ROWS, seq))
            hl = hl.reshape(bp, d_rnn)
        else:
            cst = jnp.swapaxes(state_conv[0], 0, 1)
            hg, hl = _rglru_sample(proj.reshape(steps, bs, 2 * d_rnn), state_rglru[0], cst, conv_w, conv_b,
                                   wg, bgr, bgi, lam, starts, win)
            hg = hg.reshape(n_s, d_rnn)
        h_last.append(hl)
        x_cur[gi] = _matmul_res(hg, w_out, x_cur[gi], mod_vec(mods[0], 2, gi), grp)
    moe_out = moe(x_cur, 0)
    for gi, grp in enumerate(groups):
        ya, yc, wts = moe_out[gi]
        x_cur[gi] = _combine(x_cur[gi], ya, yc, wts, mod_vec(mods[0], 5, gi), grp)

    kv_out = []
    for gi, grp in enumerate(groups):
        kv_out.append(_kv_latent(x_cur[gi], kv_norm, mod_vec(kv_mods, 0, gi), mod_vec(kv_mods, 1, gi),
                                 w_dkv, kv_latent_norm, tabs[gi][0], tabs[gi][1], grp, half,
                                 w_up=w_up if gi == 0 else None))
    qs = [_q_proj(x_cur[gi], mix_norm[1], mod_vec(mods[1], 0, gi), mod_vec(mods[1], 1, gi), w_dq,
                  mla_q_norm[0], wq, tabs[gi][0], tabs[gi][1], grp, n_heads, half)
          for gi, grp in enumerate(groups)]

    ckv_p, kpe_p, kn_p, v_p, kpeb_p = kv_out[0]
    tq = min(ATTN_ROWS, seq)
    hgrp = 4 if n_heads % 4 == 0 else 1
    attn_p = _attn_prompt(qs[0], kn_p, v_p, kpeb_p, bp, seq, n_heads, scale, tq, hgrp)

    ckv_s, kpe_s = kv_out[1]
    rows = n_heads * steps
    qlat = _head_matmul(
        qs[1], pl.BlockSpec((n_s, LANE), lambda h: (0, 2 * h)),
        w_uk_t, pl.BlockSpec((None, d_nope, r_kv), lambda h: (h, 0, 0)),
        jax.ShapeDtypeStruct((n_heads, n_s, r_kv), bf16),
        pl.BlockSpec((None, n_s, r_kv), lambda h: (h, 0, 0)), n_heads)
    qlat = jnp.transpose(qlat.reshape(n_heads, steps, bs, r_kv), (2, 0, 1, 3)).reshape(bs, rows, r_kv)
    qpe = qs[1].reshape(steps, bs, n_heads, 2, LANE)[:, :, :, 1]
    qpe = jnp.transpose(qpe, (1, 2, 0, 3)).reshape(bs, rows, LANE)

    def new_keys(a):
        a = jnp.swapaxes(a.reshape(steps, bs, a.shape[-1]), 0, 1).astype(bf16)
        return jnp.concatenate([a, jnp.zeros((bs, LANE - steps, a.shape[-1]), bf16)], axis=1)

    ppc = _pick_tile(page_table.shape[1], DECODE_PAGES, 1)
    olat = _decode_attn(page_table, qlat, qpe, new_keys(ckv_s), new_keys(kpe_s), cache_ckv,
                        jnp.swapaxes(cache_kpe, 1, 2), scale, steps, ppc,
                        DECODE_BATCH if bs % DECODE_BATCH == 0 else 1)
    olat = jnp.transpose(olat.reshape(bs, n_heads, steps, r_kv), (1, 2, 0, 3)).reshape(n_heads, n_s, r_kv)
    attn_s = _head_matmul(
        olat, pl.BlockSpec((None, n_s, r_kv), lambda h: (h, 0, 0)),
        w_uv_flat, pl.BlockSpec((r_kv, d_v), lambda h: (0, h)),
        jax.ShapeDtypeStruct((n_s, n_heads * d_v), bf16),
        pl.BlockSpec((n_s, d_v), lambda h: (0, h)), n_heads)

    attn = [attn_p, attn_s]
    for gi, grp in enumerate(groups):
        x_cur[gi] = _matmul_res(attn[gi], w_o, x_cur[gi], mod_vec(mods[1], 2, gi), grp)
    moe_out = moe(x_cur, 1)
    ys = []
    for gi, grp in enumerate(groups):
        ya, yc, wts = moe_out[gi]
        ys.append(_combine(x_cur[gi], ya, yc, wts, mod_vec(mods[1], 5, gi), grp, final_gain=final_norm))

    y_prompt = ys[0].reshape(bp, seq, d)
    y_sample = jnp.swapaxes(ys[1].reshape(steps, bs, d), 0, 1)
    conv_p = projs[0].reshape(bp, seq, 2 * d_rnn)[:, seq - (cwid - 1):, d_rnn:][None]
    conv_s = jnp.swapaxes(projs[1].reshape(steps, bs, 2 * d_rnn)[steps - (cwid - 1):, :, d_rnn:], 0, 1)[None]
    h_p = h_last[0][None]
    h_s = h_last[1][None]
    ckv_prompt = ckv_p.reshape(bp, seq, r_kv)
    kpe_prompt = kpe_p[:, :rope].reshape(bp, seq, rope)
    ckv_sample = jnp.swapaxes(ckv_s.reshape(steps, bs, r_kv), 0, 1)
    kpe_sample = jnp.swapaxes(kpe_s[:, :rope].reshape(steps, bs, rope), 0, 1)
    return (y_prompt, y_sample, conv_p, h_p, ckv_prompt, kpe_prompt, conv_s, h_s, ckv_sample, kpe_sample)
```

```python
import functools
import math
from typing import NamedTuple

import jax
import jax.numpy as jnp
from jax import lax
from jax.experimental import pallas as pl
from jax.experimental.pallas import tpu as pltpu

EPS = 1e-6
LRU_C = 8.0
ROPE_THETA = 10000.0
TOP_K = 2
LANE = 128
SUBLANE = 8
VMEM_LIMIT = 56 * 1024 * 1024
MOE_ROWS = 128
MOE_GROUP = 32
ATTN_ROWS = 512
RNN_ROWS = 64
ROUTER_ROWS = 256
SAMPLE_ROWS = 64
DECODE_PAGES = 32
DECODE_BATCH = 1
LOG2E = 1.4426950408889634

f32 = jnp.float32
bf16 = jnp.bfloat16


def _params(*sem):
    return pltpu.CompilerParams(dimension_semantics=sem, vmem_limit_bytes=VMEM_LIMIT)


def _pick_tile(n, cap, mult=LANE):
    if n <= cap:
        return n
    best = None
    for t in range(mult, cap + 1, mult):
        if n % t == 0:
            best = t
    assert best is not None, (n, cap, mult)
    return best


class _Group(NamedTuple):
    n: int
    tm: int
    seq: int
    per_token: bool


def _mod_operand(vec, grp):
    d = vec.shape[-1]
    if grp.per_token:
        arr = jnp.tile(vec, (grp.n // vec.shape[0], 1))
        return arr, pl.BlockSpec((grp.tm, d), lambda i, *_: (i, 0))
    per_seq = grp.seq // grp.tm
    return vec[:, None, :], pl.BlockSpec((None, 1, d), lambda i, *_: (i // per_seq, 0, 0))


def _pos_operand(tab, grp):
    if grp.per_token:
        arr = jnp.repeat(tab, grp.n // tab.shape[0], axis=0)
        return arr, pl.BlockSpec((grp.tm, LANE), lambda i, *_: (i, 0))
    per_seq = grp.seq // grp.tm
    return tab, pl.BlockSpec((grp.tm, LANE), lambda i, *_: (i % per_seq, 0))


def _row_spec(grp, width):
    return pl.BlockSpec((grp.tm, width), lambda i, *_: (i, 0))


def _full_spec(shape):
    nd = len(shape)
    return pl.BlockSpec(shape, lambda *_: (0,) * nd)


def _rms(x, gain):
    return x * lax.rsqrt(jnp.mean(x * x, axis=-1, keepdims=True) + EPS) * gain


def _normmod(x, gain, shift, scale):
    return _rms(x, gain) * (1.0 + scale) + shift


def _rope_tile(t, cos, sin, half):
    lane = lax.broadcasted_iota(jnp.int32, t.shape, 1)
    rot = jnp.where(lane < half, pltpu.roll(t, LANE - half, 1), pltpu.roll(t, half, 1))
    return t * cos + rot * sin


def _sigmoid(x):
    return 0.5 * (jnp.tanh(0.5 * x) + 1.0)


def _gelu(x):
    return 0.5 * x * (1.0 + jnp.tanh(math.sqrt(2.0 / math.pi) * (x + 0.044715 * (x * x * x))))


def _softplus(z):
    return jnp.maximum(z, 0.0) + jnp.log1p(jnp.exp(-jnp.abs(z)))


def _bias_matmul_kernel(a_ref, w_ref, b_ref, o_ref):
    o_ref[...] = jnp.dot(a_ref[...].astype(bf16), w_ref[...].astype(bf16),
                         preferred_element_type=f32) + b_ref[...]


def _bias_matmul(a, w3, layer, b):
    m, k = a.shape
    n = w3.shape[-1]
    tn = _pick_tile(n, 1024)
    return pl.pallas_call(
        _bias_matmul_kernel,
        grid=(n // tn,),
        in_specs=[_full_spec((m, k)),
                  pl.BlockSpec((None, k, tn), lambda j: (layer, 0, j)),
                  pl.BlockSpec((1, tn), lambda j: (0, j))],
        out_specs=pl.BlockSpec((m, tn), lambda j: (0, j)),
        out_shape=jax.ShapeDtypeStruct((m, n), f32),
        compiler_params=_params("arbitrary"),
    )(a, w3, b.reshape(1, n))


def _normmod_matmul_kernel(x_ref, gain_ref, shift_ref, scale_ref, w_ref, o_ref, xn_ref):
    @pl.when(pl.program_id(1) == 0)
    def _():
        xn_ref[...] = _normmod(x_ref[...], gain_ref[...], shift_ref[...], scale_ref[...]).astype(bf16)

    o_ref[...] = jnp.dot(xn_ref[...], w_ref[...], preferred_element_type=f32).astype(o_ref.dtype)


def _normmod_matmul(x, gain, shift, scale, w, grp, tn_cap=1024, out_dtype=f32):
    d = x.shape[1]
    n = w.shape[1]
    tn = _pick_tile(n, tn_cap)
    sh, sh_spec = _mod_operand(shift, grp)
    sc, sc_spec = _mod_operand(scale, grp)
    return pl.pallas_call(
        _normmod_matmul_kernel,
        grid=(grp.n // grp.tm, n // tn),
        in_specs=[_row_spec(grp, d), _full_spec((1, d)), sh_spec, sc_spec,
                  pl.BlockSpec((d, tn), lambda i, j: (0, j))],
        out_specs=pl.BlockSpec((grp.tm, tn), lambda i, j: (i, j)),
        out_shape=jax.ShapeDtypeStruct((grp.n, n), out_dtype),
        scratch_shapes=[pltpu.VMEM((grp.tm, d), bf16)],
        compiler_params=_params("arbitrary", "arbitrary"),
    )(x, gain.reshape(1, d), sh, sc, w)


def _matmul_res_kernel(a_ref, w_ref, res_ref, gate_ref, o_ref):
    y = jnp.dot(a_ref[...].astype(bf16), w_ref[...], preferred_element_type=f32)
    o_ref[...] = res_ref[...] + gate_ref[...] * y


def _matmul_res(a, w, res, gate, grp, tn_cap=1024):
    k = a.shape[1]
    n = w.shape[1]
    tn = _pick_tile(n, tn_cap)
    if grp.per_token:
        g_arr = jnp.tile(gate, (grp.n // gate.shape[0], 1))
        g_spec = pl.BlockSpec((grp.tm, tn), lambda j, i: (i, j))
    else:
        per_seq = grp.seq // grp.tm
        g_arr = gate[:, None, :]
        g_spec = pl.BlockSpec((None, 1, tn), lambda j, i: (i // per_seq, 0, j))
    return pl.pallas_call(
        _matmul_res_kernel,
        grid=(n // tn, grp.n // grp.tm),
        in_specs=[pl.BlockSpec((grp.tm, k), lambda j, i: (i, 0)),
                  pl.BlockSpec((k, tn), lambda j, i: (0, j)),
                  pl.BlockSpec((grp.tm, tn), lambda j, i: (i, j)),
                  g_spec],
        out_specs=pl.BlockSpec((grp.tm, tn), lambda j, i: (i, j)),
        out_shape=jax.ShapeDtypeStruct((grp.n, n), f32),
        compiler_params=_params("arbitrary", "arbitrary"),
    )(a, w, res, g_arr)


def _gate_windows(d_rnn, blk):
    nt = d_rnn // LANE
    raw, need = [], 0
    for j in range(nt):
        n0 = (LANE * j) // blk
        n1 = (LANE * j + LANE - 1) // blk
        s = (blk * n0) // LANE * LANE
        raw.append(s)
        need = max(need, blk * (n1 + 1) - s)
    win = min(d_rnn, -(-need // LANE) * LANE)
    return [min(s, d_rnn - win) for s in raw], win


def _gate_weights(w_gates, starts, win):
    nb, blk, _ = w_gates.shape
    wb = w_gates.astype(bf16)
    tiles = []
    for j, s in enumerate(starts):
        halves = []
        for off in (0, blk):
            acc = None
            for n in range((LANE * j) // blk, (LANE * j + LANE - 1) // blk + 1):
                c_lo = max(blk * n, LANE * j)
                c_hi = min(blk * (n + 1), LANE * (j + 1))
                r_off = blk * n - s
                assert 0 <= r_off and r_off + blk <= win
                piece = wb[n, :, off + c_lo - blk * n:off + c_hi - blk * n]
                piece = jnp.pad(piece, ((r_off, win - blk - r_off), (c_lo - LANE * j, LANE * (j + 1) - c_hi)))
                acc = piece if acc is None else acc + piece
            halves.append(acc)
        tiles.append(jnp.concatenate(halves, axis=1))
    return jnp.stack(tiles)


def _lru_inputs(g, b_r, b_i, neg_c_sp, u):
    r = _sigmoid(g[:, :LANE] + b_r)
    i = _sigmoid(g[:, LANE:] + b_i)
    log_a = neg_c_sp * r
    a = jnp.exp(log_a)
    x = jnp.sqrt(-jnp.tanh(log_a) * (a * a + 1.0)) * (i * u)
    return a, x


def _rglru_prompt_kernel(proj_ref, h0_ref, cinit_ref, cw_ref, cb_ref, wg_ref, bgr_ref, bgi_ref, lam_ref,
                         hg_ref, hlast_ref, ubuf, ucf, ucb, a_scr, x_scr, hcar,
                         *, starts, win, tc, d_rnn, cwid):
    t = pl.program_id(1)
    hist = SUBLANE

    @pl.when(t == 0)
    def _():
        ubuf[0:hist, :] = cinit_ref[...]
        hcar[...] = h0_ref[...]

    ubuf[hist:hist + tc, :] = proj_ref[:, d_rnn:]
    uc = cb_ref[...]
    for k in range(cwid):
        off = hist - (cwid - 1) + k
        uc = uc + cw_ref[k:k + 1, :] * ubuf[off:off + tc, :]
    ucf[...] = uc
    ucb[...] = uc.astype(bf16)
    ubuf[0:hist, :] = ubuf[tc:tc + hist, :]

    neg_c_sp = -LRU_C * _softplus(-lam_ref[...])
    seg = tc // SUBLANE
    for j in range(d_rnn // LANE):
        cs = slice(j * LANE, (j + 1) * LANE)
        g = jnp.dot(ucb[:, starts[j]:starts[j] + win], wg_ref[j], preferred_element_type=f32)
        a, x = _lru_inputs(g, bgr_ref[:, cs], bgi_ref[:, cs], neg_c_sp[:, cs], ucf[:, cs])
        a_scr[j] = a
        x_scr[j] = x
        h = jnp.zeros((SUBLANE, LANE), f32)
        p = jnp.ones((SUBLANE, LANE), f32)
        for k in range(seg):
            ak = a_scr[j, pl.ds(k, SUBLANE, stride=seg), :]
            xk = x_scr[j, pl.ds(k, SUBLANE, stride=seg), :]
            h = ak * h + xk
            p = p * ak
            x_scr[j, pl.ds(k, SUBLANE, stride=seg), :] = h
            a_scr[j, pl.ds(k, SUBLANE, stride=seg), :] = p
        c = hcar[:, cs]
        outs = []
        for r in range(SUBLANE):
            rows = slice(r * seg, (r + 1) * seg)
            outs.append(x_scr[j, rows, :] + a_scr[j, rows, :] * c)
            c = p[r:r + 1, :] * c + h[r:r + 1, :]
        hcar[:, cs] = c
        hs = jnp.concatenate(outs, axis=0)
        hg_ref[:, cs] = (hs * _gelu(proj_ref[:, cs])).astype(hg_ref.dtype)
    hlast_ref[...] = hcar[...]


def _rglru_prompt(proj, h0, conv_init, conv_w, conv_b, wg, bgr, bgi, lam, starts, win, nb_seq, seq, tc):
    d_rnn = proj.shape[1] // 2
    cwid = conv_w.shape[0]
    nt = d_rnn // LANE
    per_seq = seq // tc
    kern = functools.partial(_rglru_prompt_kernel, starts=tuple(starts), win=win, tc=tc, d_rnn=d_rnn, cwid=cwid)
    return pl.pallas_call(
        kern,
        grid=(nb_seq, per_seq),
        in_specs=[pl.BlockSpec((tc, 2 * d_rnn), lambda b, t: (b * per_seq + t, 0)),
                  pl.BlockSpec((None, 1, d_rnn), lambda b, t: (b, 0, 0)),
                  pl.BlockSpec((None, SUBLANE, d_rnn), lambda b, t: (b, 0, 0)),
                  _full_spec((cwid, d_rnn)), _full_spec((1, d_rnn)),
                  _full_spec(wg.shape), _full_spec((1, d_rnn)), _full_spec((1, d_rnn)), _full_spec((1, d_rnn))],
        out_specs=[pl.BlockSpec((tc, d_rnn), lambda b, t: (b * per_seq + t, 0)),
                   pl.BlockSpec((None, 1, d_rnn), lambda b, t: (b, 0, 0))],
        out_shape=[jax.ShapeDtypeStruct((nb_seq * seq, d_rnn), bf16),
                   jax.ShapeDtypeStruct((nb_seq, 1, d_rnn), f32)],
        scratch_shapes=[pltpu.VMEM((tc + SUBLANE, d_rnn), f32),
                        pltpu.VMEM((tc, d_rnn), f32),
                        pltpu.VMEM((tc, d_rnn), bf16),
                        pltpu.VMEM((nt, tc, LANE), f32),
                        pltpu.VMEM((nt, tc, LANE), f32),
                        pltpu.VMEM((1, d_rnn), f32)],
        compiler_params=_params("arbitrary", "arbitrary"),
    )(proj, h0, conv_init, conv_w, conv_b, wg, bgr, bgi, lam)


def _rglru_sample_kernel(proj_ref, h0_ref, cst_ref, cw_ref, cb_ref, wg_ref, bgr_ref, bgi_ref, lam_ref,
                         hg_ref, hlast_ref, ucf, ucb, *, starts, win, steps, tb, d_rnn, cwid):
    def up(tp):
        if tp < cwid - 1:
            return cst_ref[tp]
        return proj_ref[tp - cwid + 1, :, d_rnn:]

    for t in range(steps):
        uc = cb_ref[...]
        for k in range(cwid):
            uc = uc + cw_ref[k:k + 1, :] * up(t + k)
        ucf[t * tb:(t + 1) * tb, :] = uc
        ucb[t * tb:(t + 1) * tb, :] = uc.astype(bf16)

    neg_c_sp = -LRU_C * _softplus(-lam_ref[...])
    for j in range(d_rnn // LANE):
        cs = slice(j * LANE, (j + 1) * LANE)
        g = jnp.dot(ucb[:, starts[j]:starts[j] + win], wg_ref[j], preferred_element_type=f32)
        h = h0_ref[:, cs]
        for t in range(steps):
            rows = slice(t * tb, (t + 1) * tb)
            a, x = _lru_inputs(g[rows], bgr_ref[:, cs], bgi_ref[:, cs], neg_c_sp[:, cs], ucf[rows, cs])
            h = a * h + x
            hg_ref[t, :, cs] = (h * _gelu(proj_ref[t, :, cs])).astype(hg_ref.dtype)
        hlast_ref[:, cs] = h


def _rglru_sample(proj, h0, conv_state, conv_w, conv_b, wg, bgr, bgi, lam, starts, win):
    steps, nb, _ = proj.shape
    d_rnn = proj.shape[2] // 2
    cwid = conv_w.shape[0]
    tb = min(SAMPLE_ROWS, nb)
    kern = functools.partial(_rglru_sample_kernel, starts=tuple(starts), win=win, steps=steps, tb=tb,
                             d_rnn=d_rnn, cwid=cwid)
    return pl.pallas_call(
        kern,
        grid=(nb // tb,),
        in_specs=[pl.BlockSpec((steps, tb, 2 * d_rnn), lambda i: (0, i, 0)),
                  pl.BlockSpec((tb, d_rnn), lambda i: (i, 0)),
                  pl.BlockSpec((cwid - 1, tb, d_rnn), lambda i: (0, i, 0)),
                  _full_spec((cwid, d_rnn)), _full_spec((1, d_rnn)),
                  _full_spec(wg.shape), _full_spec((1, d_rnn)), _full_spec((1, d_rnn)), _full_spec((1, d_rnn))],
        out_specs=[pl.BlockSpec((steps, tb, d_rnn), lambda i: (0, i, 0)),
                   pl.BlockSpec((tb, d_rnn), lambda i: (i, 0))],
        out_shape=[jax.ShapeDtypeStruct((steps, nb, d_rnn), bf16),
                   jax.ShapeDtypeStruct((nb, d_rnn), f32)],
        scratch_shapes=[pltpu.VMEM((steps * tb, d_rnn), f32), pltpu.VMEM((steps * tb, d_rnn), bf16)],
        compiler_params=_params("arbitrary"),
    )(proj, h0, conv_state, conv_w, conv_b, wg, bgr, bgi, lam)


def _first_max(vals, ids):
    m = jnp.max(vals, axis=-1, keepdims=True)
    idx = jnp.min(jnp.where(vals == m, ids, jnp.int32(2 ** 30)), axis=-1, keepdims=True)
    return m, idx


def _router_kernel(x_ref, gain_ref, shift_ref, scale_ref, w_ref, b_ref, *rest, n_groups, n_exp):
    xn_ref, rt_ref = rest[-2:]
    xn = _normmod(x_ref[...], gain_ref[...], shift_ref[...], scale_ref[...])
    for s in range(xn_ref.shape[1]):
        xn_ref[:, s, :] = xn[:, s * LANE:(s + 1) * LANE]
    xh = xn.astype(bf16)
    xl = (xn - xh.astype(f32)).astype(bf16)
    ph = jnp.dot(xh, w_ref[...], preferred_element_type=f32)
    pl_ = jnp.dot(xl, w_ref[...], preferred_element_type=f32)
    lg = (ph[:, :LANE] + pl_[:, :LANE] + ph[:, LANE:] + pl_[:, LANE:]) + b_ref[...]
    per = n_exp // n_groups
    lane = lax.broadcasted_iota(jnp.int32, lg.shape, 1)
    neg = -jnp.inf
    gl = jnp.where(lane < n_groups, lg, neg)
    gmax, g_idx = _first_max(gl, lane)
    g_w = 1.0 / jnp.sum(jnp.exp(gl - gmax), axis=-1, keepdims=True)
    e_id = lane - n_groups
    lo = g_idx * per
    el = jnp.where(e_id >= lo, jnp.where(e_id < lo + per, lg, neg), neg)
    l1, i1 = _first_max(el, e_id)
    el2 = jnp.where(e_id == i1, neg, el)
    l2, i2 = _first_max(el2, e_id)
    e = jnp.exp(l2 - l1)
    w1 = g_w / (1.0 + e)
    w2 = w1 * e
    rt_ref[...] = jnp.where(lane == 0, i1.astype(f32),
                            jnp.where(lane == 1, i2.astype(f32),
                                      jnp.where(lane == 2, w1, jnp.where(lane == 3, w2, 0.0))))


def _router(x, gain, shift, scale, wr, br, grp, n_groups, n_exp, n_total, row0, xn_all=None):
    d = x.shape[1]
    nr = wr.shape[1]
    wh = wr.astype(bf16)
    wl = (wr - wh.astype(f32)).astype(bf16)
    sh, sh_spec = _mod_operand(shift, grp)
    sc, sc_spec = _mod_operand(scale, grp)
    assert row0 % grp.tm == 0 and n_total % grp.tm == 0
    t0 = row0 // grp.tm
    n_tiles = grp.n // grp.tm
    n_steps = n_tiles if xn_all is not None else n_total // grp.tm

    def clamp(spec):
        return pl.BlockSpec(spec.block_shape, lambda i: spec.index_map(jnp.minimum(i, n_tiles - 1)))

    args = [x, gain.reshape(1, d), sh, sc, jnp.concatenate([wh, wl], axis=1), br]
    specs = [clamp(_row_spec(grp, d)), _full_spec((1, d)), clamp(sh_spec), clamp(sc_spec),
             _full_spec((d, 2 * nr)), _full_spec((1, nr))]
    aliases = {}
    if xn_all is not None:
        aliases = {len(args): 0}
        args.append(xn_all)
        specs.append(pl.BlockSpec(memory_space=pl.ANY))
    return pl.pallas_call(
        functools.partial(_router_kernel, n_groups=n_groups, n_exp=n_exp),
        grid=(n_steps,),
        in_specs=specs,
        out_specs=[pl.BlockSpec((grp.tm, d // LANE, LANE), lambda i: (i + t0, 0, 0)),
                   clamp(_row_spec(grp, nr))],
        out_shape=[jax.ShapeDtypeStruct((n_total, d // LANE, LANE), f32),
                   jax.ShapeDtypeStruct((grp.n, nr), f32)],
        input_output_aliases=aliases,
        compiler_params=_params("arbitrary"),
    )(*args)


def _experts_kernel(be_ref, nu_ref, st_ref, nx_ref, bc_ref, xn_hbm, w13_hbm, w2_hbm, o_ref,
                    xbuf, sem, w13f, w2f, wsem, w13b, w2b, *, d_exp, layer):
    i = pl.program_id(0)
    n_used = nu_ref[0]
    slot = i % 2
    e = be_ref[i]
    prev = be_ref[jnp.maximum(i - 1, 0)]
    n_slab = xbuf.shape[2]

    def row_copy(tok, sl, r):
        return pltpu.make_async_copy(xn_hbm.at[tok], xbuf.at[sl, r], sem.at[sl])

    def row_groups(blk, fn):
        for g0 in range(0, MOE_ROWS, MOE_GROUP):
            @pl.when(g0 < bc_ref[blk])
            def _(g0=g0):
                for r in range(g0, g0 + MOE_GROUP):
                    fn(r)

    def gather(blk, sl):
        row_groups(blk, lambda r: row_copy(st_ref[blk * MOE_ROWS + r], sl, r).start())

    def weight_copies(ex):
        return (pltpu.make_async_copy(w13_hbm.at[layer, ex], w13f, wsem.at[0]),
                pltpu.make_async_copy(w2_hbm.at[layer, ex], w2f, wsem.at[1]))

    @pl.when((i == 0) & (n_used > 0))
    def _():
        xbuf[...] = jnp.zeros_like(xbuf)
        gather(0, 0)
        for cp in weight_copies(e):
            cp.start(priority=1)

    @pl.when(i + 1 < n_used)
    def _():
        gather(i + 1, 1 - slot)

    @pl.when((i < n_used) & ((i == 0) | (e != prev)))
    def _():
        for cp in weight_copies(e):
            cp.wait()
        w13b[...] = w13f[...].astype(bf16)
        w2b[...] = w2f[...].astype(bf16)

        @pl.when(nx_ref[i] >= 0)
        def _():
            for cp in weight_copies(nx_ref[i]):
                cp.start(priority=1)

    @pl.when(i < n_used)
    def _():
        row_groups(i, lambda r: row_copy(0, slot, r).wait())
        x = jnp.concatenate([xbuf[slot, :, s, :] for s in range(n_slab)], axis=-1).astype(bf16)
        gu = jnp.dot(x, w13b[...], preferred_element_type=f32)
        g = gu[:, :d_exp]
        act = (g * _sigmoid(g)) * gu[:, d_exp:]
        o_ref[...] = jnp.dot(act.astype(bf16), w2b[...], preferred_element_type=f32)

    @pl.when(i >= n_used)
    def _():
        o_ref[...] = jnp.zeros_like(o_ref)


def _experts(xn, slot_tok, block_expert, n_used, next_expert, block_rows, w13, w2, layer):
    n_slab = xn.shape[1]
    d = n_slab * LANE
    n_blocks = slot_tok.shape[0] // MOE_ROWS
    d_exp = w2.shape[2]
    gs = pltpu.PrefetchScalarGridSpec(
        num_scalar_prefetch=5,
        grid=(n_blocks,),
        in_specs=[pl.BlockSpec(memory_space=pl.ANY), pl.BlockSpec(memory_space=pl.ANY),
                  pl.BlockSpec(memory_space=pl.ANY)],
        out_specs=pl.BlockSpec((MOE_ROWS, d), lambda i, *_: (i, 0)),
        scratch_shapes=[pltpu.VMEM((2, MOE_ROWS, n_slab, LANE), f32), pltpu.SemaphoreType.DMA((2,)),
                        pltpu.VMEM((d, 2 * d_exp), f32), pltpu.VMEM((d_exp, d), f32),
                        pltpu.SemaphoreType.DMA((2,)),
                        pltpu.VMEM((d, 2 * d_exp), bf16), pltpu.VMEM((d_exp, d), bf16)],
    )
    return pl.pallas_call(
        functools.partial(_experts_kernel, d_exp=d_exp, layer=layer),
        grid_spec=gs,
        out_shape=jax.ShapeDtypeStruct((n_blocks * MOE_ROWS, d), f32),
        compiler_params=_params("arbitrary"),
    )(block_expert, n_used, slot_tok, next_expert, block_rows, xn, w13, w2)


def _combine_kernel(x_ref, ya_ref, yb_ref, wt_ref, gate_ref, *rest, final):
    if final:
        fg_ref, o_ref = rest
    else:
        (o_ref,) = rest
    wt = wt_ref[...]
    y = x_ref[...] + gate_ref[...] * (wt[:, 0:1] * ya_ref[...] + wt[:, 1:2] * yb_ref[...])
    if final:
        y = _rms(y, fg_ref[...])
    o_ref[...] = y


def _combine(x, ya, yb, wts, gate, grp, final_gain=None):
    d = x.shape[1]
    g_arr, g_spec = _mod_operand(gate, grp)
    final = final_gain is not None
    args = [x, ya, yb, wts, g_arr]
    specs = [_row_spec(grp, d), _row_spec(grp, d), _row_spec(grp, d), _row_spec(grp, TOP_K), g_spec]
    if final:
        args.append(final_gain.reshape(1, d))
        specs.append(_full_spec((1, d)))
    return pl.pallas_call(
        functools.partial(_combine_kernel, final=final),
        grid=(grp.n // grp.tm,),
        in_specs=specs,
        out_specs=_row_spec(grp, d),
        out_shape=jax.ShapeDtypeStruct((grp.n, d), f32),
        compiler_params=_params("arbitrary"),
    )(*args)


def _dispatch(expert_ids, n_exp):
    n = expert_ids.shape[0]
    na = n * TOP_K
    flat_e = expert_ids.reshape(na)
    onehot = (flat_e[:, None] == jnp.arange(n_exp, dtype=jnp.int32)[None, :]).astype(jnp.int32)
    incl = jnp.cumsum(onehot, axis=0)
    counts = incl[-1]
    padded = (counts + MOE_ROWS - 1) // MOE_ROWS * MOE_ROWS
    pad_end = jnp.cumsum(padded)
    pad_start = pad_end - padded
    dest = jnp.sum(onehot * (pad_start[None, :] + incl - 1), axis=1)
    n_blocks = (na + MOE_ROWS - 1) // MOE_ROWS + n_exp
    block_start = jnp.arange(n_blocks, dtype=jnp.int32) * MOE_ROWS
    block_expert = jnp.minimum(jnp.sum((pad_end[None, :] <= block_start[:, None]).astype(jnp.int32), axis=1),
                               n_exp - 1)
    n_used = pad_end[-1:] // MOE_ROWS
    bidx = jnp.arange(n_blocks, dtype=jnp.int32)
    is_first = (bidx == 0) | (block_expert != jnp.roll(block_expert, 1))
    pos = jnp.where(is_first & (bidx < n_used[0]), bidx, n_blocks)
    nxt = jnp.min(jnp.where(bidx[None, :] > bidx[:, None], pos[None, :], n_blocks), axis=1)
    next_expert = jnp.where(nxt < n_blocks, block_expert[jnp.minimum(nxt, n_blocks - 1)], -1)
    sel = (block_expert[:, None] == jnp.arange(n_exp, dtype=jnp.int32)[None, :]).astype(jnp.int32)
    real_end = jnp.sum(sel * (pad_start + counts)[None, :], axis=1)
    block_rows = jnp.clip(real_end - block_start, 0, MOE_ROWS)
    tok = jnp.arange(na, dtype=jnp.int32) // TOP_K
    slot_tok = jnp.zeros((n_blocks * MOE_ROWS,), jnp.int32).at[dest].set(tok)
    return slot_tok, block_expert, n_used, next_expert, block_rows, dest.reshape(n, TOP_K)


def _kv_kernel(x_ref, gain_ref, shift_ref, scale_ref, w_ref, lg_ref, cos_ref, sin_ref, *rest,
               r_kv, half, with_up):
    if with_up:
        wup_ref, ckv_ref, kpe_ref, kn_ref, v_ref, kpeb_ref = rest
    else:
        ckv_ref, kpe_ref = rest
    a = _normmod(x_ref[...], gain_ref[...], shift_ref[...], scale_ref[...]).astype(bf16)
    kv = jnp.dot(a, w_ref[...], preferred_element_type=f32)
    c = _rms(kv[:, :r_kv], lg_ref[...])
    ckv_ref[...] = c
    kp = _rope_tile(kv[:, r_kv:], cos_ref[...], sin_ref[...], half)
    kpe_ref[...] = kp
    if with_up:
        up = jnp.dot(c.astype(bf16), wup_ref[...], preferred_element_type=f32)
        hw = up.shape[1] // 2
        kn_ref[...] = up[:, :hw].astype(bf16)
        v_ref[...] = up[:, hw:].astype(bf16)
        kpeb_ref[...] = kp.astype(bf16)


def _kv_latent(x, gain, shift, scale, w_pad, latent_gain, cos, sin, grp, half, w_up=None):
    d = x.shape[1]
    r_kv = latent_gain.shape[0]
    with_up = w_up is not None
    sh, sh_spec = _mod_operand(shift, grp)
    sc, sc_spec = _mod_operand(scale, grp)
    cs, cs_spec = _pos_operand(cos, grp)
    sn, sn_spec = _pos_operand(sin, grp)
    args = [x, gain.reshape(1, d), sh, sc, w_pad, latent_gain.reshape(1, r_kv), cs, sn]
    specs = [_row_spec(grp, d), _full_spec((1, d)), sh_spec, sc_spec, _full_spec(w_pad.shape),
             _full_spec((1, r_kv)), cs_spec, sn_spec]
    out_specs = [_row_spec(grp, r_kv), _row_spec(grp, LANE)]
    out_shape = [jax.ShapeDtypeStruct((grp.n, r_kv), f32), jax.ShapeDtypeStruct((grp.n, LANE), f32)]
    if with_up:
        hw = w_up.shape[1] // 2
        args.append(w_up)
        specs.append(_full_spec(w_up.shape))
        out_specs += [_row_spec(grp, hw), _row_spec(grp, hw), _row_spec(grp, LANE)]
        out_shape += [jax.ShapeDtypeStruct((grp.n, hw), bf16), jax.ShapeDtypeStruct((grp.n, hw), bf16),
                      jax.ShapeDtypeStruct((grp.n, LANE), bf16)]
    return pl.pallas_call(
        functools.partial(_kv_kernel, r_kv=r_kv, half=half, with_up=with_up),
        grid=(grp.n // grp.tm,),
        in_specs=specs, out_specs=out_specs, out_shape=out_shape,
        compiler_params=_params("arbitrary"),
    )(*args)


def _q_kernel(x_ref, gain_ref, shift_ref, scale_ref, wdq_ref, qg_ref, wq_ref, cos_ref, sin_ref, q_ref,
              *, n_heads, half):
    a = _normmod(x_ref[...], gain_ref[...], shift_ref[...], scale_ref[...]).astype(bf16)
    ql = jnp.dot(a, wdq_ref[...], preferred_element_type=f32)
    qn = _rms(ql, qg_ref[...]).astype(bf16)
    cos = cos_ref[...]
    sin = sin_ref[...]
    for h in range(n_heads):
        q = jnp.dot(qn, wq_ref[:, 2 * LANE * h:2 * LANE * (h + 1)], preferred_element_type=f32)
        q_ref[:, 2 * LANE * h:2 * LANE * h + LANE] = q[:, :LANE].astype(bf16)
        q_ref[:, 2 * LANE * h + LANE:2 * LANE * (h + 1)] = _rope_tile(q[:, LANE:], cos, sin, half).astype(bf16)


def _q_proj(x, gain, shift, scale, w_dq, q_gain, wq, cos, sin, grp, n_heads, half):
    d = x.shape[1]
    rq = w_dq.shape[1]
    sh, sh_spec = _mod_operand(shift, grp)
    sc, sc_spec = _mod_operand(scale, grp)
    cs, cs_spec = _pos_operand(cos, grp)
    sn, sn_spec = _pos_operand(sin, grp)
    return pl.pallas_call(
        functools.partial(_q_kernel, n_heads=n_heads, half=half),
        grid=(grp.n // grp.tm,),
        in_specs=[_row_spec(grp, d), _full_spec((1, d)), sh_spec, sc_spec, _full_spec(w_dq.shape),
                  _full_spec((1, rq)), _full_spec(wq.shape), cs_spec, sn_spec],
        out_specs=_row_spec(grp, wq.shape[1]),
        out_shape=jax.ShapeDtypeStruct((grp.n, wq.shape[1]), bf16),
        compiler_params=_params("arbitrary"),
    )(x, gain.reshape(1, d), sh, sc, w_dq, q_gain.reshape(1, rq), wq, cs, sn)


def _attn_kernel(q_ref, kn_ref, v_ref, kpe_ref, o_ref, m_scr, acc_scr, *, hg, tq, c):
    qi = pl.program_id(2)
    m_scr[...] = jnp.full_like(m_scr, -jnp.inf)
    acc_scr[...] = jnp.zeros_like(acc_scr)
    ones = jnp.ones((tq, LANE), bf16)
    n_lt = tq // LANE

    def block(ks, masked):
        kpe = kpe_ref[pl.ds(ks, tq), :]
        for h in range(hg):
            hs = slice(h * LANE, (h + 1) * LANE)
            k = jnp.concatenate([kn_ref[pl.ds(ks, tq), hs], kpe], axis=-1)
            s = lax.dot_general(q_ref[:, 2 * LANE * h:2 * LANE * (h + 1)], k, (((1,), (1,)), ((), ())),
                                preferred_element_type=f32)
            if masked:
                row = lax.broadcasted_iota(jnp.int32, (tq, tq), 0)
                col = lax.broadcasted_iota(jnp.int32, (tq, tq), 1)
                s = jnp.where(col <= row, s, -jnp.inf)
            tiles = [s[:, t * LANE:(t + 1) * LANE] for t in range(n_lt)]
            mx = tiles[0]
            for t in tiles[1:]:
                mx = jnp.maximum(mx, t)
            m = m_scr[h]
            m_new = jnp.maximum(m, jnp.max(mx, axis=-1, keepdims=True))
            p = jnp.concatenate([jnp.exp2((t - m_new) * c) for t in tiles], axis=-1).astype(bf16)
            corr = jnp.exp2((m - m_new) * c)
            v_aug = jnp.concatenate([v_ref[pl.ds(ks, tq), hs], ones], axis=-1)
            pv = jnp.dot(p, v_aug, preferred_element_type=f32)
            acc_scr[h, :, :LANE] = acc_scr[h, :, :LANE] * corr + pv[:, :LANE]
            acc_scr[h, :, LANE:] = acc_scr[h, :, LANE:] * corr + pv[:, LANE:]
            m_scr[h] = m_new

    def body(kb, carry):
        block(pl.multiple_of(kb * tq, tq), False)
        return carry

    lax.fori_loop(0, qi, body, 0)
    block(pl.multiple_of(qi * tq, tq), True)
    for h in range(hg):
        o_ref[:, h * LANE:(h + 1) * LANE] = (acc_scr[h, :, :LANE] * (1.0 / acc_scr[h, :, LANE:])).astype(o_ref.dtype)


def _attn_prompt(q, kn, v, kpe, nb_seq, seq, n_heads, scale, tq, hg):
    nq = seq // tq
    return pl.pallas_call(
        functools.partial(_attn_kernel, hg=hg, tq=tq, c=scale * LOG2E),
        grid=(nb_seq, n_heads // hg, nq),
        in_specs=[pl.BlockSpec((tq, hg * 2 * LANE), lambda b, g, i: (b * nq + i, g)),
                  pl.BlockSpec((seq, hg * LANE), lambda b, g, i: (b, g)),
                  pl.BlockSpec((seq, hg * LANE), lambda b, g, i: (b, g)),
                  pl.BlockSpec((seq, LANE), lambda b, g, i: (b, 0))],
        out_specs=pl.BlockSpec((tq, hg * LANE), lambda b, g, i: (b * nq + i, g)),
        out_shape=jax.ShapeDtypeStruct((nb_seq * seq, n_heads * LANE), bf16),
        scratch_shapes=[pltpu.VMEM((hg, tq, LANE), f32), pltpu.VMEM((hg, tq, 2 * LANE), f32)],
        compiler_params=_params("arbitrary", "arbitrary", "arbitrary"),
    )(q, kn, v, kpe)


def _head_matmul_kernel(a_ref, w_ref, o_ref):
    o_ref[...] = jnp.dot(a_ref[...].astype(bf16), w_ref[...].astype(bf16),
                         preferred_element_type=f32).astype(o_ref.dtype)


def _head_matmul(a, a_spec, w, w_spec, out_shape, out_spec, n_heads):
    return pl.pallas_call(
        _head_matmul_kernel,
        grid=(n_heads,),
        in_specs=[a_spec, w_spec],
        out_specs=out_spec,
        out_shape=out_shape,
        compiler_params=_params("arbitrary"),
    )(a, w)


def _decode_kernel(pt_ref, qlat_ref, qpe_ref, cnew_ref, knew_ref, cache_c, cache_k, o_ref,
                   cbuf, kbuf, sem, m_scr, l_scr, acc_scr, *, bp, ppc, n_chunks, n_pages, steps, c):
    g = pl.program_id(0)
    ch = pl.program_id(1)
    step = g * n_chunks + ch
    slot = step % 2
    total = pl.num_programs(0) * n_chunks
    page = cbuf.shape[3]
    rope = kbuf.shape[2]

    def copies(gg, cc, sl, bi, p):
        phys = pt_ref[(gg * bp + bi) * n_pages + cc * ppc + p]
        return (pltpu.make_async_copy(cache_c.at[phys], cbuf.at[sl, bi, p], sem.at[0, sl]),
                pltpu.make_async_copy(cache_k.at[phys], kbuf.at[sl, bi, :, pl.ds(p * page, page)], sem.at[1, sl]))

    def issue(gg, cc, sl):
        for bi in range(bp):
            for p in range(ppc):
                for cp in copies(gg, cc, sl, bi, p):
                    cp.start()

    @pl.when(step == 0)
    def _():
        issue(0, 0, 0)

    @pl.when(step + 1 < total)
    def _():
        nxt = step + 1
        issue(nxt // n_chunks, nxt % n_chunks, 1 - slot)

    for bi in range(bp):
        for p in range(ppc):
            for cp in copies(g, ch, slot, bi, p):
                cp.wait()

    @pl.when(ch == 0)
    def _():
        m_scr[...] = jnp.full_like(m_scr, -jnp.inf)
        l_scr[...] = jnp.zeros_like(l_scr)
        acc_scr[...] = jnp.zeros_like(acc_scr)

    nt = (((1,), (1,)), ((), ()))

    def update(state, s, vals):
        m, l, acc = state
        m_new = jnp.maximum(m, jnp.max(s, axis=-1, keepdims=True))
        p = jnp.exp2((s - m_new) * c)
        corr = jnp.exp2((m - m_new) * c)
        l = l * corr + jnp.sum(p, axis=-1, keepdims=True)
        acc = acc * corr + jnp.dot(p.astype(bf16), vals, preferred_element_type=f32)
        return m_new, l, acc

    states = [(m_scr[bi], l_scr[bi], acc_scr[bi]) for bi in range(bp)]
    for bi in range(bp):
        ck = cbuf[slot, bi].reshape(ppc * page, cbuf.shape[4]).astype(bf16)
        s = (lax.dot_general(qlat_ref[bi], ck, nt, preferred_element_type=f32)
             + jnp.dot(qpe_ref[bi, :, :rope], kbuf[slot, bi].astype(bf16), preferred_element_type=f32))
        states[bi] = update(states[bi], s, ck)
    for bi in range(bp):
        m_scr[bi], l_scr[bi], acc_scr[bi] = states[bi]

    @pl.when(ch == n_chunks - 1)
    def _():
        for bi in range(bp):
            cn = cnew_ref[bi]
            sn = (lax.dot_general(qlat_ref[bi], cn, nt, preferred_element_type=f32)
                  + lax.dot_general(qpe_ref[bi], knew_ref[bi], nt, preferred_element_type=f32))
            t_row = lax.broadcasted_iota(jnp.int32, sn.shape, 0) % steps
            key = lax.broadcasted_iota(jnp.int32, sn.shape, 1)
            _, l, acc = update(states[bi], jnp.where(key <= t_row, sn, -jnp.inf), cn)
            o_ref[bi] = (acc * (1.0 / l)).astype(o_ref.dtype)


def _decode_attn(page_table, qlat, qpe, cnew, knew, cache_c, cache_k, scale, steps, ppc, bp):
    nb, rows, r_kv = qlat.shape
    n_pages = page_table.shape[1]
    page = cache_c.shape[1]
    rope = cache_k.shape[1]
    n_chunks = n_pages // ppc

    def seq_spec(a, b):
        return pl.BlockSpec((bp, a, b), lambda g, ch, pt: (g, 0, 0))

    gs = pltpu.PrefetchScalarGridSpec(
        num_scalar_prefetch=1,
        grid=(nb // bp, n_chunks),
        in_specs=[seq_spec(rows, r_kv), seq_spec(rows, LANE), seq_spec(LANE, r_kv), seq_spec(LANE, LANE),
                  pl.BlockSpec(memory_space=pl.ANY),
                  pl.BlockSpec(memory_space=pl.ANY)],
        out_specs=seq_spec(rows, r_kv),
        scratch_shapes=[pltpu.VMEM((2, bp, ppc, page, r_kv), f32),
                        pltpu.VMEM((2, bp, rope, ppc * page), f32),
                        pltpu.SemaphoreType.DMA((2, 2)),
                        pltpu.VMEM((bp, rows, 1), f32), pltpu.VMEM((bp, rows, 1), f32),
                        pltpu.VMEM((bp, rows, r_kv), f32)],
    )
    return pl.pallas_call(
        functools.partial(_decode_kernel, bp=bp, ppc=ppc, n_chunks=n_chunks, n_pages=n_pages, steps=steps,
                          c=scale * LOG2E),
        grid_spec=gs,
        out_shape=jax.ShapeDtypeStruct((nb, rows, r_kv), bf16),
        compiler_params=_params("arbitrary", "arbitrary"),
    )(page_table.reshape(-1), qlat, qpe, cnew, knew, cache_c, cache_k)


def _rope_tables(pos, rope):
    half = rope // 2
    inv_freq = ROPE_THETA ** (-jnp.arange(half, dtype=f32) / half)
    ang = pos.astype(f32)[:, None] * inv_freq
    cos, sin = jnp.cos(ang), jnp.sin(ang)
    pad = jnp.zeros((pos.shape[0], LANE - rope), f32)
    return (jnp.concatenate([cos, cos, pad], axis=-1), jnp.concatenate([-sin, sin, pad], axis=-1))


def kernel(x_prompt, x_sample, cache_ckv, cache_kpe, state_conv, state_rglru, page_table, c_prompt, c_sample, mod_w, mod_b, mix_norm, ffn_norm, rg_w_in, rg_conv_w, rg_conv_b, rg_w_gates, rg_b_gates, rg_lambda, rg_w_out, kv_mod_w, kv_mod_b, kv_norm, kv_w_dkv, kv_latent_norm, kv_w_uk, kv_w_uv, mla_w_dq, mla_q_norm, mla_w_uq, mla_w_o, moe_w_group, moe_b_group, moe_w_expert, moe_b_expert, moe_w13, moe_w2, final_norm):
    bp, seq, d = x_prompt.shape
    bs, steps, _ = x_sample.shape
    d_rnn = rg_conv_w.shape[-1]
    cwid = rg_conv_w.shape[1]
    nb_rnn, rnn_blk = rg_w_gates.shape[1], rg_w_gates.shape[2]
    r_kv, n_heads, d_nope = kv_w_uk.shape
    d_v = kv_w_uv.shape[2]
    rope = cache_kpe.shape[-1]
    half = rope // 2
    page = cache_ckv.shape[1]
    past = page_table.shape[1] * page
    n_groups = moe_w_group.shape[-1]
    n_exp = moe_w_expert.shape[-1]
    scale = 1.0 / math.sqrt(d_nope + rope)
    assert d_nope == LANE and d_v == LANE and rope <= LANE and d_rnn % LANE == 0
    assert rg_w_in.shape[0] == 1 and mla_w_dq.shape[0] == 1 and mod_w.shape[0] == 2
    assert steps >= cwid - 1 and seq >= cwid - 1

    n_p, n_s = bp * seq, steps * bs
    gp = _Group(n_p, min(512, seq), seq, False)
    gs_ = _Group(n_s, min(512, n_s), 0, True)
    groups = (gp, gs_)

    xp = x_prompt.reshape(n_p, d)
    xs = jnp.swapaxes(x_sample, 0, 1).reshape(n_s, d)

    n_c = bp + bs
    n_c_pad = -(-n_c // SUBLANE) * SUBLANE
    c_all = jnp.concatenate([c_prompt, c_sample, jnp.zeros((n_c_pad - n_c, d), f32)], axis=0)
    mods = [_bias_matmul(c_all, mod_w, layer, mod_b[layer]) for layer in range(2)]
    kv_mods = _bias_matmul(c_all, kv_mod_w[None], 0, kv_mod_b)

    def mod_vec(m, idx, grp_i):
        rows = slice(0, bp) if grp_i == 0 else slice(bp, bp + bs)
        return m[rows, idx * d:(idx + 1) * d]

    w_in = rg_w_in[0].astype(bf16)
    w_out = rg_w_out[0].astype(bf16)
    starts, win = _gate_windows(d_rnn, rnn_blk)
    wg = _gate_weights(rg_w_gates[0], starts, win)
    bgr = rg_b_gates[0][:, :rnn_blk].reshape(1, d_rnn)
    bgi = rg_b_gates[0][:, rnn_blk:].reshape(1, d_rnn)
    conv_w = rg_conv_w[0]
    conv_b = rg_conv_b[0].reshape(1, d_rnn)
    lam = rg_lambda[0].reshape(1, d_rnn)

    w_dkv = jnp.concatenate([kv_w_dkv, jnp.zeros((d, LANE - rope), f32)], axis=1).astype(bf16)
    w_up = jnp.concatenate([kv_w_uk.reshape(r_kv, n_heads * d_nope), kv_w_uv.reshape(r_kv, n_heads * d_v)],
                           axis=1).astype(bf16)
    w_dq = mla_w_dq[0].astype(bf16)
    wq3 = mla_w_uq[0].reshape(-1, n_heads, d_nope + rope)
    wq = jnp.concatenate([wq3, jnp.zeros((wq3.shape[0], n_heads, LANE - rope), f32)], axis=-1)
    wq = wq.reshape(-1, n_heads * 2 * LANE).astype(bf16)
    w_o = mla_w_o[0].astype(bf16)
    w_uk_t = jnp.transpose(kv_w_uk, (1, 2, 0))
    w_uv_flat = kv_w_uv.reshape(r_kv, n_heads * d_v)

    cos_p, sin_p = _rope_tables(jnp.arange(seq, dtype=jnp.int32), rope)
    cos_s, sin_s = _rope_tables(past + jnp.arange(steps, dtype=jnp.int32), rope)
    tabs = ((cos_p, sin_p), (cos_s, sin_s))

    def moe(xs_in, layer):
        rt_l = []
        n_pad = LANE - n_groups - n_exp
        wr = jnp.concatenate([moe_w_group[layer], moe_w_expert[layer], jnp.zeros((d, n_pad), f32)], axis=1)
        br = jnp.concatenate([moe_b_group[layer], moe_b_expert[layer], jnp.zeros((n_pad,), f32)]).reshape(1, LANE)
        xn_all, row0 = None, 0
        for gi, grp in enumerate(groups):
            xn_all, rt = _router(xs_in[gi], ffn_norm[layer], mod_vec(mods[layer], 3, gi),
                                 mod_vec(mods[layer], 4, gi), wr, br, grp._replace(tm=min(ROUTER_ROWS, grp.tm)),
                                 n_groups, n_exp, n_p + n_s, row0, xn_all)
            row0 += grp.n
            rt_l.append(rt)
        route = jnp.concatenate(rt_l, axis=0)
        expert_ids = route[:, :TOP_K].astype(jnp.int32)
        weights = route[:, TOP_K:2 * TOP_K]
        slot_tok, block_expert, n_used, next_expert, block_rows, dest = _dispatch(expert_ids, n_exp)
        yb = _experts(xn_all, slot_tok, block_expert, n_used, next_expert, block_rows, moe_w13, moe_w2, layer)
        out = []
        lo = 0
        for grp in groups:
            dg = dest[lo:lo + grp.n]
            out.append((yb[dg[:, 0]], yb[dg[:, 1]], weights[lo:lo + grp.n]))
            lo += grp.n
        return out

    x_cur = [xp, xs]
    projs, h_last = [], []
    for gi, grp in enumerate(groups):
        proj = _normmod_matmul(x_cur[gi], mix_norm[0], mod_vec(mods[0], 0, gi), mod_vec(mods[0], 1, gi),
                               w_in, grp, tn_cap=2688)
        projs.append(proj)
        if gi == 0:
            hg, hl = _rglru_prompt(proj, jnp.zeros((bp, 1, d_rnn), f32), jnp.zeros((bp, SUBLANE, d_rnn), f32),
                                   conv_w, conv_b, wg, bgr, bgi, lam, starts, win, bp, seq, min(RNN_ROWS, seq))
            hl = hl.reshape(bp, d_rnn)
        else:
            cst = jnp.swapaxes(state_conv[0], 0, 1)
            hg, hl = _rglru_sample(proj.reshape(steps, bs, 2 * d_rnn), state_rglru[0], cst, conv_w, conv_b,
                                   wg, bgr, bgi, lam, starts, win)
            hg = hg.reshape(n_s, d_rnn)
        h_last.append(hl)
        x_cur[gi] = _matmul_res(hg, w_out, x_cur[gi], mod_vec(mods[0], 2, gi), grp)
    moe_out = moe(x_cur, 0)
    for gi, grp in enumerate(groups):
        ya, yc, wts = moe_out[gi]
        x_cur[gi] = _combine(x_cur[gi], ya, yc, wts, mod_vec(mods[0], 5, gi), grp)

    kv_out = []
    for gi, grp in enumerate(groups):
        kv_out.append(_kv_latent(x_cur[gi], kv_norm, kv_mods[:, :d][(slice(0, bp) if gi == 0 else slice(bp, bp + bs))],
                                 kv_mods[:, d:][(slice(0, bp) if gi == 0 else slice(bp, bp + bs))],
                                 w_dkv, kv_latent_norm, tabs[gi][0], tabs[gi][1], grp, half,
                                 w_up=w_up if gi == 0 else None))
    qs = [_q_proj(x_cur[gi], mix_norm[1], mod_vec(mods[1], 0, gi), mod_vec(mods[1], 1, gi), w_dq,
                  mla_q_norm[0], wq, tabs[gi][0], tabs[gi][1], grp, n_heads, half)
          for gi, grp in enumerate(groups)]

    ckv_p, kpe_p, kn_p, v_p, kpeb_p = kv_out[0]
    tq = min(ATTN_ROWS, seq)
    hgrp = 4 if n_heads % 4 == 0 else 1
    attn_p = _attn_prompt(qs[0], kn_p, v_p, kpeb_p, bp, seq, n_heads, scale, tq, hgrp)

    ckv_s, kpe_s = kv_out[1]
    rows = n_heads * steps
    qlat = _head_matmul(
        qs[1], pl.BlockSpec((n_s, LANE), lambda h: (0, 2 * h)),
        w_uk_t, pl.BlockSpec((None, d_nope, r_kv), lambda h: (h, 0, 0)),
        jax.ShapeDtypeStruct((n_heads, n_s, r_kv), bf16),
        pl.BlockSpec((None, n_s, r_kv), lambda h: (h, 0, 0)), n_heads)
    qlat = jnp.transpose(qlat.reshape(n_heads, steps, bs, r_kv), (2, 0, 1, 3)).reshape(bs, rows, r_kv)
    qpe = qs[1].reshape(steps, bs, n_heads, 2, LANE)[:, :, :, 1]
    qpe = jnp.transpose(qpe, (1, 2, 0, 3)).reshape(bs, rows, LANE)

    def new_keys(a):
        a = jnp.swapaxes(a.reshape(steps, bs, a.shape[-1]), 0, 1).astype(bf16)
        return jnp.concatenate([a, jnp.zeros((bs, LANE - steps, a.shape[-1]), bf16)], axis=1)

    ppc = _pick_tile(page_table.shape[1], DECODE_PAGES, 1)
    olat = _decode_attn(page_table, qlat, qpe, new_keys(ckv_s), new_keys(kpe_s), cache_ckv,
                        jnp.swapaxes(cache_kpe, 1, 2), scale, steps, ppc,
                        DECODE_BATCH if bs % DECODE_BATCH == 0 else 1)
    olat = jnp.transpose(olat.reshape(bs, n_heads, steps, r_kv), (1, 2, 0, 3)).reshape(n_heads, n_s, r_kv)
    attn_s = _head_matmul(
        olat, pl.BlockSpec((None, n_s, r_kv), lambda h: (h, 0, 0)),
        w_uv_flat, pl.BlockSpec((r_kv, d_v), lambda h: (0, h)),
        jax.ShapeDtypeStruct((n_s, n_heads * d_v), bf16),
        pl.BlockSpec((n_s, d_v), lambda h: (0, h)), n_heads)

    attn = [attn_p, attn_s]
    for gi, grp in enumerate(groups):
        x_cur[gi] = _matmul_res(attn[gi], w_o, x_cur[gi], mod_vec(mods[1], 2, gi), grp)
    moe_out = moe(x_cur, 1)
    ys = []
    for gi, grp in enumerate(groups):
        ya, yc, wts = moe_out[gi]
        ys.append(_combine(x_cur[gi], ya, yc, wts, mod_vec(mods[1], 5, gi), grp, final_gain=final_norm))

    y_prompt = ys[0].reshape(bp, seq, d)
    y_sample = jnp.swapaxes(ys[1].reshape(steps, bs, d), 0, 1)
    conv_p = projs[0].reshape(bp, seq, 2 * d_rnn)[:, seq - (cwid - 1):, d_rnn:][None]
    conv_s = jnp.swapaxes(projs[1].reshape(steps, bs, 2 * d_rnn)[steps - (cwid - 1):, :, d_rnn:], 0, 1)[None]
    h_p = h_last[0][None]
    h_s = h_last[1][None]
    ckv_prompt = ckv_p.reshape(bp, seq, r_kv)
    kpe_prompt = kpe_p[:, :rope].reshape(bp, seq, rope)
    ckv_sample = jnp.swapaxes(ckv_s.reshape(steps, bs, r_kv), 0, 1)
    kpe_sample = jnp.swapaxes(kpe_s[:, :rope].reshape(steps, bs, rope), 0, 1)
    return (y_prompt, y_sample, conv_p, h_p, ckv_prompt, kpe_prompt, conv_s, h_s, ckv_sample, kpe_sample)
```
